```python
import jax, jax.numpy as jnp
from jax import lax
import numpy as np

D_MODEL = 1024
BATCH = 8
SEQ = 2048
DEPTH = 2

CTX_LEN = 256
GRID_W = 64

N_MOD = 6
DEEPNORM_ALPHA = (2.0 * DEPTH) ** 0.25
DEEPNORM_BETA = (8.0 * DEPTH) ** -0.25
LN_EPS = 1e-5
RMS_EPS = 1e-6

MLA_V = 64
MLA_HEADS = D_MODEL // (2 * MLA_V)
MLA_NOPE = 64
MLA_ROPE = 32
MLA_Q_RANK = 384
MLA_KV_RANK = 256
ROPE_AXIS = MLA_ROPE // 2
ROPE_THETA = 10000.0
Q_BLOCK = 128

RWKV_HEAD = 64
RWKV_HEADS = D_MODEL // (2 * RWKV_HEAD)
RWKV_W = RWKV_HEADS * RWKV_HEAD
DECAY_LORA = 64
AAA_LORA = 64
GATE_LORA = 128
GN_EPS = 64e-5
RWKV_IN = 3 * RWKV_W + 2 * DECAY_LORA + 2 * AAA_LORA + GATE_LORA
RWKV_SPLIT = (RWKV_W, 2 * RWKV_W, 3 * RWKV_W, 3 * RWKV_W + 2 * DECAY_LORA, 3 * RWKV_W + 2 * DECAY_LORA + 2 * AAA_LORA)

EVEN_SPLIT = (MLA_Q_RANK, MLA_Q_RANK + MLA_KV_RANK, MLA_Q_RANK + MLA_KV_RANK + MLA_ROPE)
EVEN_IN = EVEN_SPLIT[-1] + RWKV_IN

LRU_WIDTH = D_MODEL
LRU_BLOCKS = 8
LRU_BLOCK = LRU_WIDTH // LRU_BLOCKS
LRU_C = 8.0
CONV_W = 4

D_FF = 4 * D_MODEL

kernel_name = 'hybrid_mla_rwkv7_rglru_diffusion_trunk'


def layer_norm(x, g, b):
    xf = x.astype(jnp.float32)
    mu = jnp.mean(xf, axis=-1, keepdims=True)
    var = jnp.mean(jnp.square(xf - mu), axis=-1, keepdims=True)
    return ((xf - mu) * lax.rsqrt(var + LN_EPS)).astype(x.dtype) * g + b


def rms_norm(x, g):
    xf = x.astype(jnp.float32)
    return (xf * lax.rsqrt(jnp.mean(jnp.square(xf), axis=-1, keepdims=True) + RMS_EPS)).astype(x.dtype) * g


def modulate(x, shift, scale):
    return x * (1.0 + scale) + shift


def squared_relu_mlp(h, w1, w2):
    return jnp.square(jax.nn.relu(h @ w1)) @ w2


def axial_rope_tables(n):
    rows_n = n // GRID_W
    rows = jnp.repeat(jnp.arange(rows_n, dtype=jnp.float32), GRID_W)
    cols = jnp.tile(jnp.arange(GRID_W, dtype=jnp.float32), rows_n)
    inv_freq = ROPE_THETA ** (-jnp.arange(0, ROPE_AXIS, 2, dtype=jnp.float32) / ROPE_AXIS)
    ang_r = rows[:, None] * inv_freq
    ang_c = cols[:, None] * inv_freq
    ang = jnp.concatenate([ang_r, ang_r, ang_c, ang_c], axis=-1)
    return jnp.cos(ang), jnp.sin(ang)


def rotate_half_axial(x):
    xs = x.reshape(x.shape[:-1] + (2, 2, ROPE_AXIS // 2))
    x1, x2 = xs[..., 0, :], xs[..., 1, :]
    return jnp.stack([-x2, x1], axis=-2).reshape(x.shape)


def apply_rope(x, cos, sin):
    return x * cos.astype(x.dtype) + rotate_half_axial(x) * sin.astype(x.dtype)


def softmax_attend(q, k, v):
    s = jnp.einsum('bqhd,bkhd->bhqk', q, k).astype(jnp.float32) * (MLA_NOPE + MLA_ROPE) ** -0.5
    p = jax.nn.softmax(s, axis=-1).astype(v.dtype)
    return jnp.einsum('bhqk,bkhd->bqhd', p, v)


def blocked_attend(q, k, v):
    b, n, h, dk = q.shape
    qb = q.reshape(b, n // Q_BLOCK, Q_BLOCK, h, dk).transpose(1, 0, 2, 3, 4)
    out = lax.map(lambda blk: softmax_attend(blk, k, v), qb)
    return out.transpose(1, 0, 2, 3, 4).reshape(b, n, h, v.shape[-1])


def mla_features(f_q, f_kv, f_kr, q_norm, w_uq, kv_norm, w_uk, w_uv, rope):
    b, n = f_q.shape[:2]
    q = (rms_norm(f_q, q_norm) @ w_uq).reshape(b, n, MLA_HEADS, MLA_NOPE + MLA_ROPE)
    q_nope, q_rope = q[..., :MLA_NOPE], q[..., MLA_NOPE:]
    ckv = rms_norm(f_kv, kv_norm)
    k_nope = (ckv @ w_uk).reshape(b, n, MLA_HEADS, MLA_NOPE)
    v = (ckv @ w_uv).reshape(b, n, MLA_HEADS, MLA_V)
    k_rope = f_kr
    if rope is not None:
        cos, sin = rope
        q_rope = apply_rope(q_rope, cos[:, None, :], sin[:, None, :])
        k_rope = apply_rope(k_rope, cos, sin)
    q = jnp.concatenate([q_nope, q_rope], axis=-1)
    k = jnp.concatenate([k_nope, jnp.broadcast_to(k_rope[:, :, None, :], (b, n, MLA_HEADS, MLA_ROPE))], axis=-1)
    return q, k, v


def centred_shift(f):
    prev = jnp.pad(f[:, :-1], ((0, 0), (1, 0), (0, 0)))
    nxt = jnp.pad(f[:, 1:], ((0, 0), (0, 1), (0, 0)))
    return 0.5 * (prev + nxt)


def rwkv_features(f, mu, w0, w2, a0, a2, g2, k_k, k_a):
    f = f.astype(jnp.float32)
    f = f + mu * (centred_shift(f) - f)
    r, k, v, wl, al, gl = jnp.split(f, RWKV_SPLIT, axis=-1)
    b, n = f.shape[:2]
    heads = lambda t: t.reshape(t.shape[:-1] + (RWKV_HEADS, RWKV_HEAD))
    w_raw = w0 + jnp.einsum('bndl,dlc->bndc', jnp.tanh(wl.reshape(b, n, 2, DECAY_LORA)), w2)
    decay = jnp.exp(-jnp.exp(-jax.nn.softplus(-w_raw) - 0.5))
    a = jax.nn.sigmoid(a0 + jnp.einsum('bndl,dlc->bndc', al.reshape(b, n, 2, AAA_LORA), a2))
    g = jax.nn.sigmoid(gl) @ g2
    kk = heads(k * k_k)
    kk = kk * lax.rsqrt(jnp.sum(kk * kk, axis=-1, keepdims=True) + 1e-12)
    k_dir = heads(k[:, :, None, :] * (1.0 + (a - 1.0) * k_a))
    return heads(r), k_dir, heads(v), kk, heads(a), heads(decay), g


def rwkv7_scan(r, decay, k, v, kk, a, s0, reverse):
    def step(S, inp):
        r_t, w_t, k_t, v_t, kk_t, a_t = inp
        sa = jnp.einsum('bhij,bhj->bhi', S, kk_t)
        S = S * w_t[:, :, None, :] - sa[..., :, None] * (kk_t * a_t)[..., None, :] + v_t[..., :, None] * k_t[..., None, :]
        return S, jnp.einsum('bhij,bhj->bhi', S, r_t)
    xs = tuple(jnp.moveaxis(t.astype(jnp.float32), 1, 0) for t in (r, decay, k, v, kk, a))
    s_final, ys = lax.scan(step, s0, xs, reverse=reverse)
    return s_final, jnp.moveaxis(ys, 0, 1)


def rwkv_output(y, r, k_dir, v, r_k, gn_w, gn_b, g):
    b, n = y.shape[:2]
    mu = jnp.mean(y, axis=-1, keepdims=True)
    var = jnp.mean(jnp.square(y - mu), axis=-1, keepdims=True)
    yn = ((y - mu) * lax.rsqrt(var + GN_EPS)).reshape(b, n, RWKV_W) * gn_w + gn_b
    bonus = (jnp.sum(r[:, :, None] * k_dir * r_k, axis=(2, -1))[..., None] * v).reshape(b, n, RWKV_W)
    return (yn + bonus) * g


def depthwise_conv(x, w, b):
    y = lax.conv_general_dilated(x, w[:, None, :].astype(x.dtype), window_strides=(1,),
                                 padding=[(CONV_W // 2, CONV_W - 1 - CONV_W // 2)],
                                 dimension_numbers=('NWC', 'WIO', 'NWC'), feature_group_count=x.shape[-1])
    return y + b


def block_diag_linear(x, w, b):
    xb = x.reshape(x.shape[:-1] + (LRU_BLOCKS, LRU_BLOCK))
    return jnp.einsum('bnkc,kcd->bnkd', xb, w).reshape(x.shape) + b


def rglru_inputs(x, wa, ba, wx, bx, lam):
    xf = x.astype(jnp.float32)
    r = jax.nn.sigmoid(block_diag_linear(xf, wa, ba))
    i = jax.nn.sigmoid(block_diag_linear(xf, wx, bx))
    log_a = -LRU_C * r * jax.nn.softplus(-lam)
    return jnp.exp(log_a), jnp.sqrt(-jnp.expm1(2.0 * log_a)) * (i * xf)


def _lin_combine(e1, e2):
    a1, b1 = e1
    a2, b2 = e2
    return a1 * a2, a2 * b1 + b2


def rglru_scan(a, u, h0, reverse):
    a_cum, h = lax.associative_scan(_lin_combine, (a, u), axis=1, reverse=reverse)
    return h + a_cum * h0[:, None, :]


def even_mixer(hl, hc, w_in, q_norm, w_uq, kv_norm, w_uk, w_uv, mu, w0, w2, a0, a2, g2, k_k, k_a, r_k,
               gn_w, gn_b, w_out, ctx_out):
    b, n = hl.shape[:2]
    fq_l, fkv_l, fkr_l, frw_l = jnp.split(hl @ w_in, EVEN_SPLIT, axis=-1)
    fq_c, fkv_c, fkr_c, frw_c = jnp.split(hc @ w_in, EVEN_SPLIT, axis=-1)
    q_l, k_l, v_l = mla_features(fq_l, fkv_l, fkr_l, q_norm, w_uq, kv_norm, w_uk, w_uv, axial_rope_tables(n))
    q_c, k_c, v_c = mla_features(fq_c, fkv_c, fkr_c, q_norm, w_uq, kv_norm, w_uk, w_uv, None)
    att_l = blocked_attend(q_l, jnp.concatenate([k_l, k_c], axis=1),
                           jnp.concatenate([v_l, v_c], axis=1)).reshape(b, n, MLA_HEADS * MLA_V)
    r_l, kd_l, vr_l, kk_l, a_l, w_l, g_l = rwkv_features(frw_l, mu, w0, w2, a0, a2, g2, k_k, k_a)
    r_c, kd_c, vr_c, kk_c, a_c, w_c, g_c = rwkv_features(frw_c, mu, w0, w2, a0, a2, g2, k_k, k_a)
    s0 = jnp.zeros((b, RWKV_HEADS, RWKV_HEAD, RWKV_HEAD), jnp.float32)
    ys_l, ys_c = [], []
    for d, reverse in ((0, False), (1, True)):
        s_ctx, y_c = rwkv7_scan(r_c, w_c[:, :, d], kd_c[:, :, d], vr_c, kk_c, a_c[:, :, d], s0, reverse)
        _, y_l = rwkv7_scan(r_l, w_l[:, :, d], kd_l[:, :, d], vr_l, kk_l, a_l[:, :, d], s_ctx, reverse)
        ys_l.append(y_l)
        ys_c.append(y_c)
    rw_l = rwkv_output(ys_l[0] + ys_l[1], r_l, kd_l, vr_l, r_k, gn_w, gn_b, g_l).astype(hl.dtype)
    y_lat = jnp.concatenate([att_l, rw_l], axis=-1) @ w_out
    if ctx_out:
        att_c = softmax_attend(q_c, k_c, v_c).reshape(b, hc.shape[1], MLA_HEADS * MLA_V)
        rw_c = rwkv_output(ys_c[0] + ys_c[1], r_c, kd_c, vr_c, r_k, gn_w, gn_b, g_c).astype(hc.dtype)
        y_ctx = jnp.concatenate([att_c, rw_c], axis=-1) @ w_out
    else:
        y_ctx = None
    return y_lat, y_ctx


def odd_mixer(hl, hc, w_in, conv_w, conv_b, ga_w, ga_b, gx_w, gx_b, lam, w_out, ctx_out):
    b = hl.shape[0]
    gate_l, xr_l = jnp.split(hl @ w_in, 2, axis=-1)
    gate_c, xr_c = jnp.split(hc @ w_in, 2, axis=-1)
    xr_l = depthwise_conv(xr_l, conv_w, conv_b)
    xr_c = depthwise_conv(xr_c, conv_w, conv_b)
    hs_l, hs_c = [], []
    for d, reverse in ((0, False), (1, True)):
        a_c, u_c = rglru_inputs(xr_c, ga_w[d], ga_b[d], gx_w[d], gx_b[d], lam[d])
        h_c = rglru_scan(a_c, u_c, jnp.zeros((b, LRU_WIDTH), jnp.float32), reverse)
        h0 = h_c[:, 0] if reverse else h_c[:, -1]
        a_l, u_l = rglru_inputs(xr_l, ga_w[d], ga_b[d], gx_w[d], gx_b[d], lam[d])
        hs_l.append(rglru_scan(a_l, u_l, h0, reverse))
        hs_c.append(h_c)
    y_lat = (jax.nn.gelu(gate_l) * (hs_l[0] + hs_l[1]).astype(hl.dtype)) @ w_out
    if ctx_out:
        y_ctx = (jax.nn.gelu(gate_c) * (hs_c[0] + hs_c[1]).astype(hc.dtype)) @ w_out
    else:
        y_ctx = None
    return y_lat, y_ctx


def trunk_layer(x, xc, c, c_ctx, mixer_fn, mixer_params, mod_w, mod_b, ln1_g, ln1_b, mlp_w1, mlp_w2,
                ln2_g, ln2_b, ctx_out):
    mod_l = jnp.split(jax.nn.silu(c) @ mod_w + mod_b, N_MOD, axis=-1)
    mod_c = jnp.split(jax.nn.silu(c_ctx) @ mod_w + mod_b, N_MOD, axis=-1)
    sh1, sc1, g1, sh2, sc2, g2 = [m[:, None, :] for m in mod_l]
    csh1, csc1, cg1, csh2, csc2, cg2 = mod_c
    y_l, y_c = mixer_fn(modulate(x, sh1, sc1), modulate(xc, csh1, csc1), *mixer_params, ctx_out=ctx_out)
    x = layer_norm(DEEPNORM_ALPHA * x + g1 * y_l, ln1_g, ln1_b)
    x = layer_norm(DEEPNORM_ALPHA * x + g2 * squared_relu_mlp(modulate(x, sh2, sc2), mlp_w1, mlp_w2), ln2_g, ln2_b)
    if ctx_out:
        xc = layer_norm(DEEPNORM_ALPHA * xc + cg1 * y_c, ln1_g, ln1_b)
        xc = layer_norm(DEEPNORM_ALPHA * xc + cg2 * squared_relu_mlp(modulate(xc, csh2, csc2), mlp_w1, mlp_w2),
                        ln2_g, ln2_b)
    return x, xc


def setup_inputs(seed: int = 0) -> dict:
    key = jax.random.key(seed)
    ks = jax.random.split(key, 64)
    keys = iter([ks[i] for i in range(64)])

    def nrm(shape, scale):
        return scale * jax.random.normal(next(keys), shape, jnp.float32)

    def uni(shape, lo, hi):
        return jax.random.uniform(next(keys), shape, jnp.float32, lo, hi)

    def lru_lambda(shape):
        p = uni(shape, 0.9, 0.999) ** (1.0 / LRU_C)
        return jnp.log(p) - jnp.log1p(-p)

    D = D_MODEL
    return {
        'x': nrm((BATCH, SEQ, D), 1.0),
        'c': nrm((BATCH, D), 1.0),
        'ctx': nrm((BATCH, CTX_LEN, D), 1.0),
        'c_ctx': nrm((D,), 1.0),
        'l0_mod_w': nrm((D, N_MOD * D), 0.5 * D ** -0.5),
        'l0_mod_b': nrm((N_MOD * D,), 0.02),
        'l0_w_in': nrm((D, EVEN_IN), D ** -0.5),
        'l0_mla_q_norm': 1.0 + nrm((MLA_Q_RANK,), 0.05),
        'l0_mla_w_uq': nrm((MLA_Q_RANK, MLA_HEADS * (MLA_NOPE + MLA_ROPE)), MLA_Q_RANK ** -0.5),
        'l0_mla_kv_norm': 1.0 + nrm((MLA_KV_RANK,), 0.05),
        'l0_mla_w_uk': nrm((MLA_KV_RANK, MLA_HEADS * MLA_NOPE), MLA_KV_RANK ** -0.5),
        'l0_mla_w_uv': nrm((MLA_KV_RANK, MLA_HEADS * MLA_V), MLA_KV_RANK ** -0.5),
        'l0_rwkv_mu': uni((RWKV_IN,), 0.0, 1.0),
        'l0_rwkv_w0': uni((2, RWKV_W), -6.0, 1.0),
        'l0_rwkv_w2': nrm((2, DECAY_LORA, RWKV_W), 0.5 * DECAY_LORA ** -0.5),
        'l0_rwkv_a0': nrm((2, RWKV_W), 0.5),
        'l0_rwkv_a2': nrm((2, AAA_LORA, RWKV_W), 0.5 * AAA_LORA ** -0.5),
        'l0_rwkv_g2': nrm((GATE_LORA, RWKV_W), GATE_LORA ** -0.5),
        'l0_rwkv_k_k': 0.85 + nrm((RWKV_W,), 0.05),
        'l0_rwkv_k_a': 1.0 + nrm((RWKV_W,), 0.05),
        'l0_rwkv_r_k': nrm((RWKV_HEADS, RWKV_HEAD), 0.1),
        'l0_rwkv_gn_w': 1.0 + nrm((RWKV_W,), 0.05),
        'l0_rwkv_gn_b': nrm((RWKV_W,), 0.02),
        'l0_w_out': nrm((MLA_HEADS * MLA_V + RWKV_W, D), DEEPNORM_BETA * D ** -0.5),
        'l0_ln1_g': 1.0 + nrm((D,), 0.05),
        'l0_ln1_b': nrm((D,), 0.02),
        'l0_mlp_w1': nrm((D, D_FF), D ** -0.5),
        'l0_mlp_w2': nrm((D_FF, D), DEEPNORM_BETA * D_FF ** -0.5),
        'l0_ln2_g': 1.0 + nrm((D,), 0.05),
        'l0_ln2_b': nrm((D,), 0.02),
        'l1_mod_w': nrm((D, N_MOD * D), 0.5 * D ** -0.5),
        'l1_mod_b': nrm((N_MOD * D,), 0.02),
        'l1_w_in': nrm((D, 2 * LRU_WIDTH), D ** -0.5),
        'l1_conv_w': nrm((CONV_W, LRU_WIDTH), CONV_W ** -0.5),
        'l1_conv_b': nrm((LRU_WIDTH,), 0.02),
        'l1_lru_ga_w': nrm((2, LRU_BLOCKS, LRU_BLOCK, LRU_BLOCK), LRU_BLOCK ** -0.5),
        'l1_lru_ga_b': nrm((2, LRU_WIDTH), 0.02),
        'l1_lru_gx_w': nrm((2, LRU_BLOCKS, LRU_BLOCK, LRU_BLOCK), LRU_BLOCK ** -0.5),
        'l1_lru_gx_b': nrm((2, LRU_WIDTH), 0.02),
        'l1_lru_lambda': lru_lambda((2, LRU_WIDTH)),
        'l1_w_out': nrm((LRU_WIDTH, D), DEEPNORM_BETA * LRU_WIDTH ** -0.5),
        'l1_ln1_g': 1.0 + nrm((D,), 0.05),
        'l1_ln1_b': nrm((D,), 0.02),
        'l1_mlp_w1': nrm((D, D_FF), D ** -0.5),
        'l1_mlp_w2': nrm((D_FF, D), DEEPNORM_BETA * D_FF ** -0.5),
        'l1_ln2_g': 1.0 + nrm((D,), 0.05),
        'l1_ln2_b': nrm((D,), 0.02),
    }


def reference(x, c, ctx, c_ctx,
              l0_mod_w, l0_mod_b, l0_w_in, l0_mla_q_norm, l0_mla_w_uq, l0_mla_kv_norm, l0_mla_w_uk, l0_mla_w_uv,
              l0_rwkv_mu, l0_rwkv_w0, l0_rwkv_w2, l0_rwkv_a0, l0_rwkv_a2, l0_rwkv_g2, l0_rwkv_k_k, l0_rwkv_k_a,
              l0_rwkv_r_k, l0_rwkv_gn_w, l0_rwkv_gn_b, l0_w_out,
              l0_ln1_g, l0_ln1_b, l0_mlp_w1, l0_mlp_w2, l0_ln2_g, l0_ln2_b,
              l1_mod_w, l1_mod_b, l1_w_in, l1_conv_w, l1_conv_b, l1_lru_ga_w, l1_lru_ga_b, l1_lru_gx_w, l1_lru_gx_b,
              l1_lru_lambda, l1_w_out,
              l1_ln1_g, l1_ln1_b, l1_mlp_w1, l1_mlp_w2, l1_ln2_g, l1_ln2_b):
    mixers = (
        (even_mixer, (l0_w_in, l0_mla_q_norm, l0_mla_w_uq, l0_mla_kv_norm, l0_mla_w_uk, l0_mla_w_uv,
                      l0_rwkv_mu, l0_rwkv_w0, l0_rwkv_w2, l0_rwkv_a0, l0_rwkv_a2, l0_rwkv_g2, l0_rwkv_k_k,
                      l0_rwkv_k_a, l0_rwkv_r_k, l0_rwkv_gn_w, l0_rwkv_gn_b, l0_w_out)),
        (odd_mixer, (l1_w_in, l1_conv_w, l1_conv_b, l1_lru_ga_w, l1_lru_ga_b, l1_lru_gx_w, l1_lru_gx_b,
                     l1_lru_lambda, l1_w_out)),
    )
    commons = (
        (l0_mod_w, l0_mod_b, l0_ln1_g, l0_ln1_b, l0_mlp_w1, l0_mlp_w2, l0_ln2_g, l0_ln2_b),
        (l1_mod_w, l1_mod_b, l1_ln1_g, l1_ln1_b, l1_mlp_w1, l1_mlp_w2, l1_ln2_g, l1_ln2_b),
    )
    xc = ctx
    for layer in range(DEPTH):
        mixer_fn, mixer_params = mixers[layer % 2]
        x, xc = trunk_layer(x, xc, c, c_ctx, mixer_fn, mixer_params, *commons[layer],
                            ctx_out=(layer < DEPTH - 1))
    return x
```

```python
import functools
import math

import jax
import jax.numpy as jnp
from jax import lax
from jax.experimental import pallas as pl
from jax.experimental.pallas import tpu as pltpu

F32 = jnp.float32
BF16 = jnp.bfloat16

D = 1024
DEPTH = 2
N_MOD = 6
ALPHA = (2.0 * DEPTH) ** 0.25
LN_EPS = 1e-5
RMS_EPS = 1e-6

HEADS = 8
MLA_NOPE = 64
MLA_ROPE = 32
MLA_V = 64
MLA_Q_RANK = 384
MLA_KV_RANK = 256
ROPE_AXIS = MLA_ROPE // 2
ROPE_THETA = 10000.0
GRID_W = 64
ATT_SCALE = (MLA_NOPE + MLA_ROPE) ** -0.5
HEAD_PAD = 128

RW_HEAD = 64
RW_W = HEADS * RW_HEAD
LORA_W = 64
LORA_A = 64
LORA_G = 128
RW_IN = 3 * RW_W + 2 * LORA_W + 2 * LORA_A + LORA_G
GN_EPS = 64e-5
EXP_NEG_HALF = math.exp(-0.5)

LRU_W = D
LRU_BLOCKS = 8
LRU_BLOCK = LRU_W // LRU_BLOCKS
LRU_C = 8.0
D_FF = 4 * D

TM = 256
SUB = 8
SCAN_TC = 32
VMEM_LIMIT = 56 * 1024 * 1024

IN0_COLS = MLA_Q_RANK + MLA_KV_RANK + 2 * HEAD_PAD + RW_IN


def _params(sem):
    return pltpu.CompilerParams(dimension_semantics=sem, vmem_limit_bytes=VMEM_LIMIT)


def _bdot(a, w):
    return jnp.dot(a.astype(BF16), w, preferred_element_type=F32)


def _sigmoid(x):
    return 1.0 / (1.0 + jnp.exp(-x))


def _layer_norm(z, g, b):
    mu = jnp.mean(z, axis=-1, keepdims=True)
    zc = z - mu
    var = jnp.mean(zc * zc, axis=-1, keepdims=True)
    return zc * lax.rsqrt(var + LN_EPS) * g + b


def _head_sum(x, ones_bd):
    hi = x.astype(BF16)
    lo = (x - hi.astype(F32)).astype(BF16)
    return (jnp.dot(hi, ones_bd, preferred_element_type=F32)
            + jnp.dot(lo, ones_bd, preferred_element_type=F32))


def _const_spec(shape):
    nd = len(shape)
    return pl.BlockSpec(shape, lambda *_: (0,) * nd)


def _mod_kernel(c_ref, w_ref, b_ref, o_ref):
    c = c_ref[...]
    s = c * _sigmoid(c)
    o_ref[...] = jnp.dot(s, w_ref[...], precision=lax.Precision.HIGHEST,
                         preferred_element_type=F32) + b_ref[...]


def _mod_table(c, c_ctx, mod_w, mod_b):
    b = c.shape[0]
    rows = 16
    cc = jnp.zeros((rows, D), F32).at[:b].set(c).at[b].set(c_ctx)
    tn = 1024
    out = pl.pallas_call(
        _mod_kernel,
        grid=(N_MOD * D // tn,),
        in_specs=[pl.BlockSpec((rows, D), lambda j: (0, 0)),
                  pl.BlockSpec((D, tn), lambda j: (0, j)),
                  pl.BlockSpec((1, tn), lambda j: (0, j))],
        out_specs=pl.BlockSpec((rows, tn), lambda j: (0, j)),
        out_shape=jax.ShapeDtypeStruct((rows, N_MOD * D), F32),
        compiler_params=_params(("arbitrary",)),
        name="mod_proj",
    )(cc, mod_w, mod_b.reshape(1, -1))
    lat = out[:b].reshape(b, N_MOD, D)
    ctx = jnp.broadcast_to(out[b].reshape(1, N_MOD, D), (b, N_MOD, D))
    tab = jnp.stack([ctx, lat], axis=1)
    return jnp.pad(tab, ((0, 0), (0, 0), (0, SUB - N_MOD), (0, 0)))


def _mod_spec(combined):
    if combined:
        return pl.BlockSpec((1, 1, SUB, D), lambda b, i: (b, jnp.minimum(i, 1), 0, 0))
    return pl.BlockSpec((1, 1, SUB, D), lambda b, i: (b, 1, 0, 0))


def _in0_kernel(x_ref, mod_ref, win_ref, qn_ref, kvn_ref, wqa_ref, wqb_ref, wuk_ref, wuv_ref,
                cs_ref, sn_ref, q_ref, k_ref, v_ref, frw_ref):
    x = x_ref[0]
    m = mod_ref[0, 0]
    h = x * (1.0 + m[1:2]) + m[0:1]
    f = _bdot(h, win_ref[...])
    o_kv = MLA_Q_RANK
    o_kr = o_kv + MLA_KV_RANK
    o_rw = o_kr + 2 * HEAD_PAD
    fq = f[:, :o_kv]
    fkv = f[:, o_kv:o_kr]
    fkr = f[:, o_kr:o_kr + HEAD_PAD]
    fkr_rot = f[:, o_kr + HEAD_PAD:o_rw]
    frw_ref[0] = f[:, o_rw:]
    qn = fq * lax.rsqrt(jnp.mean(fq * fq, axis=-1, keepdims=True) + RMS_EPS) * qn_ref[...]
    ckv = fkv * lax.rsqrt(jnp.mean(fkv * fkv, axis=-1, keepdims=True) + RMS_EPS) * kvn_ref[...]
    qn = qn.astype(BF16)
    ckv = ckv.astype(BF16)
    qa = jnp.dot(qn, wqa_ref[...], preferred_element_type=F32)
    qb = jnp.dot(qn, wqb_ref[...], preferred_element_type=F32)
    kn = jnp.dot(ckv, wuk_ref[...], preferred_element_type=F32)
    cs = cs_ref[...]
    sn = sn_ref[...]
    kr = fkr * cs + fkr_rot * sn
    for hd in range(HEADS):
        sl = slice(hd * HEAD_PAD, (hd + 1) * HEAD_PAD)
        q_ref[0, :, sl] = ((qa[:, sl] * cs + qb[:, sl] * sn) * ATT_SCALE).astype(BF16)
        k_ref[0, :, sl] = (kn[:, sl] + kr).astype(BF16)
    v_ref[0] = jnp.dot(ckv, wuv_ref[...], preferred_element_type=F32).astype(BF16)


def _in0(xc, modt, win, qn, kvn, wqa, wqb, wuk, wuv, cs, sn):
    b, t, _ = xc.shape
    row = lambda w: pl.BlockSpec((1, TM, w), lambda bb, i: (bb, i, 0))
    return pl.pallas_call(
        _in0_kernel,
        grid=(b, t // TM),
        in_specs=[row(D), _mod_spec(True), _const_spec(win.shape), _const_spec(qn.shape),
                  _const_spec(kvn.shape), _const_spec(wqa.shape), _const_spec(wqb.shape),
                  _const_spec(wuk.shape), _const_spec(wuv.shape),
                  pl.BlockSpec((TM, HEAD_PAD), lambda bb, i: (i, 0)),
                  pl.BlockSpec((TM, HEAD_PAD), lambda bb, i: (i, 0))],
        out_specs=[row(HEADS * HEAD_PAD), row(HEADS * HEAD_PAD), row(HEADS * MLA_V), row(RW_IN)],
        out_shape=[jax.ShapeDtypeStruct((b, t, HEADS * HEAD_PAD), BF16),
                   jax.ShapeDtypeStruct((b, t, HEADS * HEAD_PAD), BF16),
                   jax.ShapeDtypeStruct((b, t, HEADS * MLA_V), BF16),
                   jax.ShapeDtypeStruct((b, t, RW_IN), F32)],
        compiler_params=_params(("parallel", "parallel")),
        name="l0_in_proj",
    )(xc, modt, win, qn, kvn, wqa, wqb, wuk, wuv, cs, sn)


def _att_kernel(q_ref, k_ref, v_ref, o_ref, *, n_ctx, n_all):
    i = pl.program_id(2)

    def attend(nk):
        v = v_ref[0, :nk, :]
        lane = lax.broadcasted_iota(jnp.int32, v.shape, 1)
        acc = None
        for hh in range(2):
            sl = slice(hh * HEAD_PAD, (hh + 1) * HEAD_PAD)
            q = q_ref[0, :, sl]
            k = k_ref[0, :nk, sl]
            s = lax.dot_general(q, k, (((1,), (1,)), ((), ())), preferred_element_type=F32)
            p = jnp.exp(s - jnp.max(s, axis=-1, keepdims=True))
            l = jnp.sum(p, axis=-1, keepdims=True)
            keep = (lane < MLA_V) if hh == 0 else (lane >= MLA_V)
            vh = jnp.where(keep, v, jnp.zeros_like(v))
            o = jnp.dot(p.astype(BF16), vh, preferred_element_type=F32) / l
            acc = o if acc is None else acc + o
        o_ref[0] = acc.astype(o_ref.dtype)

    @pl.when(i == 0)
    def _():
        attend(n_ctx)

    @pl.when(i > 0)
    def _():
        attend(n_all)


def _attention(q, k, v):
    b, t, _ = q.shape
    return pl.pallas_call(
        functools.partial(_att_kernel, n_ctx=TM, n_all=t),
        grid=(b, HEADS // 2, t // TM),
        in_specs=[pl.BlockSpec((1, TM, 2 * HEAD_PAD), lambda bb, hp, i: (bb, i, hp)),
                  pl.BlockSpec((1, t, 2 * HEAD_PAD), lambda bb, hp, i: (bb, 0, hp)),
                  pl.BlockSpec((1, t, 2 * MLA_V), lambda bb, hp, i: (bb, 0, hp))],
        out_specs=pl.BlockSpec((1, TM, 2 * MLA_V), lambda bb, hp, i: (bb, i, hp)),
        out_shape=jax.ShapeDtypeStruct((b, t, HEADS * MLA_V), BF16),
        compiler_params=_params(("parallel", "parallel", "parallel")),
        name="l0_attention",
    )(q, k, v)


def _halo_flags(i, n_tiles):
    return i >= 2, jnp.logical_and(i >= 1, i < n_tiles - 1)


def _rwfeat_kernel(f_ref, fp_ref, fn_ref, mu_ref, w0_ref, w2_ref, a0_ref, a2_ref, g2_ref,
                   kk_ref, ka_ref, rk_ref, ones_ref, sf_ref, g_ref, bonus_ref):
    i = pl.program_id(1)
    has_prev, has_next = _halo_flags(i, pl.num_programs(1))
    f = f_ref[0]
    prow = jnp.where(has_prev, fp_ref[0, SUB - 1:SUB, :], 0.0)
    nrow = jnp.where(has_next, fn_ref[0, 0:1, :], 0.0)
    rid = lax.broadcasted_iota(jnp.int32, (TM, 1), 0)
    prev = jnp.where(rid == 0, prow, pltpu.roll(f, 1, axis=0))
    nxt = jnp.where(rid == TM - 1, nrow, pltpu.roll(f, TM - 1, axis=0))
    f = f + mu_ref[...] * (0.5 * (prev + nxt) - f)
    r = f[:, 0:RW_W]
    k = f[:, RW_W:2 * RW_W]
    v = f[:, 2 * RW_W:3 * RW_W]
    o = 3 * RW_W
    wl = f[:, o:o + 2 * LORA_W]
    al = f[:, o + 2 * LORA_W:o + 2 * LORA_W + 2 * LORA_A]
    gl = f[:, o + 2 * LORA_W + 2 * LORA_A:]
    w_raw = w0_ref[...] + _bdot(jnp.tanh(wl), w2_ref[...])
    decay = jnp.exp(-EXP_NEG_HALF * _sigmoid(w_raw))
    a = _sigmoid(a0_ref[...] + _bdot(al, a2_ref[...]))
    g_ref[0] = _bdot(_sigmoid(gl), g2_ref[...])
    ones_bd = ones_ref[...]
    kk = k * kk_ref[...]
    kk = kk * lax.rsqrt(_head_sum(kk * kk, ones_bd) + 1e-12)
    ka = ka_ref[...]
    kd = [k * (1.0 + (a[:, d * RW_W:(d + 1) * RW_W] - 1.0) * ka) for d in range(2)]
    bonus_ref[0] = _head_sum(r * (kd[0] + kd[1]) * rk_ref[...], ones_bd) * v
    parts = [r, kk, v, decay[:, :RW_W], decay[:, RW_W:], kd[0], kd[1],
             kk * a[:, :RW_W], kk * a[:, RW_W:]]
    for n, p in enumerate(parts):
        sf_ref[0, :, n * RW_W:(n + 1) * RW_W] = p


def _halo_specs(width, t):
    nb = TM // SUB
    last = t // SUB - 1
    prev = pl.BlockSpec((1, SUB, width), lambda bb, i: (bb, jnp.maximum(i * nb - 1, 0), 0))
    nxt = pl.BlockSpec((1, SUB, width), lambda bb, i: (bb, jnp.minimum((i + 1) * nb, last), 0))
    return prev, nxt


def _rwfeat(frw, mu, w0, w2bd, a0, a2bd, g2, k_k, k_a, r_k, ones_bd):
    b, t, _ = frw.shape
    row = lambda w: pl.BlockSpec((1, TM, w), lambda bb, i: (bb, i, 0))
    prev, nxt = _halo_specs(RW_IN, t)
    consts = (mu, w0, w2bd, a0, a2bd, g2, k_k, k_a, r_k, ones_bd)
    return pl.pallas_call(
        _rwfeat_kernel,
        grid=(b, t // TM),
        in_specs=[row(RW_IN), prev, nxt] + [_const_spec(c.shape) for c in consts],
        out_specs=[row(9 * RW_W), row(RW_W), row(RW_W)],
        out_shape=[jax.ShapeDtypeStruct((b, t, 9 * RW_W), F32),
                   jax.ShapeDtypeStruct((b, t, RW_W), F32),
                   jax.ShapeDtypeStruct((b, t, RW_W), F32)],
        compiler_params=_params(("parallel", "parallel")),
        name="l0_rwkv_feat",
    )(frw, frw, frw, *consts)


def _rwscan_kernel(x_ref, y_ref, s_ref):
    @pl.when(pl.program_id(0) == 0)
    def _():
        s_ref[...] = jnp.zeros_like(s_ref)

    def step(s, carry):
        v = x_ref[2, s]
        sa = jnp.zeros_like(v)
        for j in range(RW_HEAD):
            sa = sa + s_ref[j] * x_ref[1, s, pl.ds(j, 1), :]
        y = jnp.zeros_like(v)
        for j in range(RW_HEAD):
            sj = (s_ref[j] * x_ref[3, s, pl.ds(j, 1), :] - sa * x_ref[5, s, pl.ds(j, 1), :]
                  + v * x_ref[4, s, pl.ds(j, 1), :])
            s_ref[j] = sj
            y = y + sj * x_ref[0, s, pl.ds(j, 1), :]
        y_ref[s] = y
        return carry

    lax.fori_loop(0, x_ref.shape[1], step, 0)


def _rwscan(x):
    _, t, _, lanes = x.shape
    return pl.pallas_call(
        _rwscan_kernel,
        grid=(t // SCAN_TC,),
        in_specs=[pl.BlockSpec((6, SCAN_TC, RW_HEAD, lanes), lambda i: (0, i, 0, 0))],
        out_specs=pl.BlockSpec((SCAN_TC, RW_HEAD, lanes), lambda i: (i, 0, 0)),
        out_shape=jax.ShapeDtypeStruct((t, RW_HEAD, lanes), F32),
        scratch_shapes=[pltpu.VMEM((RW_HEAD, RW_HEAD, lanes), F32)],
        compiler_params=_params(("arbitrary",)),
        name="l0_rwkv_scan",
    )(x)


def _seg_flip(x, axis):
    c = lax.slice_in_dim(x, 0, TM, axis=axis)
    l = lax.slice_in_dim(x, TM, x.shape[axis], axis=axis)
    return jnp.concatenate([jnp.flip(c, axis), jnp.flip(l, axis)], axis=axis)


def _rwkv_scan_both(sf):
    b, t, _ = sf.shape
    x = sf.reshape(b, t, 9, HEADS, RW_HEAD).transpose(2, 1, 4, 0, 3).reshape(9, t, RW_HEAD, b * HEADS)
    fwd = jnp.stack([x[0], x[1], x[2], x[3], x[5], x[7]])
    rev = _seg_flip(jnp.stack([x[0], x[1], x[2], x[4], x[6], x[8]]), 1)
    y = _rwscan(jnp.concatenate([fwd, rev], axis=-1))
    yf = y[:, :, :b * HEADS]
    yr = _seg_flip(y[:, :, b * HEADS:], 0)
    ys = (yf + yr).reshape(t, RW_HEAD, b, HEADS)
    return ys.transpose(2, 0, 3, 1).reshape(b, t, RW_W)


def _out0_kernel(att_ref, ys_ref, bonus_ref, g_ref, x_ref, mod_ref, gnw_ref, gnb_ref, ones_ref,
                 woa_ref, wob_ref, lng_ref, lnb_ref, o_ref):
    ones_bd = ones_ref[...]
    y = ys_ref[0]
    mu = _head_sum(y, ones_bd) * (1.0 / RW_HEAD)
    yc = y - mu
    var = _head_sum(yc * yc, ones_bd) * (1.0 / RW_HEAD)
    yn = yc * lax.rsqrt(var + GN_EPS) * gnw_ref[...] + gnb_ref[...]
    rw = (yn + bonus_ref[0]) * g_ref[0]
    o = (jnp.dot(att_ref[0], woa_ref[...], preferred_element_type=F32)
         + _bdot(rw, wob_ref[...]))
    m = mod_ref[0, 0]
    o_ref[0] = _layer_norm(ALPHA * x_ref[0] + m[2:3] * o, lng_ref[...], lnb_ref[...])


def _out0(att, ys, bonus, g, xc, modt, gnw, gnb, ones_bd, woa, wob, lng, lnb):
    b, t, _ = xc.shape
    row = lambda w: pl.BlockSpec((1, TM, w), lambda bb, i: (bb, i, 0))
    consts = (gnw, gnb, ones_bd, woa, wob, lng, lnb)
    return pl.pallas_call(
        _out0_kernel,
        grid=(b, t // TM),
        in_specs=[row(RW_W), row(RW_W), row(RW_W), row(RW_W), row(D), _mod_spec(True)]
                 + [_const_spec(c.shape) for c in consts],
        out_specs=row(D),
        out_shape=jax.ShapeDtypeStruct((b, t, D), F32),
        compiler_params=_params(("parallel", "parallel")),
        name="l0_out_proj",
    )(att, ys, bonus, g, xc, modt, *consts)


def _mlp_kernel(x_ref, mod_ref, w1_ref, w2_ref, lng_ref, lnb_ref, o_ref):
    x = x_ref[0]
    m = mod_ref[0, 0]
    h = (x * (1.0 + m[4:5]) + m[3:4]).astype(BF16)
    acc = None
    fc = 1024
    for c in range(D_FF // fc):
        u = jnp.dot(h, w1_ref[:, c * fc:(c + 1) * fc], preferred_element_type=F32)
        u = jnp.square(jnp.maximum(u, 0.0)).astype(BF16)
        part = jnp.dot(u, w2_ref[c * fc:(c + 1) * fc, :], preferred_element_type=F32)
        acc = part if acc is None else acc + part
    o_ref[0] = _layer_norm(ALPHA * x + m[5:6] * acc, lng_ref[...], lnb_ref[...])


def _mlp(x, modt, w1, w2, lng, lnb, *, combined, row_off):
    b, tx, _ = x.shape
    nt = tx // TM - row_off
    single = pl.Buffered(1)
    wspec = lambda shape: pl.BlockSpec(shape, lambda *_: (0,) * len(shape), pipeline_mode=single)
    return pl.pallas_call(
        _mlp_kernel,
        grid=(b, nt),
        in_specs=[pl.BlockSpec((1, TM, D), lambda bb, i: (bb, i + row_off, 0)), _mod_spec(combined),
                  wspec(w1.shape), wspec(w2.shape), _const_spec(lng.shape), _const_spec(lnb.shape)],
        out_specs=pl.BlockSpec((1, TM, D), lambda bb, i: (bb, i, 0)),
        out_shape=jax.ShapeDtypeStruct((b, nt * TM, D), F32),
        compiler_params=_params(("parallel", "parallel")),
        name="mlp",
    )(x, modt, w1, w2, lng, lnb)


def _in1_kernel(x_ref, mod_ref, w_ref, gate_ref, xr_ref):
    m = mod_ref[0, 0]
    h = x_ref[0] * (1.0 + m[1:2]) + m[0:1]
    f = _bdot(h, w_ref[...])
    gate_ref[0] = f[:, :LRU_W]
    xr_ref[0] = f[:, LRU_W:]


def _in1(xc, modt, w):
    b, t, _ = xc.shape
    row = lambda wd: pl.BlockSpec((1, TM, wd), lambda bb, i: (bb, i, 0))
    return pl.pallas_call(
        _in1_kernel,
        grid=(b, t // TM),
        in_specs=[row(D), _mod_spec(True), _const_spec(w.shape)],
        out_specs=[row(LRU_W), row(LRU_W)],
        out_shape=[jax.ShapeDtypeStruct((b, t, LRU_W), F32)] * 2,
        compiler_params=_params(("parallel", "parallel")),
        name="l1_in_proj",
    )(xc, modt, w)


def _lrufeat_kernel(x_ref, xp_ref, xn_ref, cw_ref, cb_ref, wbd_ref, gb_ref, lam_ref,
                    a0_ref, u0_ref, a1_ref, u1_ref):
    i = pl.program_id(1)
    has_prev, has_next = _halo_flags(i, pl.num_programs(1))
    x = x_ref[0]
    p2 = jnp.where(has_prev, xp_ref[0, SUB - 2:SUB - 1, :], 0.0)
    p1 = jnp.where(has_prev, xp_ref[0, SUB - 1:SUB, :], 0.0)
    n1 = jnp.where(has_next, xn_ref[0, 0:1, :], 0.0)
    rid = lax.broadcasted_iota(jnp.int32, (TM, 1), 0)
    xm1 = jnp.where(rid == 0, p1, pltpu.roll(x, 1, axis=0))
    xm2 = jnp.where(rid == 0, p2, jnp.where(rid == 1, p1, pltpu.roll(x, 2, axis=0)))
    xp1 = jnp.where(rid == TM - 1, n1, pltpu.roll(x, TM - 1, axis=0))
    cw = cw_ref[...]
    xc = cw[0:1] * xm2 + cw[1:2] * xm1 + cw[2:3] * x + cw[3:4] * xp1 + cb_ref[...]
    lam = lam_ref[...]
    nl = -lam
    softplus = jnp.maximum(nl, 0.0) + jnp.log1p(jnp.exp(-jnp.abs(nl)))
    gb = gb_ref[...]
    outs = ((a0_ref, u0_ref), (a1_ref, u1_ref))
    for blk in range(LRU_BLOCKS):
        sl = slice(blk * LRU_BLOCK, (blk + 1) * LRU_BLOCK)
        xb = xc[:, sl]
        z = _bdot(xb, wbd_ref[blk])
        for d in range(2):
            zr = z[:, (2 * d) * LRU_BLOCK:(2 * d + 1) * LRU_BLOCK] + gb[2 * d:2 * d + 1, sl]
            zi = z[:, (2 * d + 1) * LRU_BLOCK:(2 * d + 2) * LRU_BLOCK] + gb[2 * d + 1:2 * d + 2, sl]
            log_a = -LRU_C * _sigmoid(zr) * softplus[d:d + 1, sl]
            a_ref, u_ref = outs[d]
            a = jnp.exp(log_a)
            a_ref[0, :, sl] = a
            u_ref[0, :, sl] = jnp.sqrt(-jnp.tanh(log_a) * (a * a + 1.0)) * (_sigmoid(zi) * xb)


def _lrufeat(xr, cw, cb, wbd, gb, lam):
    b, t, _ = xr.shape
    row = lambda w: pl.BlockSpec((1, TM, w), lambda bb, i: (bb, i, 0))
    prev, nxt = _halo_specs(LRU_W, t)
    consts = (cw, cb, wbd, gb, lam)
    return pl.pallas_call(
        _lrufeat_kernel,
        grid=(b, t // TM),
        in_specs=[row(LRU_W), prev, nxt] + [_const_spec(c.shape) for c in consts],
        out_specs=[row(LRU_W)] * 4,
        out_shape=[jax.ShapeDtypeStruct((b, t, LRU_W), F32)] * 4,
        compiler_params=_params(("parallel", "parallel")),
        name="l1_lru_feat",
    )(xr, xr, xr, *consts)


def _lruscan_kernel(af_ref, uf_ref, ar_ref, ur_ref, hf_ref, hr_ref, cf_ref, cr_ref):
    @pl.when(pl.program_id(1) == 0)
    def _():
        cf_ref[...] = jnp.zeros_like(cf_ref)
        cr_ref[...] = jnp.zeros_like(cr_ref)

    def body(t, carry):
        hf, hr = carry
        hf = af_ref[0, pl.ds(t, 1), :] * hf + uf_ref[0, pl.ds(t, 1), :]
        hf_ref[0, pl.ds(t, 1), :] = hf
        tr = TM - 1 - t
        hr = ar_ref[0, pl.ds(tr, 1), :] * hr + ur_ref[0, pl.ds(tr, 1), :]
        hr_ref[0, pl.ds(tr, 1), :] = hr
        return hf, hr

    hf, hr = lax.fori_loop(0, TM, body, (cf_ref[...], cr_ref[...]))
    cf_ref[...] = hf
    cr_ref[...] = hr


def _lruscan(a0, u0, a1, u1):
    b, t, _ = a0.shape
    nt = t // TM
    fwd = pl.BlockSpec((1, TM, LRU_W), lambda bb, i: (bb, i, 0))
    rev = pl.BlockSpec((1, TM, LRU_W), lambda bb, i: (bb, jnp.where(i == 0, 0, nt - i), 0))
    return pl.pallas_call(
        _lruscan_kernel,
        grid=(b, nt),
        in_specs=[fwd, fwd, rev, rev],
        out_specs=[fwd, rev],
        out_shape=[jax.ShapeDtypeStruct((b, t, LRU_W), F32)] * 2,
        scratch_shapes=[pltpu.VMEM((1, LRU_W), F32), pltpu.VMEM((1, LRU_W), F32)],
        compiler_params=_params(("parallel", "arbitrary")),
        name="l1_lru_scan",
    )(a0, u0, a1, u1)


def _out1_kernel(gate_ref, hf_ref, hr_ref, x_ref, mod_ref, w_ref, lng_ref, lnb_ref, o_ref):
    gate = gate_ref[0]
    gelu = 0.5 * gate * (1.0 + jnp.tanh(math.sqrt(2.0 / math.pi) * (gate + 0.044715 * gate * gate * gate)))
    o = _bdot(gelu * (hf_ref[0] + hr_ref[0]), w_ref[...])
    m = mod_ref[0, 0]
    o_ref[0] = _layer_norm(ALPHA * x_ref[0] + m[2:3] * o, lng_ref[...], lnb_ref[...])


def _out1(gate, hf, hr, xc, modt, w, lng, lnb):
    b, t, _ = xc.shape
    nt = t // TM - 1
    lat = lambda wd: pl.BlockSpec((1, TM, wd), lambda bb, i: (bb, i + 1, 0))
    consts = (w, lng, lnb)
    return pl.pallas_call(
        _out1_kernel,
        grid=(b, nt),
        in_specs=[lat(LRU_W), lat(LRU_W), lat(LRU_W), lat(D), _mod_spec(False)]
                 + [_const_spec(c.shape) for c in consts],
        out_specs=pl.BlockSpec((1, TM, D), lambda bb, i: (bb, i, 0)),
        out_shape=jax.ShapeDtypeStruct((b, nt * TM, D), F32),
        compiler_params=_params(("parallel", "parallel")),
        name="l1_out_proj",
    )(gate, hf, hr, xc, modt, *consts)


def _rot_cols(w):
    ws = w.reshape(w.shape[:-1] + (2, 2, ROPE_AXIS // 2))
    return jnp.stack([-ws[..., 1, :], ws[..., 0, :]], axis=-2).reshape(w.shape)


def _rope_tables(n, n_ctx):
    rows_n = n // GRID_W
    rows = jnp.repeat(jnp.arange(rows_n, dtype=F32), GRID_W)
    cols = jnp.tile(jnp.arange(GRID_W, dtype=F32), rows_n)
    inv_freq = ROPE_THETA ** (-jnp.arange(0, ROPE_AXIS, 2, dtype=F32) / ROPE_AXIS)
    ang_r = rows[:, None] * inv_freq
    ang_c = cols[:, None] * inv_freq
    ang = jnp.concatenate([ang_r, ang_r, ang_c, ang_c], axis=-1)
    cos = jnp.concatenate([jnp.ones((n_ctx, MLA_ROPE), F32), jnp.cos(ang)], axis=0)
    sin = jnp.concatenate([jnp.zeros((n_ctx, MLA_ROPE), F32), jnp.sin(ang)], axis=0)
    t = n + n_ctx
    cs = jnp.concatenate([jnp.ones((t, MLA_NOPE), F32), cos, jnp.zeros((t, 32), F32)], axis=-1)
    sn = jnp.concatenate([jnp.zeros((t, MLA_NOPE), F32), sin, jnp.zeros((t, 32), F32)], axis=-1)
    return cs, sn


def _block_diag2(w):
    z = jnp.zeros_like(w[0])
    return jnp.concatenate([jnp.concatenate([w[0], z], axis=1), jnp.concatenate([z, w[1]], axis=1)], axis=0)


def kernel(x, c, ctx, c_ctx, l0_mod_w, l0_mod_b, l0_w_in, l0_mla_q_norm, l0_mla_w_uq, l0_mla_kv_norm, l0_mla_w_uk, l0_mla_w_uv, l0_rwkv_mu, l0_rwkv_w0, l0_rwkv_w2, l0_rwkv_a0, l0_rwkv_a2, l0_rwkv_g2, l0_rwkv_k_k, l0_rwkv_k_a, l0_rwkv_r_k, l0_rwkv_gn_w, l0_rwkv_gn_b, l0_w_out, l0_ln1_g, l0_ln1_b, l0_mlp_w1, l0_mlp_w2, l0_ln2_g, l0_ln2_b, l1_mod_w, l1_mod_b, l1_w_in, l1_conv_w, l1_conv_b, l1_lru_ga_w, l1_lru_ga_b, l1_lru_gx_w, l1_lru_gx_b, l1_lru_lambda, l1_w_out, l1_ln1_g, l1_ln1_b, l1_mlp_w1, l1_mlp_w2, l1_ln2_g, l1_ln2_b):
    b, n, _ = x.shape
    n_ctx = ctx.shape[1]
    assert n_ctx == TM and n % TM == 0 and x.shape[2] == D
    row = lambda v: v.reshape(1, -1)

    xc = jnp.concatenate([ctx, x], axis=1)
    mod0 = _mod_table(c, c_ctx, l0_mod_w, l0_mod_b)
    mod1 = _mod_table(c, c_ctx, l1_mod_w, l1_mod_b)

    o_kv = MLA_Q_RANK
    o_kr = o_kv + MLA_KV_RANK
    o_rw = o_kr + MLA_ROPE
    w_kr = l0_w_in[:, o_kr:o_rw]
    zl = jnp.zeros((D, MLA_NOPE), F32)
    zr = jnp.zeros((D, HEAD_PAD - MLA_NOPE - MLA_ROPE), F32)
    win0 = jnp.concatenate([l0_w_in[:, :o_kr], zl, w_kr, zr, zl, _rot_cols(w_kr), zr,
                            l0_w_in[:, o_rw:]], axis=1).astype(BF16)
    wq = l0_mla_w_uq.reshape(MLA_Q_RANK, HEADS, MLA_NOPE + MLA_ROPE)
    q_nope, q_rope = wq[..., :MLA_NOPE], wq[..., MLA_NOPE:]
    zq = jnp.zeros((MLA_Q_RANK, HEADS, 32), F32)
    wqa = jnp.concatenate([q_nope, q_rope, zq], axis=-1).reshape(MLA_Q_RANK, HEADS * HEAD_PAD).astype(BF16)
    wqb = jnp.concatenate([jnp.zeros_like(q_nope), _rot_cols(q_rope), zq],
                          axis=-1).reshape(MLA_Q_RANK, HEADS * HEAD_PAD).astype(BF16)
    wk = l0_mla_w_uk.reshape(MLA_KV_RANK, HEADS, MLA_NOPE)
    wuk = jnp.concatenate([wk, jnp.zeros_like(wk)], axis=-1).reshape(MLA_KV_RANK, HEADS * HEAD_PAD).astype(BF16)
    wuv = l0_mla_w_uv.astype(BF16)
    cs, sn = _rope_tables(n, n_ctx)
    hid = jnp.arange(RW_W) // RW_HEAD
    ones_bd = (hid[:, None] == hid[None, :]).astype(BF16)

    q, k, v, frw = _in0(xc, mod0, win0, row(l0_mla_q_norm), row(l0_mla_kv_norm), wqa, wqb, wuk, wuv, cs, sn)
    att = _attention(q, k, v)
    sf, g, bonus = _rwfeat(frw, row(l0_rwkv_mu), row(l0_rwkv_w0), _block_diag2(l0_rwkv_w2).astype(BF16),
                           row(l0_rwkv_a0), _block_diag2(l0_rwkv_a2).astype(BF16), l0_rwkv_g2.astype(BF16),
                           row(l0_rwkv_k_k), row(l0_rwkv_k_a), row(l0_rwkv_r_k), ones_bd)
    ys = _rwkv_scan_both(sf)
    wo = l0_w_out.astype(BF16)
    xc = _out0(att, ys, bonus, g, xc, mod0, row(l0_rwkv_gn_w), row(l0_rwkv_gn_b), ones_bd,
               wo[:HEADS * MLA_V], wo[HEADS * MLA_V:], row(l0_ln1_g), row(l0_ln1_b))
    xc = _mlp(xc, mod0, l0_mlp_w1.astype(BF16), l0_mlp_w2.astype(BF16), row(l0_ln2_g), row(l0_ln2_b),
              combined=True, row_off=0)

    gate, xr = _in1(xc, mod1, l1_w_in.astype(BF16))
    wbd = jnp.concatenate([l1_lru_ga_w[0], l1_lru_gx_w[0], l1_lru_ga_w[1], l1_lru_gx_w[1]], axis=-1).astype(BF16)
    gb = jnp.stack([l1_lru_ga_b[0], l1_lru_gx_b[0], l1_lru_ga_b[1], l1_lru_gx_b[1]])
    a0, u0, a1, u1 = _lrufeat(xr, l1_conv_w, row(l1_conv_b), wbd, gb, l1_lru_lambda)
    hf, hr = _lruscan(a0, u0, a1, u1)
    xl = _out1(gate, hf, hr, xc, mod1, l1_w_out.astype(BF16), row(l1_ln1_g), row(l1_ln1_b))
    return _mlp(xl, mod1, l1_mlp_w1.astype(BF16), l1_mlp_w2.astype(BF16), row(l1_ln2_g), row(l1_ln2_b),
                combined=False, row_off=0)
```

```python
import functools
import math

import jax
import jax.numpy as jnp
from jax import lax
from jax.experimental import pallas as pl
from jax.experimental.pallas import tpu as pltpu

F32 = jnp.float32
BF16 = jnp.bfloat16

D = 1024
DEPTH = 2
N_MOD = 6
ALPHA = (2.0 * DEPTH) ** 0.25
LN_EPS = 1e-5
RMS_EPS = 1e-6

HEADS = 8
MLA_NOPE = 64
MLA_ROPE = 32
MLA_V = 64
MLA_Q_RANK = 384
MLA_KV_RANK = 256
ROPE_AXIS = MLA_ROPE // 2
ROPE_THETA = 10000.0
GRID_W = 64
ATT_SCALE = (MLA_NOPE + MLA_ROPE) ** -0.5
HEAD_PAD = 128

RW_HEAD = 64
RW_W = HEADS * RW_HEAD
LORA_W = 64
LORA_A = 64
LORA_G = 128
RW_IN = 3 * RW_W + 2 * LORA_W + 2 * LORA_A + LORA_G
GN_EPS = 64e-5
EXP_NEG_HALF = math.exp(-0.5)

LRU_W = D
LRU_BLOCKS = 8
LRU_BLOCK = LRU_W // LRU_BLOCKS
LRU_C = 8.0
D_FF = 4 * D

TM = 256
SUB = 8
SCAN_TC = 32
RELAY_T = 128
PITCH_F = RW_HEAD + SUB
VMEM_LIMIT = 56 * 1024 * 1024

IN0_COLS = MLA_Q_RANK + MLA_KV_RANK + 2 * HEAD_PAD + RW_IN


def _params(sem):
    return pltpu.CompilerParams(dimension_semantics=sem, vmem_limit_bytes=VMEM_LIMIT)


def _bdot(a, w):
    return jnp.dot(a.astype(BF16), w, preferred_element_type=F32)


def _sigmoid(x):
    return 1.0 / (1.0 + jnp.exp(-x))


def _layer_norm(z, g, b):
    mu = jnp.mean(z, axis=-1, keepdims=True)
    zc = z - mu
    var = jnp.mean(zc * zc, axis=-1, keepdims=True)
    return zc * lax.rsqrt(var + LN_EPS) * g + b


def _head_sum(x, ones_bd):
    hi = x.astype(BF16)
    lo = (x - hi.astype(F32)).astype(BF16)
    return (jnp.dot(hi, ones_bd, preferred_element_type=F32)
            + jnp.dot(lo, ones_bd, preferred_element_type=F32))


def _const_spec(shape):
    nd = len(shape)
    return pl.BlockSpec(shape, lambda *_: (0,) * nd)


def _mod_kernel(c_ref, w_ref, b_ref, o_ref):
    c = c_ref[...]
    s = c * _sigmoid(c)
    o_ref[...] = jnp.dot(s, w_ref[...], precision=lax.Precision.HIGHEST,
                         preferred_element_type=F32) + b_ref[...]


def _mod_table(c, c_ctx, mod_w, mod_b):
    b = c.shape[0]
    rows = 16
    cc = jnp.zeros((rows, D), F32).at[:b].set(c).at[b].set(c_ctx)
    tn = 1024
    out = pl.pallas_call(
        _mod_kernel,
        grid=(N_MOD * D // tn,),
        in_specs=[pl.BlockSpec((rows, D), lambda j: (0, 0)),
                  pl.BlockSpec((D, tn), lambda j: (0, j)),
                  pl.BlockSpec((1, tn), lambda j: (0, j))],
        out_specs=pl.BlockSpec((rows, tn), lambda j: (0, j)),
        out_shape=jax.ShapeDtypeStruct((rows, N_MOD * D), F32),
        compiler_params=_params(("arbitrary",)),
        name="mod_proj",
    )(cc, mod_w, mod_b.reshape(1, -1))
    lat = out[:b].reshape(b, N_MOD, D)
    ctx = jnp.broadcast_to(out[b].reshape(1, N_MOD, D), (b, N_MOD, D))
    tab = jnp.stack([ctx, lat], axis=1)
    return jnp.pad(tab, ((0, 0), (0, 0), (0, SUB - N_MOD), (0, 0)))


def _mod_spec(combined):
    if combined:
        return pl.BlockSpec((1, 1, SUB, D), lambda b, i: (b, jnp.minimum(i, 1), 0, 0))
    return pl.BlockSpec((1, 1, SUB, D), lambda b, i: (b, 1, 0, 0))


def _in0_kernel(x_ref, mod_ref, win_ref, qn_ref, kvn_ref, wqa_ref, wqb_ref, wuk_ref, wuv_ref,
                cs_ref, sn_ref, q_ref, k_ref, v_ref, frw_ref):
    x = x_ref[0]
    m = mod_ref[0, 0]
    h = x * (1.0 + m[1:2]) + m[0:1]
    f = _bdot(h, win_ref[...])
    o_kv = MLA_Q_RANK
    o_kr = o_kv + MLA_KV_RANK
    o_rw = o_kr + 2 * HEAD_PAD
    fq = f[:, :o_kv]
    fkv = f[:, o_kv:o_kr]
    fkr = f[:, o_kr:o_kr + HEAD_PAD]
    fkr_rot = f[:, o_kr + HEAD_PAD:o_rw]
    frw_ref[0] = f[:, o_rw:]
    qn = fq * lax.rsqrt(jnp.mean(fq * fq, axis=-1, keepdims=True) + RMS_EPS) * qn_ref[...]
    ckv = fkv * lax.rsqrt(jnp.mean(fkv * fkv, axis=-1, keepdims=True) + RMS_EPS) * kvn_ref[...]
    qn = qn.astype(BF16)
    ckv = ckv.astype(BF16)
    qa = jnp.dot(qn, wqa_ref[...], preferred_element_type=F32)
    qb = jnp.dot(qn, wqb_ref[...], preferred_element_type=F32)
    kn = jnp.dot(ckv, wuk_ref[...], preferred_element_type=F32)
    cs = cs_ref[...]
    sn = sn_ref[...]
    kr = fkr * cs + fkr_rot * sn
    for hd in range(HEADS):
        sl = slice(hd * HEAD_PAD, (hd + 1) * HEAD_PAD)
        q_ref[0, :, sl] = ((qa[:, sl] * cs + qb[:, sl] * sn) * ATT_SCALE).astype(BF16)
        k_ref[0, :, sl] = (kn[:, sl] + kr).astype(BF16)
    v_ref[0] = jnp.dot(ckv, wuv_ref[...], preferred_element_type=F32).astype(BF16)


def _in0(xc, modt, win, qn, kvn, wqa, wqb, wuk, wuv, cs, sn):
    b, t, _ = xc.shape
    row = lambda w: pl.BlockSpec((1, TM, w), lambda bb, i: (bb, i, 0))
    return pl.pallas_call(
        _in0_kernel,
        grid=(b, t // TM),
        in_specs=[row(D), _mod_spec(True), _const_spec(win.shape), _const_spec(qn.shape),
                  _const_spec(kvn.shape), _const_spec(wqa.shape), _const_spec(wqb.shape),
                  _const_spec(wuk.shape), _const_spec(wuv.shape),
                  pl.BlockSpec((TM, HEAD_PAD), lambda bb, i: (i, 0)),
                  pl.BlockSpec((TM, HEAD_PAD), lambda bb, i: (i, 0))],
        out_specs=[row(HEADS * HEAD_PAD), row(HEADS * HEAD_PAD), row(HEADS * MLA_V), row(RW_IN)],
        out_shape=[jax.ShapeDtypeStruct((b, t, HEADS * HEAD_PAD), BF16),
                   jax.ShapeDtypeStruct((b, t, HEADS * HEAD_PAD), BF16),
                   jax.ShapeDtypeStruct((b, t, HEADS * MLA_V), BF16),
                   jax.ShapeDtypeStruct((b, t, RW_IN), F32)],
        compiler_params=_params(("parallel", "parallel")),
        name="l0_in_proj",
    )(xc, modt, win, qn, kvn, wqa, wqb, wuk, wuv, cs, sn)


def _att_kernel(q_ref, k_ref, v_ref, o_ref, *, n_ctx, n_all):
    i = pl.program_id(2)

    def attend(nk):
        v = v_ref[0, :nk, :]
        lane = lax.broadcasted_iota(jnp.int32, v.shape, 1)
        acc = None
        for hh in range(2):
            sl = slice(hh * HEAD_PAD, (hh + 1) * HEAD_PAD)
            q = q_ref[0, :, sl]
            k = k_ref[0, :nk, sl]
            s = lax.dot_general(q, k, (((1,), (1,)), ((), ())), preferred_element_type=F32)
            p = jnp.exp(s - jnp.max(s, axis=-1, keepdims=True))
            l = jnp.sum(p, axis=-1, keepdims=True)
            keep = (lane < MLA_V) if hh == 0 else (lane >= MLA_V)
            vh = jnp.where(keep, v, jnp.zeros_like(v))
            o = jnp.dot(p.astype(BF16), vh, preferred_element_type=F32) / l
            acc = o if acc is None else acc + o
        o_ref[0] = acc.astype(o_ref.dtype)

    @pl.when(i == 0)
    def _():
        attend(n_ctx)

    @pl.when(i > 0)
    def _():
        attend(n_all)


def _attention(q, k, v):
    b, t, _ = q.shape
    return pl.pallas_call(
        functools.partial(_att_kernel, n_ctx=TM, n_all=t),
        grid=(b, HEADS // 2, t // TM),
        in_specs=[pl.BlockSpec((1, TM, 2 * HEAD_PAD), lambda bb, hp, i: (bb, i, hp)),
                  pl.BlockSpec((1, t, 2 * HEAD_PAD), lambda bb, hp, i: (bb, 0, hp)),
                  pl.BlockSpec((1, t, 2 * MLA_V), lambda bb, hp, i: (bb, 0, hp))],
        out_specs=pl.BlockSpec((1, TM, 2 * MLA_V), lambda bb, hp, i: (bb, i, hp)),
        out_shape=jax.ShapeDtypeStruct((b, t, HEADS * MLA_V), BF16),
        compiler_params=_params(("parallel", "parallel", "parallel")),
        name="l0_attention",
    )(q, k, v)


def _halo_flags(i, n_tiles):
    return i >= 2, jnp.logical_and(i >= 1, i < n_tiles - 1)


def _rwfeat_kernel(f_ref, fp_ref, fn_ref, mu_ref, w0_ref, w2_ref, a0_ref, a2_ref, g2_ref,
                   kk_ref, ka_ref, rk_ref, ones_ref, sf_ref, g_ref, bonus_ref):
    i = pl.program_id(1)
    has_prev, has_next = _halo_flags(i, pl.num_programs(1))
    f = f_ref[0]
    prow = jnp.where(has_prev, fp_ref[0, SUB - 1:SUB, :], 0.0)
    nrow = jnp.where(has_next, fn_ref[0, 0:1, :], 0.0)
    rid = lax.broadcasted_iota(jnp.int32, (TM, 1), 0)
    prev = jnp.where(rid == 0, prow, pltpu.roll(f, 1, axis=0))
    nxt = jnp.where(rid == TM - 1, nrow, pltpu.roll(f, TM - 1, axis=0))
    f = f + mu_ref[...] * (0.5 * (prev + nxt) - f)
    r = f[:, 0:RW_W]
    k = f[:, RW_W:2 * RW_W]
    v = f[:, 2 * RW_W:3 * RW_W]
    o = 3 * RW_W
    wl = f[:, o:o + 2 * LORA_W]
    al = f[:, o + 2 * LORA_W:o + 2 * LORA_W + 2 * LORA_A]
    gl = f[:, o + 2 * LORA_W + 2 * LORA_A:]
    w_raw = w0_ref[...] + _bdot(jnp.tanh(wl), w2_ref[...])
    decay = jnp.exp(-EXP_NEG_HALF * _sigmoid(w_raw))
    a = _sigmoid(a0_ref[...] + _bdot(al, a2_ref[...]))
    g_ref[0] = _bdot(_sigmoid(gl), g2_ref[...])
    ones_bd = ones_ref[...]
    kk = k * kk_ref[...]
    kk = kk * lax.rsqrt(_head_sum(kk * kk, ones_bd) + 1e-12)
    ka = ka_ref[...]
    kd = [k * (1.0 + (a[:, d * RW_W:(d + 1) * RW_W] - 1.0) * ka) for d in range(2)]
    bonus_ref[0] = _head_sum(r * (kd[0] + kd[1]) * rk_ref[...], ones_bd) * v
    parts = [r, kk, v, decay[:, :RW_W], kd[0], kk * a[:, :RW_W],
             decay[:, RW_W:], kd[1], kk * a[:, RW_W:]]
    for n, p in enumerate(parts):
        sf_ref[0, :, n * RW_W:(n + 1) * RW_W] = p


def _halo_specs(width, t):
    nb = TM // SUB
    last = t // SUB - 1
    prev = pl.BlockSpec((1, SUB, width), lambda bb, i: (bb, jnp.maximum(i * nb - 1, 0), 0))
    nxt = pl.BlockSpec((1, SUB, width), lambda bb, i: (bb, jnp.minimum((i + 1) * nb, last), 0))
    return prev, nxt


def _rwfeat(frw, mu, w0, w2bd, a0, a2bd, g2, k_k, k_a, r_k, ones_bd):
    b, t, _ = frw.shape
    row = lambda w: pl.BlockSpec((1, TM, w), lambda bb, i: (bb, i, 0))
    prev, nxt = _halo_specs(RW_IN, t)
    consts = (mu, w0, w2bd, a0, a2bd, g2, k_k, k_a, r_k, ones_bd)
    return pl.pallas_call(
        _rwfeat_kernel,
        grid=(b, t // TM),
        in_specs=[row(RW_IN), prev, nxt] + [_const_spec(c.shape) for c in consts],
        out_specs=[row(9 * RW_W), row(RW_W), row(RW_W)],
        out_shape=[jax.ShapeDtypeStruct((b, t, 9 * RW_W), F32),
                   jax.ShapeDtypeStruct((b, t, RW_W), F32),
                   jax.ShapeDtypeStruct((b, t, RW_W), F32)],
        compiler_params=_params(("parallel", "parallel")),
        name="l0_rwkv_feat",
    )(frw, frw, frw, *consts)


def _relayout_in_kernel(xa_ref, xb_ref, o_ref, q_ref):
    nb = xa_ref.shape[0]
    chains = 2 * nb * HEADS
    for a2, x_ref in enumerate((xa_ref, xb_ref)):
        for b in range(nb):
            for hp in range(HEADS // 2):
                xt = x_ref[b, :, hp * 128:(hp + 1) * 128].T
                for h2 in range(2):
                    row0 = ((a2 * nb + b) * HEADS + 2 * hp + h2) * PITCH_F
                    q_ref[row0:row0 + RW_HEAD, :] = xt[h2 * RW_HEAD:(h2 + 1) * RW_HEAD]
    for f in range(RW_HEAD):
        o_ref[0, f] = q_ref[pl.ds(f, chains, stride=PITCH_F), :].T


def _relayout_in(sf):
    b, t, _ = sf.shape
    chains = 2 * b * HEADS
    fwd = pl.BlockSpec((b, RELAY_T, RW_W), lambda i, p: (0, i, p))
    rev = pl.BlockSpec((b, RELAY_T, RW_W), lambda i, p: (0, i, p + 3 * (p // 3)))
    return pl.pallas_call(
        _relayout_in_kernel,
        grid=(t // RELAY_T, 6),
        in_specs=[fwd, rev],
        out_specs=pl.BlockSpec((1, RW_HEAD, RELAY_T, chains), lambda i, p: (p, 0, i, 0)),
        out_shape=jax.ShapeDtypeStruct((6, RW_HEAD, t, chains), F32),
        scratch_shapes=[pltpu.VMEM((chains * PITCH_F, RELAY_T), F32)],
        compiler_params=_params(("parallel", "parallel")),
        name="l0_rwkv_relayout_in",
    )(sf, sf)


def _rwscan_kernel(xf_ref, xr_ref, yf_ref, yr_ref, s_ref):
    tc = xf_ref.shape[2]
    chains = xf_ref.shape[3]
    fwd8 = lax.broadcasted_iota(jnp.int32, (SUB, chains), 1) < chains // 2
    fwd64 = lax.broadcasted_iota(jnp.int32, (RW_HEAD, chains), 1) < chains // 2

    def row(a, j, s):
        f = jnp.broadcast_to(xf_ref[a, j, pl.ds(s, 1), :], (SUB, chains))
        r = jnp.broadcast_to(xr_ref[a, j, pl.ds(tc - 1 - s, 1), :], (SUB, chains))
        return pltpu.repeat(jnp.where(fwd8, f, r), RW_HEAD // SUB, axis=0)

    @pl.when(pl.program_id(0) == 0)
    def _():
        s_ref[...] = jnp.zeros_like(s_ref)

    sa0 = jnp.zeros((RW_HEAD, chains), F32)
    for j in range(RW_HEAD):
        sa0 = sa0 + s_ref[j] * row(1, j, 0)

    def step(s, sa):
        v = jnp.where(fwd64, xf_ref[2, :, s, :], xr_ref[2, :, tc - 1 - s, :])
        s_next = jnp.minimum(s + 1, tc - 1)
        y = jnp.zeros_like(v)
        sa_next = jnp.zeros_like(v)
        for j in range(RW_HEAD):
            sj = s_ref[j] * row(3, j, s) - sa * row(5, j, s) + v * row(4, j, s)
            s_ref[j] = sj
            y = y + sj * row(0, j, s)
            sa_next = sa_next + sj * row(1, j, s_next)
        pad = jnp.zeros((PITCH_F - RW_HEAD, chains), F32)
        for y_ref, ts in ((yf_ref, s), (yr_ref, tc - 1 - s)):
            base = pl.multiple_of(ts * PITCH_F, SUB)
            y_ref[pl.ds(base, RW_HEAD), :] = y
            y_ref[pl.ds(base + RW_HEAD, PITCH_F - RW_HEAD), :] = pad
        return sa_next

    lax.fori_loop(0, tc, step, sa0)


def _rwscan(x):
    _, _, t, chains = x.shape
    nt = t // SCAN_TC
    nc = TM // SCAN_TC
    mirror = lambda i: jnp.where(i < nc, nc - 1 - i, nt + nc - 1 - i)
    return pl.pallas_call(
        _rwscan_kernel,
        grid=(nt,),
        in_specs=[pl.BlockSpec((6, RW_HEAD, SCAN_TC, chains), lambda i: (0, 0, i, 0)),
                  pl.BlockSpec((6, RW_HEAD, SCAN_TC, chains), lambda i: (0, 0, mirror(i), 0))],
        out_specs=[pl.BlockSpec((SCAN_TC * PITCH_F, chains), lambda i: (i, 0)),
                   pl.BlockSpec((SCAN_TC * PITCH_F, chains), lambda i: (mirror(i), 0))],
        out_shape=[jax.ShapeDtypeStruct((t * PITCH_F, chains), F32)] * 2,
        scratch_shapes=[pltpu.VMEM((RW_HEAD, RW_HEAD, chains), F32)],
        compiler_params=_params(("arbitrary",)),
        name="l0_rwkv_scan",
    )(x, x)


def _relayout_out_kernel(yf_ref, yr_ref, o_ref, q_ref):
    nb = o_ref.shape[0]
    tr = o_ref.shape[1]
    chains = yf_ref.shape[1]
    pitch_c = chains + SUB
    fwd = lax.broadcasted_iota(jnp.int32, (tr, chains), 1) < chains // 2
    for i in range(RW_HEAD):
        rows = pl.ds(i, tr, stride=PITCH_F)
        q_ref[i * pitch_c:i * pitch_c + chains, :] = jnp.where(fwd, yf_ref[rows, :], yr_ref[rows, :]).T
    for b in range(nb):
        for hp in range(HEADS // 2):
            parts = []
            for h2 in range(2):
                c = b * HEADS + 2 * hp + h2
                parts.append(q_ref[pl.ds(c, RW_HEAD, stride=pitch_c), :]
                             + q_ref[pl.ds(chains // 2 + c, RW_HEAD, stride=pitch_c), :])
            o_ref[b, :, hp * 128:(hp + 1) * 128] = jnp.concatenate(parts, axis=0).T


def _relayout_out(yf, yr, b):
    chains = yf.shape[1]
    t = yf.shape[0] // PITCH_F
    blk = pl.BlockSpec((RELAY_T * PITCH_F, chains), lambda i: (i, 0))
    return pl.pallas_call(
        _relayout_out_kernel,
        grid=(t // RELAY_T,),
        in_specs=[blk, blk],
        out_specs=pl.BlockSpec((b, RELAY_T, RW_W), lambda i: (0, i, 0)),
        out_shape=jax.ShapeDtypeStruct((b, t, RW_W), F32),
        scratch_shapes=[pltpu.VMEM((RW_HEAD * (chains + SUB), RELAY_T), F32)],
        compiler_params=_params(("parallel",)),
        name="l0_rwkv_relayout_out",
    )(yf, yr)


def _rwkv_scan_both(sf):
    yf, yr = _rwscan(_relayout_in(sf))
    return _relayout_out(yf, yr, sf.shape[0])


def _out0_kernel(att_ref, ys_ref, bonus_ref, g_ref, x_ref, mod_ref, gnw_ref, gnb_ref, ones_ref,
                 woa_ref, wob_ref, lng_ref, lnb_ref, o_ref):
    ones_bd = ones_ref[...]
    y = ys_ref[0]
    mu = _head_sum(y, ones_bd) * (1.0 / RW_HEAD)
    yc = y - mu
    var = _head_sum(yc * yc, ones_bd) * (1.0 / RW_HEAD)
    yn = yc * lax.rsqrt(var + GN_EPS) * gnw_ref[...] + gnb_ref[...]
    rw = (yn + bonus_ref[0]) * g_ref[0]
    o = (jnp.dot(att_ref[0], woa_ref[...], preferred_element_type=F32)
         + _bdot(rw, wob_ref[...]))
    m = mod_ref[0, 0]
    o_ref[0] = _layer_norm(ALPHA * x_ref[0] + m[2:3] * o, lng_ref[...], lnb_ref[...])


def _out0(att, ys, bonus, g, xc, modt, gnw, gnb, ones_bd, woa, wob, lng, lnb):
    b, t, _ = xc.shape
    row = lambda w: pl.BlockSpec((1, TM, w), lambda bb, i: (bb, i, 0))
    consts = (gnw, gnb, ones_bd, woa, wob, lng, lnb)
    return pl.pallas_call(
        _out0_kernel,
        grid=(b, t // TM),
        in_specs=[row(RW_W), row(RW_W), row(RW_W), row(RW_W), row(D), _mod_spec(True)]
                 + [_const_spec(c.shape) for c in consts],
        out_specs=row(D),
        out_shape=jax.ShapeDtypeStruct((b, t, D), F32),
        compiler_params=_params(("parallel", "parallel")),
        name="l0_out_proj",
    )(att, ys, bonus, g, xc, modt, *consts)


def _mlp_kernel(x_ref, mod_ref, w1_ref, w2_ref, lng_ref, lnb_ref, o_ref):
    x = x_ref[0]
    m = mod_ref[0, 0]
    h = (x * (1.0 + m[4:5]) + m[3:4]).astype(BF16)
    acc = None
    fc = 1024
    for c in range(D_FF // fc):
        u = jnp.dot(h, w1_ref[:, c * fc:(c + 1) * fc], preferred_element_type=F32)
        u = jnp.square(jnp.maximum(u, 0.0)).astype(BF16)
        part = jnp.dot(u, w2_ref[c * fc:(c + 1) * fc, :], preferred_element_type=F32)
        acc = part if acc is None else acc + part
    o_ref[0] = _layer_norm(ALPHA * x + m[5:6] * acc, lng_ref[...], lnb_ref[...])


def _mlp(x, modt, w1, w2, lng, lnb, *, combined, row_off):
    b, tx, _ = x.shape
    nt = tx // TM - row_off
    single = pl.Buffered(1)
    wspec = lambda shape: pl.BlockSpec(shape, lambda *_: (0,) * len(shape), pipeline_mode=single)
    return pl.pallas_call(
        _mlp_kernel,
        grid=(b, nt),
        in_specs=[pl.BlockSpec((1, TM, D), lambda bb, i: (bb, i + row_off, 0)), _mod_spec(combined),
                  wspec(w1.shape), wspec(w2.shape), _const_spec(lng.shape), _const_spec(lnb.shape)],
        out_specs=pl.BlockSpec((1, TM, D), lambda bb, i: (bb, i, 0)),
        out_shape=jax.ShapeDtypeStruct((b, nt * TM, D), F32),
        compiler_params=_params(("parallel", "parallel")),
        name="mlp",
    )(x, modt, w1, w2, lng, lnb)


def _in1_kernel(x_ref, mod_ref, w_ref, gate_ref, xr_ref):
    m = mod_ref[0, 0]
    h = x_ref[0] * (1.0 + m[1:2]) + m[0:1]
    f = _bdot(h, w_ref[...])
    gate_ref[0] = f[:, :LRU_W]
    xr_ref[0] = f[:, LRU_W:]


def _in1(xc, modt, w):
    b, t, _ = xc.shape
    row = lambda wd: pl.BlockSpec((1, TM, wd), lambda bb, i: (bb, i, 0))
    return pl.pallas_call(
        _in1_kernel,
        grid=(b, t // TM),
        in_specs=[row(D), _mod_spec(True), _const_spec(w.shape)],
        out_specs=[row(LRU_W), row(LRU_W)],
        out_shape=[jax.ShapeDtypeStruct((b, t, LRU_W), F32)] * 2,
        compiler_params=_params(("parallel", "parallel")),
        name="l1_in_proj",
    )(xc, modt, w)


def _lrufeat_kernel(x_ref, xp_ref, xn_ref, cw_ref, cb_ref, wbd_ref, gb_ref, lam_ref,
                    a0_ref, u0_ref, a1_ref, u1_ref):
    i = pl.program_id(1)
    has_prev, has_next = _halo_flags(i, pl.num_programs(1))
    x = x_ref[0]
    p2 = jnp.where(has_prev, xp_ref[0, SUB - 2:SUB - 1, :], 0.0)
    p1 = jnp.where(has_prev, xp_ref[0, SUB - 1:SUB, :], 0.0)
    n1 = jnp.where(has_next, xn_ref[0, 0:1, :], 0.0)
    rid = lax.broadcasted_iota(jnp.int32, (TM, 1), 0)
    xm1 = jnp.where(rid == 0, p1, pltpu.roll(x, 1, axis=0))
    xm2 = jnp.where(rid == 0, p2, jnp.where(rid == 1, p1, pltpu.roll(x, 2, axis=0)))
    xp1 = jnp.where(rid == TM - 1, n1, pltpu.roll(x, TM - 1, axis=0))
    cw = cw_ref[...]
    xc = cw[0:1] * xm2 + cw[1:2] * xm1 + cw[2:3] * x + cw[3:4] * xp1 + cb_ref[...]
    lam = lam_ref[...]
    nl = -lam
    softplus = jnp.maximum(nl, 0.0) + jnp.log1p(jnp.exp(-jnp.abs(nl)))
    gb = gb_ref[...]
    outs = ((a0_ref, u0_ref), (a1_ref, u1_ref))
    for blk in range(LRU_BLOCKS):
        sl = slice(blk * LRU_BLOCK, (blk + 1) * LRU_BLOCK)
        xb = xc[:, sl]
        z = _bdot(xb, wbd_ref[blk])
        for d in range(2):
            zr = z[:, (2 * d) * LRU_BLOCK:(2 * d + 1) * LRU_BLOCK] + gb[2 * d:2 * d + 1, sl]
            zi = z[:, (2 * d + 1) * LRU_BLOCK:(2 * d + 2) * LRU_BLOCK] + gb[2 * d + 1:2 * d + 2, sl]
            log_a = -LRU_C * _sigmoid(zr) * softplus[d:d + 1, sl]
            a_ref, u_ref = outs[d]
            a = jnp.exp(log_a)
            a_ref[0, :, sl] = a
            u_ref[0, :, sl] = jnp.sqrt(-jnp.tanh(log_a) * (a * a + 1.0)) * (_sigmoid(zi) * xb)


def _lrufeat(xr, cw, cb, wbd, gb, lam):
    b, t, _ = xr.shape
    row = lambda w: pl.BlockSpec((1, TM, w), lambda bb, i: (bb, i, 0))
    prev, nxt = _halo_specs(LRU_W, t)
    consts = (cw, cb, wbd, gb, lam)
    return pl.pallas_call(
        _lrufeat_kernel,
        grid=(b, t // TM),
        in_specs=[row(LRU_W), prev, nxt] + [_const_spec(c.shape) for c in consts],
        out_specs=[row(LRU_W)] * 4,
        out_shape=[jax.ShapeDtypeStruct((b, t, LRU_W), F32)] * 4,
        compiler_params=_params(("parallel", "parallel")),
        name="l1_lru_feat",
    )(xr, xr, xr, *consts)


def _lruscan_kernel(af_ref, uf_ref, ar_ref, ur_ref, hf_ref, hr_ref, cf_ref, cr_ref):
    @pl.when(pl.program_id(1) == 0)
    def _():
        cf_ref[...] = jnp.zeros_like(cf_ref)
        cr_ref[...] = jnp.zeros_like(cr_ref)

    def body(t, carry):
        hf, hr = carry
        hf = af_ref[0, pl.ds(t, 1), :] * hf + uf_ref[0, pl.ds(t, 1), :]
        hf_ref[0, pl.ds(t, 1), :] = hf
        tr = TM - 1 - t
        hr = ar_ref[0, pl.ds(tr, 1), :] * hr + ur_ref[0, pl.ds(tr, 1), :]
        hr_ref[0, pl.ds(tr, 1), :] = hr
        return hf, hr

    hf, hr = lax.fori_loop(0, TM, body, (cf_ref[...], cr_ref[...]))
    cf_ref[...] = hf
    cr_ref[...] = hr


def _lruscan(a0, u0, a1, u1):
    b, t, _ = a0.shape
    nt = t // TM
    fwd = pl.BlockSpec((1, TM, LRU_W), lambda bb, i: (bb, i, 0))
    rev = pl.BlockSpec((1, TM, LRU_W), lambda bb, i: (bb, jnp.where(i == 0, 0, nt - i), 0))
    return pl.pallas_call(
        _lruscan_kernel,
        grid=(b, nt),
        in_specs=[fwd, fwd, rev, rev],
        out_specs=[fwd, rev],
        out_shape=[jax.ShapeDtypeStruct((b, t, LRU_W), F32)] * 2,
        scratch_shapes=[pltpu.VMEM((1, LRU_W), F32), pltpu.VMEM((1, LRU_W), F32)],
        compiler_params=_params(("parallel", "arbitrary")),
        name="l1_lru_scan",
    )(a0, u0, a1, u1)


def _out1_kernel(gate_ref, hf_ref, hr_ref, x_ref, mod_ref, w_ref, lng_ref, lnb_ref, o_ref):
    gate = gate_ref[0]
    gelu = 0.5 * gate * (1.0 + jnp.tanh(math.sqrt(2.0 / math.pi) * (gate + 0.044715 * gate * gate * gate)))
    o = _bdot(gelu * (hf_ref[0] + hr_ref[0]), w_ref[...])
    m = mod_ref[0, 0]
    o_ref[0] = _layer_norm(ALPHA * x_ref[0] + m[2:3] * o, lng_ref[...], lnb_ref[...])


def _out1(gate, hf, hr, xc, modt, w, lng, lnb):
    b, t, _ = xc.shape
    nt = t // TM - 1
    lat = lambda wd: pl.BlockSpec((1, TM, wd), lambda bb, i: (bb, i + 1, 0))
    consts = (w, lng, lnb)
    return pl.pallas_call(
        _out1_kernel,
        grid=(b, nt),
        in_specs=[lat(LRU_W), lat(LRU_W), lat(LRU_W), lat(D), _mod_spec(False)]
                 + [_const_spec(c.shape) for c in consts],
        out_specs=pl.BlockSpec((1, TM, D), lambda bb, i: (bb, i, 0)),
        out_shape=jax.ShapeDtypeStruct((b, nt * TM, D), F32),
        compiler_params=_params(("parallel", "parallel")),
        name="l1_out_proj",
    )(gate, hf, hr, xc, modt, *consts)


def _rot_cols(w):
    ws = w.reshape(w.shape[:-1] + (2, 2, ROPE_AXIS // 2))
    return jnp.stack([-ws[..., 1, :], ws[..., 0, :]], axis=-2).reshape(w.shape)


def _rope_tables(n, n_ctx):
    rows_n = n // GRID_W
    rows = jnp.repeat(jnp.arange(rows_n, dtype=F32), GRID_W)
    cols = jnp.tile(jnp.arange(GRID_W, dtype=F32), rows_n)
    inv_freq = ROPE_THETA ** (-jnp.arange(0, ROPE_AXIS, 2, dtype=F32) / ROPE_AXIS)
    ang_r = rows[:, None] * inv_freq
    ang_c = cols[:, None] * inv_freq
    ang = jnp.concatenate([ang_r, ang_r, ang_c, ang_c], axis=-1)
    cos = jnp.concatenate([jnp.ones((n_ctx, MLA_ROPE), F32), jnp.cos(ang)], axis=0)
    sin = jnp.concatenate([jnp.zeros((n_ctx, MLA_ROPE), F32), jnp.sin(ang)], axis=0)
    t = n + n_ctx
    cs = jnp.concatenate([jnp.ones((t, MLA_NOPE), F32), cos, jnp.zeros((t, 32), F32)], axis=-1)
    sn = jnp.concatenate([jnp.zeros((t, MLA_NOPE), F32), sin, jnp.zeros((t, 32), F32)], axis=-1)
    return cs, sn


def _block_diag2(w):
    z = jnp.zeros_like(w[0])
    return jnp.concatenate([jnp.concatenate([w[0], z], axis=1), jnp.concatenate([z, w[1]], axis=1)], axis=0)


def kernel(x, c, ctx, c_ctx, l0_mod_w, l0_mod_b, l0_w_in, l0_mla_q_norm, l0_mla_w_uq, l0_mla_kv_norm, l0_mla_w_uk, l0_mla_w_uv, l0_rwkv_mu, l0_rwkv_w0, l0_rwkv_w2, l0_rwkv_a0, l0_rwkv_a2, l0_rwkv_g2, l0_rwkv_k_k, l0_rwkv_k_a, l0_rwkv_r_k, l0_rwkv_gn_w, l0_rwkv_gn_b, l0_w_out, l0_ln1_g, l0_ln1_b, l0_mlp_w1, l0_mlp_w2, l0_ln2_g, l0_ln2_b, l1_mod_w, l1_mod_b, l1_w_in, l1_conv_w, l1_conv_b, l1_lru_ga_w, l1_lru_ga_b, l1_lru_gx_w, l1_lru_gx_b, l1_lru_lambda, l1_w_out, l1_ln1_g, l1_ln1_b, l1_mlp_w1, l1_mlp_w2, l1_ln2_g, l1_ln2_b):
    b, n, _ = x.shape
    n_ctx = ctx.shape[1]
    assert n_ctx == TM and n % TM == 0 and x.shape[2] == D
    row = lambda v: v.reshape(1, -1)

    xc = jnp.concatenate([ctx, x], axis=1)
    mod0 = _mod_table(c, c_ctx, l0_mod_w, l0_mod_b)
    mod1 = _mod_table(c, c_ctx, l1_mod_w, l1_mod_b)

    o_kv = MLA_Q_RANK
    o_kr = o_kv + MLA_KV_RANK
    o_rw = o_kr + MLA_ROPE
    w_kr = l0_w_in[:, o_kr:o_rw]
    zl = jnp.zeros((D, MLA_NOPE), F32)
    zr = jnp.zeros((D, HEAD_PAD - MLA_NOPE - MLA_ROPE), F32)
    win0 = jnp.concatenate([l0_w_in[:, :o_kr], zl, w_kr, zr, zl, _rot_cols(w_kr), zr,
                            l0_w_in[:, o_rw:]], axis=1).astype(BF16)
    wq = l0_mla_w_uq.reshape(MLA_Q_RANK, HEADS, MLA_NOPE + MLA_ROPE)
    q_nope, q_rope = wq[..., :MLA_NOPE], wq[..., MLA_NOPE:]
    zq = jnp.zeros((MLA_Q_RANK, HEADS, 32), F32)
    wqa = jnp.concatenate([q_nope, q_rope, zq], axis=-1).reshape(MLA_Q_RANK, HEADS * HEAD_PAD).astype(BF16)
    wqb = jnp.concatenate([jnp.zeros_like(q_nope), _rot_cols(q_rope), zq],
                          axis=-1).reshape(MLA_Q_RANK, HEADS * HEAD_PAD).astype(BF16)
    wk = l0_mla_w_uk.reshape(MLA_KV_RANK, HEADS, MLA_NOPE)
    wuk = jnp.concatenate([wk, jnp.zeros_like(wk)], axis=-1).reshape(MLA_KV_RANK, HEADS * HEAD_PAD).astype(BF16)
    wuv = l0_mla_w_uv.astype(BF16)
    cs, sn = _rope_tables(n, n_ctx)
    hid = jnp.arange(RW_W) // RW_HEAD
    ones_bd = (hid[:, None] == hid[None, :]).astype(BF16)

    q, k, v, frw = _in0(xc, mod0, win0, row(l0_mla_q_norm), row(l0_mla_kv_norm), wqa, wqb, wuk, wuv, cs, sn)
    att = _attention(q, k, v)
    sf, g, bonus = _rwfeat(frw, row(l0_rwkv_mu), row(l0_rwkv_w0), _block_diag2(l0_rwkv_w2).astype(BF16),
                           row(l0_rwkv_a0), _block_diag2(l0_rwkv_a2).astype(BF16), l0_rwkv_g2.astype(BF16),
                           row(l0_rwkv_k_k), row(l0_rwkv_k_a), row(l0_rwkv_r_k), ones_bd)
    ys = _rwkv_scan_both(sf)
    wo = l0_w_out.astype(BF16)
    xc = _out0(att, ys, bonus, g, xc, mod0, row(l0_rwkv_gn_w), row(l0_rwkv_gn_b), ones_bd,
               wo[:HEADS * MLA_V], wo[HEADS * MLA_V:], row(l0_ln1_g), row(l0_ln1_b))
    xc = _mlp(xc, mod0, l0_mlp_w1.astype(BF16), l0_mlp_w2.astype(BF16), row(l0_ln2_g), row(l0_ln2_b),
              combined=True, row_off=0)

    gate, xr = _in1(xc, mod1, l1_w_in.astype(BF16))
    wbd = jnp.concatenate([l1_lru_ga_w[0], l1_lru_gx_w[0], l1_lru_ga_w[1], l1_lru_gx_w[1]], axis=-1).astype(BF16)
    gb = jnp.stack([l1_lru_ga_b[0], l1_lru_gx_b[0], l1_lru_ga_b[1], l1_lru_gx_b[1]])
    a0, u0, a1, u1 = _lrufeat(xr, l1_conv_w, row(l1_conv_b), wbd, gb, l1_lru_lambda)
    hf, hr = _lruscan(a0, u0, a1, u1)
    xl = _out1(gate, hf, hr, xc, mod1, l1_w_out.astype(BF16), row(l1_ln1_g), row(l1_ln1_b))
    return _mlp(xl, mod1, l1_mlp_w1.astype(BF16), l1_mlp_w2.astype(BF16), row(l1_ln2_g), row(l1_ln2_b),
                combined=False, row_off=0)
```

```python
import functools
import math

import jax
import jax.numpy as jnp
from jax import lax
from jax.experimental import pallas as pl
from jax.experimental.pallas import tpu as pltpu

F32 = jnp.float32
BF16 = jnp.bfloat16

D = 1024
DEPTH = 2
N_MOD = 6
ALPHA = (2.0 * DEPTH) ** 0.25
LN_EPS = 1e-5
RMS_EPS = 1e-6

HEADS = 8
MLA_NOPE = 64
MLA_ROPE = 32
MLA_V = 64
MLA_Q_RANK = 384
MLA_KV_RANK = 256
ROPE_AXIS = MLA_ROPE // 2
ROPE_THETA = 10000.0
GRID_W = 64
ATT_SCALE = (MLA_NOPE + MLA_ROPE) ** -0.5
HEAD_PAD = 128

RW_HEAD = 64
RW_W = HEADS * RW_HEAD
LORA_W = 64
LORA_A = 64
LORA_G = 128
RW_IN = 3 * RW_W + 2 * LORA_W + 2 * LORA_A + LORA_G
GN_EPS = 64e-5
EXP_NEG_HALF = math.exp(-0.5)

LRU_W = D
LRU_BLOCKS = 8
LRU_BLOCK = LRU_W // LRU_BLOCKS
LRU_C = 8.0
D_FF = 4 * D

TM = 256
SUB = 8
SCAN_TC = 72
SCAN_TC_SMALL = 32
RELAY_T = 128
PITCH_F = RW_HEAD + SUB
PITCH_H = LRU_BLOCKS * SUB + SUB
LRU_TL = 32
VMEM_LIMIT = 56 * 1024 * 1024

IN0_COLS = MLA_Q_RANK + MLA_KV_RANK + 2 * HEAD_PAD + RW_IN


def _params(sem):
    return pltpu.CompilerParams(dimension_semantics=sem, vmem_limit_bytes=VMEM_LIMIT)


def _bdot(a, w):
    return jnp.dot(a.astype(BF16), w, preferred_element_type=F32)


def _sigmoid(x):
    return 1.0 / (1.0 + jnp.exp(-x))


def _layer_norm(z, g, b):
    mu = jnp.mean(z, axis=-1, keepdims=True)
    zc = z - mu
    var = jnp.mean(zc * zc, axis=-1, keepdims=True)
    return zc * lax.rsqrt(var + LN_EPS) * g + b


def _head_sum(x, ones_bd):
    hi = x.astype(BF16)
    lo = (x - hi.astype(F32)).astype(BF16)
    return (jnp.dot(hi, ones_bd, preferred_element_type=F32)
            + jnp.dot(lo, ones_bd, preferred_element_type=F32))


def _const_spec(shape):
    nd = len(shape)
    return pl.BlockSpec(shape, lambda *_: (0,) * nd)


def _mod_kernel(c_ref, w_ref, b_ref, o_ref):
    c = c_ref[...]
    s = c * _sigmoid(c)
    o_ref[...] = jnp.dot(s, w_ref[...], precision=lax.Precision.HIGHEST,
                         preferred_element_type=F32) + b_ref[...]


def _mod_table(c, c_ctx, mod_w, mod_b):
    b = c.shape[0]
    rows = 16
    cc = jnp.zeros((rows, D), F32).at[:b].set(c).at[b].set(c_ctx)
    tn = 1024
    out = pl.pallas_call(
        _mod_kernel,
        grid=(N_MOD * D // tn,),
        in_specs=[pl.BlockSpec((rows, D), lambda j: (0, 0)),
                  pl.BlockSpec((D, tn), lambda j: (0, j)),
                  pl.BlockSpec((1, tn), lambda j: (0, j))],
        out_specs=pl.BlockSpec((rows, tn), lambda j: (0, j)),
        out_shape=jax.ShapeDtypeStruct((rows, N_MOD * D), F32),
        compiler_params=_params(("arbitrary",)),
        name="mod_proj",
    )(cc, mod_w, mod_b.reshape(1, -1))
    lat = out[:b].reshape(b, N_MOD, D)
    ctx = jnp.broadcast_to(out[b].reshape(1, N_MOD, D), (b, N_MOD, D))
    tab = jnp.stack([ctx, lat], axis=1)
    return jnp.pad(tab, ((0, 0), (0, 0), (0, SUB - N_MOD), (0, 0)))


def _mod_spec(combined):
    if combined:
        return pl.BlockSpec((1, 1, SUB, D), lambda b, i: (b, jnp.minimum(i, 1), 0, 0))
    return pl.BlockSpec((1, 1, SUB, D), lambda b, i: (b, 1, 0, 0))


def _in0_kernel(x_ref, mod_ref, win_ref, qn_ref, kvn_ref, wqa_ref, wqb_ref, wuk_ref, wuv_ref,
                cs_ref, sn_ref, q_ref, k_ref, v_ref, frw_ref):
    x = x_ref[0]
    m = mod_ref[0, 0]
    h = x * (1.0 + m[1:2]) + m[0:1]
    f = _bdot(h, win_ref[...])
    o_kv = MLA_Q_RANK
    o_kr = o_kv + MLA_KV_RANK
    o_rw = o_kr + 2 * HEAD_PAD
    fq = f[:, :o_kv]
    fkv = f[:, o_kv:o_kr]
    fkr = f[:, o_kr:o_kr + HEAD_PAD]
    fkr_rot = f[:, o_kr + HEAD_PAD:o_rw]
    frw_ref[0] = f[:, o_rw:]
    qn = fq * lax.rsqrt(jnp.mean(fq * fq, axis=-1, keepdims=True) + RMS_EPS) * qn_ref[...]
    ckv = fkv * lax.rsqrt(jnp.mean(fkv * fkv, axis=-1, keepdims=True) + RMS_EPS) * kvn_ref[...]
    qn = qn.astype(BF16)
    ckv = ckv.astype(BF16)
    qa = jnp.dot(qn, wqa_ref[...], preferred_element_type=F32)
    qb = jnp.dot(qn, wqb_ref[...], preferred_element_type=F32)
    kn = jnp.dot(ckv, wuk_ref[...], preferred_element_type=F32)
    cs = cs_ref[...]
    sn = sn_ref[...]
    kr = fkr * cs + fkr_rot * sn
    for hd in range(HEADS):
        sl = slice(hd * HEAD_PAD, (hd + 1) * HEAD_PAD)
        q_ref[0, :, sl] = ((qa[:, sl] * cs + qb[:, sl] * sn) * ATT_SCALE).astype(BF16)
        k_ref[0, :, sl] = (kn[:, sl] + kr).astype(BF16)
    v_ref[0] = jnp.dot(ckv, wuv_ref[...], preferred_element_type=F32).astype(BF16)


def _in0(xc, modt, win, qn, kvn, wqa, wqb, wuk, wuv, cs, sn):
    b, t, _ = xc.shape
    row = lambda w: pl.BlockSpec((1, TM, w), lambda bb, i: (bb, i, 0))
    return pl.pallas_call(
        _in0_kernel,
        grid=(b, t // TM),
        in_specs=[row(D), _mod_spec(True), _const_spec(win.shape), _const_spec(qn.shape),
                  _const_spec(kvn.shape), _const_spec(wqa.shape), _const_spec(wqb.shape),
                  _const_spec(wuk.shape), _const_spec(wuv.shape),
                  pl.BlockSpec((TM, HEAD_PAD), lambda bb, i: (i, 0)),
                  pl.BlockSpec((TM, HEAD_PAD), lambda bb, i: (i, 0))],
        out_specs=[row(HEADS * HEAD_PAD), row(HEADS * HEAD_PAD), row(HEADS * MLA_V), row(RW_IN)],
        out_shape=[jax.ShapeDtypeStruct((b, t, HEADS * HEAD_PAD), BF16),
                   jax.ShapeDtypeStruct((b, t, HEADS * HEAD_PAD), BF16),
                   jax.ShapeDtypeStruct((b, t, HEADS * MLA_V), BF16),
                   jax.ShapeDtypeStruct((b, t, RW_IN), F32)],
        compiler_params=_params(("parallel", "parallel")),
        name="l0_in_proj",
    )(xc, modt, win, qn, kvn, wqa, wqb, wuk, wuv, cs, sn)


def _att_kernel(q_ref, k_ref, v_ref, o_ref, *, n_ctx, n_all):
    i = pl.program_id(2)

    def attend(nk):
        v = v_ref[0, :nk, :]
        lane = lax.broadcasted_iota(jnp.int32, v.shape, 1)
        acc = None
        for hh in range(2):
            sl = slice(hh * HEAD_PAD, (hh + 1) * HEAD_PAD)
            q = q_ref[0, :, sl]
            k = k_ref[0, :nk, sl]
            s = lax.dot_general(q, k, (((1,), (1,)), ((), ())), preferred_element_type=F32)
            p = jnp.exp(s - jnp.max(s, axis=-1, keepdims=True))
            l = jnp.sum(p, axis=-1, keepdims=True)
            keep = (lane < MLA_V) if hh == 0 else (lane >= MLA_V)
            vh = jnp.where(keep, v, jnp.zeros_like(v))
            o = jnp.dot(p.astype(BF16), vh, preferred_element_type=F32) / l
            acc = o if acc is None else acc + o
        o_ref[0] = acc.astype(o_ref.dtype)

    @pl.when(i == 0)
    def _():
        attend(n_ctx)

    @pl.when(i > 0)
    def _():
        attend(n_all)


def _attention(q, k, v):
    b, t, _ = q.shape
    return pl.pallas_call(
        functools.partial(_att_kernel, n_ctx=TM, n_all=t),
        grid=(b, HEADS // 2, t // TM),
        in_specs=[pl.BlockSpec((1, TM, 2 * HEAD_PAD), lambda bb, hp, i: (bb, i, hp)),
                  pl.BlockSpec((1, t, 2 * HEAD_PAD), lambda bb, hp, i: (bb, 0, hp)),
                  pl.BlockSpec((1, t, 2 * MLA_V), lambda bb, hp, i: (bb, 0, hp))],
        out_specs=pl.BlockSpec((1, TM, 2 * MLA_V), lambda bb, hp, i: (bb, i, hp)),
        out_shape=jax.ShapeDtypeStruct((b, t, HEADS * MLA_V), BF16),
        compiler_params=_params(("parallel", "parallel", "parallel")),
        name="l0_attention",
    )(q, k, v)


def _halo_flags(i, n_tiles):
    return i >= 2, jnp.logical_and(i >= 1, i < n_tiles - 1)


def _rwfeat_kernel(f_ref, fp_ref, fn_ref, mu_ref, w0_ref, w2_ref, a0_ref, a2_ref, g2_ref,
                   kk_ref, ka_ref, rk_ref, ones_ref, sf_ref, g_ref, bonus_ref):
    i = pl.program_id(1)
    has_prev, has_next = _halo_flags(i, pl.num_programs(1))
    f = f_ref[0]
    prow = jnp.where(has_prev, fp_ref[0, SUB - 1:SUB, :], 0.0)
    nrow = jnp.where(has_next, fn_ref[0, 0:1, :], 0.0)
    rid = lax.broadcasted_iota(jnp.int32, (TM, 1), 0)
    prev = jnp.where(rid == 0, prow, pltpu.roll(f, 1, axis=0))
    nxt = jnp.where(rid == TM - 1, nrow, pltpu.roll(f, TM - 1, axis=0))
    f = f + mu_ref[...] * (0.5 * (prev + nxt) - f)
    r = f[:, 0:RW_W]
    k = f[:, RW_W:2 * RW_W]
    v = f[:, 2 * RW_W:3 * RW_W]
    o = 3 * RW_W
    wl = f[:, o:o + 2 * LORA_W]
    al = f[:, o + 2 * LORA_W:o + 2 * LORA_W + 2 * LORA_A]
    gl = f[:, o + 2 * LORA_W + 2 * LORA_A:]
    w_raw = w0_ref[...] + _bdot(jnp.tanh(wl), w2_ref[...])
    decay = jnp.exp(-EXP_NEG_HALF * _sigmoid(w_raw))
    a = _sigmoid(a0_ref[...] + _bdot(al, a2_ref[...]))
    g_ref[0] = _bdot(_sigmoid(gl), g2_ref[...])
    ones_bd = ones_ref[...]
    kk = k * kk_ref[...]
    kk = kk * lax.rsqrt(_head_sum(kk * kk, ones_bd) + 1e-12)
    ka = ka_ref[...]
    kd = [k * (1.0 + (a[:, d * RW_W:(d + 1) * RW_W] - 1.0) * ka) for d in range(2)]
    bonus_ref[0] = _head_sum(r * (kd[0] + kd[1]) * rk_ref[...], ones_bd) * v
    parts = [r, kk, v, decay[:, :RW_W], kd[0], kk * a[:, :RW_W],
             decay[:, RW_W:], kd[1], kk * a[:, RW_W:]]
    for n, p in enumerate(parts):
        sf_ref[0, :, n * RW_W:(n + 1) * RW_W] = p


def _halo_specs(width, t):
    nb = TM // SUB
    last = t // SUB - 1
    prev = pl.BlockSpec((1, SUB, width), lambda bb, i: (bb, jnp.maximum(i * nb - 1, 0), 0))
    nxt = pl.BlockSpec((1, SUB, width), lambda bb, i: (bb, jnp.minimum((i + 1) * nb, last), 0))
    return prev, nxt


def _rwfeat(frw, mu, w0, w2bd, a0, a2bd, g2, k_k, k_a, r_k, ones_bd):
    b, t, _ = frw.shape
    row = lambda w: pl.BlockSpec((1, TM, w), lambda bb, i: (bb, i, 0))
    prev, nxt = _halo_specs(RW_IN, t)
    consts = (mu, w0, w2bd, a0, a2bd, g2, k_k, k_a, r_k, ones_bd)
    return pl.pallas_call(
        _rwfeat_kernel,
        grid=(b, t // TM),
        in_specs=[row(RW_IN), prev, nxt] + [_const_spec(c.shape) for c in consts],
        out_specs=[row(9 * RW_W), row(RW_W), row(RW_W)],
        out_shape=[jax.ShapeDtypeStruct((b, t, 9 * RW_W), F32),
                   jax.ShapeDtypeStruct((b, t, RW_W), F32),
                   jax.ShapeDtypeStruct((b, t, RW_W), F32)],
        compiler_params=_params(("parallel", "parallel")),
        name="l0_rwkv_feat",
    )(frw, frw, frw, *consts)


def _time_mirror(i, n_ctx_tiles, n_tiles):
    return jnp.where(i < n_ctx_tiles, n_ctx_tiles - 1 - i, n_tiles + n_ctx_tiles - 1 - i)


def _reverse_lanes(x, j3):
    hi = x.astype(BF16)
    r1 = x - hi.astype(F32)
    mid = r1.astype(BF16)
    lo = (r1 - mid.astype(F32)).astype(BF16)
    return jnp.dot(jnp.concatenate([hi, mid, lo], axis=1), j3, preferred_element_type=F32)


def _relayout_in_kernel(xa_ref, xb_ref, j3_ref, o_ref, q_ref, *, step_major):
    nb = xa_ref.shape[0]
    tr = xa_ref.shape[1]
    chains = 2 * nb * HEADS
    for a2, x_ref in enumerate((xa_ref, xb_ref)):
        for b in range(nb):
            for hp in range(HEADS // 2):
                xt = x_ref[b, :, hp * 128:(hp + 1) * 128].T
                if a2 == 1:
                    xt = _reverse_lanes(xt, j3_ref[...])
                for h2 in range(2):
                    row0 = ((a2 * nb + b) * HEADS + 2 * hp + h2) * PITCH_F
                    q_ref[row0:row0 + RW_HEAD, :] = xt[h2 * RW_HEAD:(h2 + 1) * RW_HEAD]
    for f in range(RW_HEAD):
        mt = q_ref[pl.ds(f, chains, stride=PITCH_F), :].T
        if step_major:
            o_ref[pl.ds(f, tr, stride=PITCH_F), :] = mt
        else:
            o_ref[0, f] = mt
    if step_major:
        for f in range(RW_HEAD, PITCH_F):
            o_ref[pl.ds(f, tr, stride=PITCH_F), :] = jnp.zeros((tr, chains), F32)


def _relayout_in(sf, j3):
    b, t, _ = sf.shape
    chains = 2 * b * HEADS
    nt = t // RELAY_T
    nc = TM // RELAY_T
    mirror = lambda i: _time_mirror(i, nc, nt)
    blk = (b, RELAY_T, RW_W)
    scratch = [pltpu.VMEM((chains * PITCH_F, RELAY_T), F32)]
    dirdep = lambda p: jnp.minimum(p // 2, 1)
    x = pl.pallas_call(
        functools.partial(_relayout_in_kernel, step_major=False),
        grid=(nt, 5),
        in_specs=[pl.BlockSpec(blk, lambda i, p: (0, i, p + dirdep(p))),
                  pl.BlockSpec(blk, lambda i, p: (0, mirror(i), p + 4 * dirdep(p))),
                  _const_spec(j3.shape)],
        out_specs=pl.BlockSpec((1, RW_HEAD, RELAY_T, chains), lambda i, p: (p, 0, i, 0)),
        out_shape=jax.ShapeDtypeStruct((5, RW_HEAD, t, chains), F32),
        scratch_shapes=scratch,
        compiler_params=_params(("parallel", "parallel")),
        name="l0_rwkv_relayout_in",
    )(sf, sf, j3)
    v = pl.pallas_call(
        functools.partial(_relayout_in_kernel, step_major=True),
        grid=(nt,),
        in_specs=[pl.BlockSpec(blk, lambda i: (0, i, 2)),
                  pl.BlockSpec(blk, lambda i: (0, mirror(i), 2)),
                  _const_spec(j3.shape)],
        out_specs=pl.BlockSpec((RELAY_T * PITCH_F, chains), lambda i: (i, 0)),
        out_shape=jax.ShapeDtypeStruct((t * PITCH_F, chains), F32),
        scratch_shapes=scratch,
        compiler_params=_params(("parallel",)),
        name="l0_rwkv_relayout_v",
    )(sf, sf, j3)
    return x, v


def _rwscan_kernel(x_ref, v_ref, y_ref, s_ref):
    tc = x_ref.shape[2]
    chains = x_ref.shape[3]

    @pl.when(pl.program_id(0) == 0)
    def _():
        s_ref[...] = jnp.zeros_like(s_ref)

    groups = RW_HEAD // SUB
    zeros = tuple(jnp.zeros((SUB, chains), F32) for _ in range(groups))

    def row(a, j, s):
        return jnp.broadcast_to(x_ref[a, j, pl.ds(s, 1), :], (SUB, chains))

    def s_at(j, g):
        return s_ref.at[j, g * SUB:(g + 1) * SUB, :]

    sa0 = list(zeros)
    for j in range(RW_HEAD):
        kk = row(1, j, 0)
        for g in range(groups):
            sa0[g] = sa0[g] + s_at(j, g)[...] * kk

    def step(s, sa):
        base = pl.multiple_of(s * PITCH_F, SUB)
        v = v_ref[pl.ds(base, RW_HEAD), :]
        s_next = jnp.minimum(s + 1, tc - 1)
        y = list(zeros)
        sa_next = list(zeros)
        for j in range(RW_HEAD):
            r, kk_next, w, k, kka = row(0, j, s), row(1, j, s_next), row(2, j, s), row(3, j, s), row(4, j, s)
            for g in range(groups):
                sj = s_at(j, g)[...] * w - sa[g] * kka + v[g * SUB:(g + 1) * SUB] * k
                s_at(j, g)[...] = sj
                y[g] = y[g] + sj * r
                sa_next[g] = sa_next[g] + sj * kk_next
        y_ref[pl.ds(base, RW_HEAD), :] = jnp.concatenate(y, axis=0)
        y_ref[pl.ds(base + RW_HEAD, PITCH_F - RW_HEAD), :] = zeros[0]
        return tuple(sa_next)

    lax.fori_loop(0, tc, step, tuple(sa0))


def _rwscan(x, v):
    _, _, t, chains = x.shape
    tc = SCAN_TC if t % SCAN_TC == 0 else SCAN_TC_SMALL
    return pl.pallas_call(
        _rwscan_kernel,
        grid=(t // tc,),
        in_specs=[pl.BlockSpec((5, RW_HEAD, tc, chains), lambda i: (0, 0, i, 0)),
                  pl.BlockSpec((tc * PITCH_F, chains), lambda i: (i, 0))],
        out_specs=pl.BlockSpec((tc * PITCH_F, chains), lambda i: (i, 0)),
        out_shape=jax.ShapeDtypeStruct((t * PITCH_F, chains), F32),
        scratch_shapes=[pltpu.VMEM((RW_HEAD, RW_HEAD, chains), F32)],
        compiler_params=_params(("arbitrary",)),
        name="l0_rwkv_scan",
    )(x, v)


def _relayout_out_kernel(yf_ref, yr_ref, j3_ref, o_ref, q_ref):
    nb = o_ref.shape[0]
    tr = o_ref.shape[1]
    chains = yf_ref.shape[1]
    half = chains // 2
    pitch_c = chains + SUB
    fwd = lax.broadcasted_iota(jnp.int32, (tr, chains), 1) < half
    for i in range(RW_HEAD):
        rows = pl.ds(i, tr, stride=PITCH_F)
        mt = jnp.where(fwd, yf_ref[rows, :], yr_ref[rows, :]).T
        q_ref[i * pitch_c:i * pitch_c + half, :] = mt[:half]
        q_ref[i * pitch_c + half:i * pitch_c + chains, :] = _reverse_lanes(mt[half:], j3_ref[...])
    for b in range(nb):
        for hp in range(HEADS // 2):
            parts = []
            for h2 in range(2):
                c = b * HEADS + 2 * hp + h2
                parts.append(q_ref[pl.ds(c, RW_HEAD, stride=pitch_c), :]
                             + q_ref[pl.ds(half + c, RW_HEAD, stride=pitch_c), :])
            o_ref[b, :, hp * 128:(hp + 1) * 128] = jnp.concatenate(parts, axis=0).T


def _relayout_out(y, j3, b):
    chains = y.shape[1]
    t = y.shape[0] // PITCH_F
    nt = t // RELAY_T
    nc = TM // RELAY_T
    return pl.pallas_call(
        _relayout_out_kernel,
        grid=(nt,),
        in_specs=[pl.BlockSpec((RELAY_T * PITCH_F, chains), lambda i: (i, 0)),
                  pl.BlockSpec((RELAY_T * PITCH_F, chains), lambda i: (_time_mirror(i, nc, nt), 0)),
                  _const_spec(j3.shape)],
        out_specs=pl.BlockSpec((b, RELAY_T, RW_W), lambda i: (0, i, 0)),
        out_shape=jax.ShapeDtypeStruct((b, t, RW_W), F32),
        scratch_shapes=[pltpu.VMEM((RW_HEAD * (chains + SUB), RELAY_T), F32)],
        compiler_params=_params(("parallel",)),
        name="l0_rwkv_relayout_out",
    )(y, y, j3)


def _rwkv_scan_both(sf):
    k = jnp.arange(RELAY_T)
    anti = (k[:, None] + k[None, :] == RELAY_T - 1).astype(BF16)
    j3 = jnp.concatenate([anti, anti, anti], axis=0)
    y = _rwscan(*_relayout_in(sf, j3))
    return _relayout_out(y, j3, sf.shape[0])


def _out0_kernel(att_ref, ys_ref, bonus_ref, g_ref, x_ref, mod_ref, gnw_ref, gnb_ref, ones_ref,
                 woa_ref, wob_ref, lng_ref, lnb_ref, o_ref):
    ones_bd = ones_ref[...]
    y = ys_ref[0]
    mu = _head_sum(y, ones_bd) * (1.0 / RW_HEAD)
    yc = y - mu
    var = _head_sum(yc * yc, ones_bd) * (1.0 / RW_HEAD)
    yn = yc * lax.rsqrt(var + GN_EPS) * gnw_ref[...] + gnb_ref[...]
    rw = (yn + bonus_ref[0]) * g_ref[0]
    o = (jnp.dot(att_ref[0], woa_ref[...], preferred_element_type=F32)
         + _bdot(rw, wob_ref[...]))
    m = mod_ref[0, 0]
    o_ref[0] = _layer_norm(ALPHA * x_ref[0] + m[2:3] * o, lng_ref[...], lnb_ref[...])


def _out0(att, ys, bonus, g, xc, modt, gnw, gnb, ones_bd, woa, wob, lng, lnb):
    b, t, _ = xc.shape
    row = lambda w: pl.BlockSpec((1, TM, w), lambda bb, i: (bb, i, 0))
    consts = (gnw, gnb, ones_bd, woa, wob, lng, lnb)
    return pl.pallas_call(
        _out0_kernel,
        grid=(b, t // TM),
        in_specs=[row(RW_W), row(RW_W), row(RW_W), row(RW_W), row(D), _mod_spec(True)]
                 + [_const_spec(c.shape) for c in consts],
        out_specs=row(D),
        out_shape=jax.ShapeDtypeStruct((b, t, D), F32),
        compiler_params=_params(("parallel", "parallel")),
        name="l0_out_proj",
    )(att, ys, bonus, g, xc, modt, *consts)


def _mlp_kernel(x_ref, mod_ref, w1_ref, w2_ref, lng_ref, lnb_ref, o_ref):
    x = x_ref[0]
    m = mod_ref[0, 0]
    h = (x * (1.0 + m[4:5]) + m[3:4]).astype(BF16)
    acc = None
    fc = 1024
    for c in range(D_FF // fc):
        u = jnp.dot(h, w1_ref[:, c * fc:(c + 1) * fc], preferred_element_type=F32)
        u = jnp.square(jnp.maximum(u, 0.0)).astype(BF16)
        part = jnp.dot(u, w2_ref[c * fc:(c + 1) * fc, :], preferred_element_type=F32)
        acc = part if acc is None else acc + part
    o_ref[0] = _layer_norm(ALPHA * x + m[5:6] * acc, lng_ref[...], lnb_ref[...])


def _mlp(x, modt, w1, w2, lng, lnb, *, combined, row_off):
    b, tx, _ = x.shape
    nt = tx // TM - row_off
    single = pl.Buffered(1)
    wspec = lambda shape: pl.BlockSpec(shape, lambda *_: (0,) * len(shape), pipeline_mode=single)
    return pl.pallas_call(
        _mlp_kernel,
        grid=(b, nt),
        in_specs=[pl.BlockSpec((1, TM, D), lambda bb, i: (bb, i + row_off, 0)), _mod_spec(combined),
                  wspec(w1.shape), wspec(w2.shape), _const_spec(lng.shape), _const_spec(lnb.shape)],
        out_specs=pl.BlockSpec((1, TM, D), lambda bb, i: (bb, i, 0)),
        out_shape=jax.ShapeDtypeStruct((b, nt * TM, D), F32),
        compiler_params=_params(("parallel", "parallel")),
        name="mlp",
    )(x, modt, w1, w2, lng, lnb)


def _in1_kernel(x_ref, mod_ref, w_ref, gate_ref, xr_ref):
    m = mod_ref[0, 0]
    h = x_ref[0] * (1.0 + m[1:2]) + m[0:1]
    f = _bdot(h, w_ref[...])
    gate_ref[0] = f[:, :LRU_W]
    xr_ref[0] = f[:, LRU_W:]


def _in1(xc, modt, w):
    b, t, _ = xc.shape
    row = lambda wd: pl.BlockSpec((1, TM, wd), lambda bb, i: (bb, i, 0))
    return pl.pallas_call(
        _in1_kernel,
        grid=(b, t // TM),
        in_specs=[row(D), _mod_spec(True), _const_spec(w.shape)],
        out_specs=[row(LRU_W), row(LRU_W)],
        out_shape=[jax.ShapeDtypeStruct((b, t, LRU_W), F32)] * 2,
        compiler_params=_params(("parallel", "parallel")),
        name="l1_in_proj",
    )(xc, modt, w)


def _lru_kernel(xf_ref, xfp_ref, xfn_ref, xr_ref, xrp_ref, xrn_ref, cw_ref, cb_ref, wbd_ref, gb_ref, lam_ref,
                hf_ref, hr_ref, a_scr, u_scr, cf_ref, cr_ref):
    i = pl.program_id(0)
    n = pl.num_programs(0)
    nb, tl, _ = xf_ref.shape
    rows = nb * tl
    nc = TM // tl
    pitch_b = tl + SUB

    @pl.when(i == 0)
    def _():
        cf_ref[...] = jnp.zeros_like(cf_ref)
        cr_ref[...] = jnp.zeros_like(cr_ref)

    lam = lam_ref[...]
    nl = -lam
    softplus = jnp.maximum(nl, 0.0) + jnp.log1p(jnp.exp(-jnp.abs(nl)))
    gb = gb_ref[...]
    cw = cw_ref[...]
    tpos = jnp.bitwise_and(lax.broadcasted_iota(jnp.int32, (rows, 1), 0), tl - 1)
    tile_r = _time_mirror(i, nc, n)
    dirs = ((xf_ref, xfp_ref, xfn_ref, i, hf_ref, cf_ref), (xr_ref, xrp_ref, xrn_ref, tile_r, hr_ref, cr_ref))
    for d, (x_ref, xp_ref, xn_ref, tile, h_ref, c_ref) in enumerate(dirs):
        seg_first = jnp.logical_or(tile == 0, tile == nc)
        seg_last = jnp.logical_or(tile == nc - 1, tile == n - 1)

        def halo(ref, r, edge):
            per_b = [jnp.broadcast_to(ref[b, r:r + 1, :], (tl, LRU_W)) for b in range(nb)]
            return jnp.where(edge, 0.0, jnp.concatenate(per_b, axis=0))

        x = x_ref[...].reshape(rows, LRU_W)
        p1 = halo(xp_ref, SUB - 1, seg_first)
        p2 = halo(xp_ref, SUB - 2, seg_first)
        n1 = halo(xn_ref, 0, seg_last)
        xm1 = jnp.where(tpos == 0, p1, pltpu.roll(x, 1, axis=0))
        xm2 = jnp.where(tpos == 0, p2, jnp.where(tpos == 1, p1, pltpu.roll(x, 2, axis=0)))
        xp1 = jnp.where(tpos == tl - 1, n1, pltpu.roll(x, rows - 1, axis=0))
        xc = cw[0:1] * xm2 + cw[1:2] * xm1 + cw[2:3] * x + cw[3:4] * xp1 + cb_ref[...]
        for blk in range(LRU_BLOCKS):
            sl = slice(blk * LRU_BLOCK, (blk + 1) * LRU_BLOCK)
            xb = xc[:, sl]
            z = _bdot(xb, wbd_ref[blk, :, 2 * d * LRU_BLOCK:(2 * d + 2) * LRU_BLOCK])
            zr = z[:, :LRU_BLOCK] + gb[2 * d:2 * d + 1, sl]
            zi = z[:, LRU_BLOCK:] + gb[2 * d + 1:2 * d + 2, sl]
            log_a = -LRU_C * _sigmoid(zr) * softplus[d:d + 1, sl]
            a = jnp.exp(log_a)
            u = jnp.sqrt(-jnp.tanh(log_a) * (a * a + 1.0)) * (_sigmoid(zi) * xb)
            for b in range(nb):
                a_scr[blk, b * pitch_b:b * pitch_b + tl, :] = a[b * tl:(b + 1) * tl]
                u_scr[blk, b * pitch_b:b * pitch_b + tl, :] = u[b * tl:(b + 1) * tl]
        h = [c_ref[:, blk * LRU_BLOCK:(blk + 1) * LRU_BLOCK] for blk in range(LRU_BLOCKS)]
        for step in range(tl):
            t = step if d == 0 else tl - 1 - step
            for blk in range(LRU_BLOCKS):
                at = a_scr[blk, pl.ds(t, nb, stride=pitch_b), :]
                ut = u_scr[blk, pl.ds(t, nb, stride=pitch_b), :]
                h[blk] = at * h[blk] + ut
                h_ref[t * PITCH_H + blk * SUB:t * PITCH_H + blk * SUB + nb, :] = h[blk]
            h_ref[t * PITCH_H + LRU_BLOCKS * SUB:(t + 1) * PITCH_H, :] = jnp.zeros((SUB, LRU_BLOCK), F32)
        for blk in range(LRU_BLOCKS):
            c_ref[:, blk * LRU_BLOCK:(blk + 1) * LRU_BLOCK] = h[blk]


def _lru(xr, cw, cb, wbd, gb, lam):
    b, t, _ = xr.shape
    assert b <= SUB
    n = t // LRU_TL
    nc = TM // LRU_TL
    hb = LRU_TL // SUB
    last = t // SUB - 1
    mirror = lambda i: _time_mirror(i, nc, n)

    def specs(tile):
        return [pl.BlockSpec((b, LRU_TL, LRU_W), lambda i: (0, tile(i), 0)),
                pl.BlockSpec((b, SUB, LRU_W), lambda i: (0, jnp.maximum(tile(i) * hb - 1, 0), 0)),
                pl.BlockSpec((b, SUB, LRU_W), lambda i: (0, jnp.minimum((tile(i) + 1) * hb, last), 0))]

    consts = (cw, cb, wbd, gb, lam)
    hspec = lambda tile: pl.BlockSpec((LRU_TL * PITCH_H, LRU_BLOCK), lambda i: (tile(i), 0))
    return pl.pallas_call(
        _lru_kernel,
        grid=(n,),
        in_specs=specs(lambda i: i) + specs(mirror) + [_const_spec(c.shape) for c in consts],
        out_specs=[hspec(lambda i: i), hspec(mirror)],
        out_shape=[jax.ShapeDtypeStruct((t * PITCH_H, LRU_BLOCK), F32)] * 2,
        scratch_shapes=[pltpu.VMEM((LRU_BLOCKS, b * (LRU_TL + SUB), LRU_BLOCK), F32),
                        pltpu.VMEM((LRU_BLOCKS, b * (LRU_TL + SUB), LRU_BLOCK), F32),
                        pltpu.VMEM((b, LRU_W), F32), pltpu.VMEM((b, LRU_W), F32)],
        compiler_params=_params(("arbitrary",)),
        name="l1_lru",
    )(xr, xr, xr, xr, xr, xr, *consts)


def _out1_kernel(gate_ref, hf_ref, hr_ref, x_ref, mod_ref, w_ref, lng_ref, lnb_ref, o_ref):
    nb, tl, _ = gate_ref.shape
    per_b = []
    for b in range(nb):
        cols = []
        for blk in range(LRU_BLOCKS):
            rows = pl.ds(blk * SUB + b, tl, stride=PITCH_H)
            cols.append(hf_ref[rows, :] + hr_ref[rows, :])
        per_b.append(jnp.concatenate(cols, axis=1))
    h = jnp.concatenate(per_b, axis=0)
    gate = gate_ref[...].reshape(nb * tl, LRU_W)
    gelu = 0.5 * gate * (1.0 + jnp.tanh(math.sqrt(2.0 / math.pi) * (gate + 0.044715 * gate * gate * gate)))
    o = _bdot(gelu * h, w_ref[...])
    for b in range(nb):
        m = mod_ref[b, 0]
        z = ALPHA * x_ref[b] + m[2:3] * o[b * tl:(b + 1) * tl]
        o_ref[b] = _layer_norm(z, lng_ref[...], lnb_ref[...])


def _out1(gate, hf, hr, xc, modt, w, lng, lnb):
    b, t, _ = xc.shape
    off = TM // LRU_TL
    nt = t // LRU_TL - off
    lat = lambda wd: pl.BlockSpec((b, LRU_TL, wd), lambda i: (0, i + off, 0))
    hspec = pl.BlockSpec((LRU_TL * PITCH_H, LRU_BLOCK), lambda i: (i + off, 0))
    consts = (w, lng, lnb)
    return pl.pallas_call(
        _out1_kernel,
        grid=(nt,),
        in_specs=[lat(LRU_W), hspec, hspec, lat(D),
                  pl.BlockSpec((b, 1, SUB, D), lambda i: (0, 1, 0, 0))]
                 + [_const_spec(c.shape) for c in consts],
        out_specs=pl.BlockSpec((b, LRU_TL, D), lambda i: (0, i, 0)),
        out_shape=jax.ShapeDtypeStruct((b, nt * LRU_TL, D), F32),
        compiler_params=_params(("parallel",)),
        name="l1_out_proj",
    )(gate, hf, hr, xc, modt, *consts)


def _rot_cols(w):
    ws = w.reshape(w.shape[:-1] + (2, 2, ROPE_AXIS // 2))
    return jnp.stack([-ws[..., 1, :], ws[..., 0, :]], axis=-2).reshape(w.shape)


def _rope_tables(n, n_ctx):
    rows_n = n // GRID_W
    rows = jnp.repeat(jnp.arange(rows_n, dtype=F32), GRID_W)
    cols = jnp.tile(jnp.arange(GRID_W, dtype=F32), rows_n)
    inv_freq = ROPE_THETA ** (-jnp.arange(0, ROPE_AXIS, 2, dtype=F32) / ROPE_AXIS)
    ang_r = rows[:, None] * inv_freq
    ang_c = cols[:, None] * inv_freq
    ang = jnp.concatenate([ang_r, ang_r, ang_c, ang_c], axis=-1)
    cos = jnp.concatenate([jnp.ones((n_ctx, MLA_ROPE), F32), jnp.cos(ang)], axis=0)
    sin = jnp.concatenate([jnp.zeros((n_ctx, MLA_ROPE), F32), jnp.sin(ang)], axis=0)
    t = n + n_ctx
    cs = jnp.concatenate([jnp.ones((t, MLA_NOPE), F32), cos, jnp.zeros((t, 32), F32)], axis=-1)
    sn = jnp.concatenate([jnp.zeros((t, MLA_NOPE), F32), sin, jnp.zeros((t, 32), F32)], axis=-1)
    return cs, sn


def _block_diag2(w):
    z = jnp.zeros_like(w[0])
    return jnp.concatenate([jnp.concatenate([w[0], z], axis=1), jnp.concatenate([z, w[1]], axis=1)], axis=0)


def kernel(x, c, ctx, c_ctx, l0_mod_w, l0_mod_b, l0_w_in, l0_mla_q_norm, l0_mla_w_uq, l0_mla_kv_norm, l0_mla_w_uk, l0_mla_w_uv, l0_rwkv_mu, l0_rwkv_w0, l0_rwkv_w2, l0_rwkv_a0, l0_rwkv_a2, l0_rwkv_g2, l0_rwkv_k_k, l0_rwkv_k_a, l0_rwkv_r_k, l0_rwkv_gn_w, l0_rwkv_gn_b, l0_w_out, l0_ln1_g, l0_ln1_b, l0_mlp_w1, l0_mlp_w2, l0_ln2_g, l0_ln2_b, l1_mod_w, l1_mod_b, l1_w_in, l1_conv_w, l1_conv_b, l1_lru_ga_w, l1_lru_ga_b, l1_lru_gx_w, l1_lru_gx_b, l1_lru_lambda, l1_w_out, l1_ln1_g, l1_ln1_b, l1_mlp_w1, l1_mlp_w2, l1_ln2_g, l1_ln2_b):
    b, n, _ = x.shape
    n_ctx = ctx.shape[1]
    assert n_ctx == TM and n % TM == 0 and x.shape[2] == D
    row = lambda v: v.reshape(1, -1)

    xc = jnp.concatenate([ctx, x], axis=1)
    mod0 = _mod_table(c, c_ctx, l0_mod_w, l0_mod_b)
    mod1 = _mod_table(c, c_ctx, l1_mod_w, l1_mod_b)

    o_kv = MLA_Q_RANK
    o_kr = o_kv + MLA_KV_RANK
    o_rw = o_kr + MLA_ROPE
    w_kr = l0_w_in[:, o_kr:o_rw]
    zl = jnp.zeros((D, MLA_NOPE), F32)
    zr = jnp.zeros((D, HEAD_PAD - MLA_NOPE - MLA_ROPE), F32)
    win0 = jnp.concatenate([l0_w_in[:, :o_kr], zl, w_kr, zr, zl, _rot_cols(w_kr), zr,
                            l0_w_in[:, o_rw:]], axis=1).astype(BF16)
    wq = l0_mla_w_uq.reshape(MLA_Q_RANK, HEADS, MLA_NOPE + MLA_ROPE)
    q_nope, q_rope = wq[..., :MLA_NOPE], wq[..., MLA_NOPE:]
    zq = jnp.zeros((MLA_Q_RANK, HEADS, 32), F32)
    wqa = jnp.concatenate([q_nope, q_rope, zq], axis=-1).reshape(MLA_Q_RANK, HEADS * HEAD_PAD).astype(BF16)
    wqb = jnp.concatenate([jnp.zeros_like(q_nope), _rot_cols(q_rope), zq],
                          axis=-1).reshape(MLA_Q_RANK, HEADS * HEAD_PAD).astype(BF16)
    wk = l0_mla_w_uk.reshape(MLA_KV_RANK, HEADS, MLA_NOPE)
    wuk = jnp.concatenate([wk, jnp.zeros_like(wk)], axis=-1).reshape(MLA_KV_RANK, HEADS * HEAD_PAD).astype(BF16)
    wuv = l0_mla_w_uv.astype(BF16)
    cs, sn = _rope_tables(n, n_ctx)
    hid = jnp.arange(RW_W) // RW_HEAD
    ones_bd = (hid[:, None] == hid[None, :]).astype(BF16)

    q, k, v, frw = _in0(xc, mod0, win0, row(l0_mla_q_norm), row(l0_mla_kv_norm), wqa, wqb, wuk, wuv, cs, sn)
    att = _attention(q, k, v)
    sf, g, bonus = _rwfeat(frw, row(l0_rwkv_mu), row(l0_rwkv_w0), _block_diag2(l0_rwkv_w2).astype(BF16),
                           row(l0_rwkv_a0), _block_diag2(l0_rwkv_a2).astype(BF16), l0_rwkv_g2.astype(BF16),
                           row(l0_rwkv_k_k), row(l0_rwkv_k_a), row(l0_rwkv_r_k), ones_bd)
    ys = _rwkv_scan_both(sf)
    wo = l0_w_out.astype(BF16)
    xc = _out0(att, ys, bonus, g, xc, mod0, row(l0_rwkv_gn_w), row(l0_rwkv_gn_b), ones_bd,
               wo[:HEADS * MLA_V], wo[HEADS * MLA_V:], row(l0_ln1_g), row(l0_ln1_b))
    xc = _mlp(xc, mod0, l0_mlp_w1.astype(BF16), l0_mlp_w2.astype(BF16), row(l0_ln2_g), row(l0_ln2_b),
              combined=True, row_off=0)

    gate, xr = _in1(xc, mod1, l1_w_in.astype(BF16))
    wbd = jnp.concatenate([l1_lru_ga_w[0], l1_lru_gx_w[0], l1_lru_ga_w[1], l1_lru_gx_w[1]], axis=-1).astype(BF16)
    gb = jnp.stack([l1_lru_ga_b[0], l1_lru_gx_b[0], l1_lru_ga_b[1], l1_lru_gx_b[1]])
    hf, hr = _lru(xr, l1_conv_w, row(l1_conv_b), wbd, gb, l1_lru_lambda)
    xl = _out1(gate, hf, hr, xc, mod1, l1_w_out.astype(BF16), row(l1_ln1_g), row(l1_ln1_b))
    return _mlp(xl, mod1, l1_mlp_w1.astype(BF16), l1_mlp_w2.astype(BF16), row(l1_ln2_g), row(l1_ln2_b),
                combined=False, row_off=0)
```

```python
import functools
import math

import jax
import jax.numpy as jnp
from jax import lax
from jax.experimental import pallas as pl
from jax.experimental.pallas import tpu as pltpu

F32 = jnp.float32
BF16 = jnp.bfloat16

D = 1024
DEPTH = 2
N_MOD = 6
ALPHA = (2.0 * DEPTH) ** 0.25
LN_EPS = 1e-5
RMS_EPS = 1e-6

HEADS = 8
MLA_NOPE = 64
MLA_ROPE = 32
MLA_V = 64
MLA_Q_RANK = 384
MLA_KV_RANK = 256
ROPE_AXIS = MLA_ROPE // 2
ROPE_THETA = 10000.0
GRID_W = 64
ATT_SCALE = (MLA_NOPE + MLA_ROPE) ** -0.5
HEAD_PAD = 128
ATT_HEADS = 8

RW_HEAD = 64
RW_W = HEADS * RW_HEAD
LORA_W = 64
LORA_A = 64
LORA_G = 128
RW_IN = 3 * RW_W + 2 * LORA_W + 2 * LORA_A + LORA_G
GN_EPS = 64e-5
EXP_NEG_HALF = math.exp(-0.5)

LRU_W = D
LRU_BLOCKS = 8
LRU_BLOCK = LRU_W // LRU_BLOCKS
LRU_C = 8.0
D_FF = 4 * D

TM = 256
SUB = 8
SCAN_TC = 72
SCAN_TC_SMALL = 32
RELAY_T = 128
PITCH_F = RW_HEAD + SUB
PITCH_H = LRU_BLOCKS * SUB + SUB
LRU_TL = 32
VMEM_LIMIT = 56 * 1024 * 1024

IN0_COLS = MLA_Q_RANK + MLA_KV_RANK + 2 * HEAD_PAD + RW_IN


def _params(sem):
    return pltpu.CompilerParams(dimension_semantics=sem, vmem_limit_bytes=VMEM_LIMIT)


def _bdot(a, w):
    return jnp.dot(a.astype(BF16), w, preferred_element_type=F32)


def _sigmoid(x):
    return 1.0 / (1.0 + jnp.exp(-x))


def _layer_norm(z, g, b):
    mu = jnp.mean(z, axis=-1, keepdims=True)
    zc = z - mu
    var = jnp.mean(zc * zc, axis=-1, keepdims=True)
    return zc * lax.rsqrt(var + LN_EPS) * g + b


def _head_sum(x, ones_bd):
    hi = x.astype(BF16)
    lo = (x - hi.astype(F32)).astype(BF16)
    return (jnp.dot(hi, ones_bd, preferred_element_type=F32)
            + jnp.dot(lo, ones_bd, preferred_element_type=F32))


def _const_spec(shape):
    nd = len(shape)
    return pl.BlockSpec(shape, lambda *_: (0,) * nd)


def _mod_kernel(c_ref, w_ref, b_ref, o_ref):
    c = c_ref[...]
    s = c * _sigmoid(c)
    o_ref[...] = jnp.dot(s, w_ref[...], precision=lax.Precision.HIGHEST,
                         preferred_element_type=F32) + b_ref[...]


def _mod_table(c, c_ctx, mod_w, mod_b):
    b = c.shape[0]
    rows = 16
    cc = jnp.zeros((rows, D), F32).at[:b].set(c).at[b].set(c_ctx)
    tn = 1024
    out = pl.pallas_call(
        _mod_kernel,
        grid=(N_MOD * D // tn,),
        in_specs=[pl.BlockSpec((rows, D), lambda j: (0, 0)),
                  pl.BlockSpec((D, tn), lambda j: (0, j)),
                  pl.BlockSpec((1, tn), lambda j: (0, j))],
        out_specs=pl.BlockSpec((rows, tn), lambda j: (0, j)),
        out_shape=jax.ShapeDtypeStruct((rows, N_MOD * D), F32),
        compiler_params=_params(("arbitrary",)),
        name="mod_proj",
    )(cc, mod_w, mod_b.reshape(1, -1))
    lat = out[:b].reshape(b, N_MOD, D)
    ctx = jnp.broadcast_to(out[b].reshape(1, N_MOD, D), (b, N_MOD, D))
    tab = jnp.stack([ctx, lat], axis=1)
    return jnp.pad(tab, ((0, 0), (0, 0), (0, SUB - N_MOD), (0, 0)))


def _mod_spec(combined):
    if combined:
        return pl.BlockSpec((1, 1, SUB, D), lambda b, i: (b, jnp.minimum(i, 1), 0, 0))
    return pl.BlockSpec((1, 1, SUB, D), lambda b, i: (b, 1, 0, 0))


def _in0_kernel(x_ref, mod_ref, win_ref, qn_ref, kvn_ref, wqa_ref, wqb_ref, wuk_ref, wuv_ref,
                cs_ref, sn_ref, q_ref, k_ref, v_ref, frw_ref):
    x = x_ref[0]
    m = mod_ref[0, 0]
    h = x * (1.0 + m[1:2]) + m[0:1]
    f = _bdot(h, win_ref[...])
    o_kv = MLA_Q_RANK
    o_kr = o_kv + MLA_KV_RANK
    o_rw = o_kr + 2 * HEAD_PAD
    fq = f[:, :o_kv]
    fkv = f[:, o_kv:o_kr]
    fkr = f[:, o_kr:o_kr + HEAD_PAD]
    fkr_rot = f[:, o_kr + HEAD_PAD:o_rw]
    frw_ref[0] = f[:, o_rw:]
    qn = fq * lax.rsqrt(jnp.mean(fq * fq, axis=-1, keepdims=True) + RMS_EPS) * qn_ref[...]
    ckv = fkv * lax.rsqrt(jnp.mean(fkv * fkv, axis=-1, keepdims=True) + RMS_EPS) * kvn_ref[...]
    qn = qn.astype(BF16)
    ckv = ckv.astype(BF16)
    qa = jnp.dot(qn, wqa_ref[...], preferred_element_type=F32)
    qb = jnp.dot(qn, wqb_ref[...], preferred_element_type=F32)
    kn = jnp.dot(ckv, wuk_ref[...], preferred_element_type=F32)
    cs = cs_ref[...]
    sn = sn_ref[...]
    kr = fkr * cs + fkr_rot * sn
    for hd in range(HEADS):
        sl = slice(hd * HEAD_PAD, (hd + 1) * HEAD_PAD)
        q_ref[0, :, sl] = ((qa[:, sl] * cs + qb[:, sl] * sn) * ATT_SCALE).astype(BF16)
        k_ref[0, :, sl] = (kn[:, sl] + kr).astype(BF16)
    v_ref[0] = jnp.dot(ckv, wuv_ref[...], preferred_element_type=F32).astype(BF16)


def _in0(xc, modt, win, qn, kvn, wqa, wqb, wuk, wuv, cs, sn):
    b, t, _ = xc.shape
    row = lambda w: pl.BlockSpec((1, TM, w), lambda bb, i: (bb, i, 0))
    return pl.pallas_call(
        _in0_kernel,
        grid=(b, t // TM),
        in_specs=[row(D), _mod_spec(True), _const_spec(win.shape), _const_spec(qn.shape),
                  _const_spec(kvn.shape), _const_spec(wqa.shape), _const_spec(wqb.shape),
                  _const_spec(wuk.shape), _const_spec(wuv.shape),
                  pl.BlockSpec((TM, HEAD_PAD), lambda bb, i: (i, 0)),
                  pl.BlockSpec((TM, HEAD_PAD), lambda bb, i: (i, 0))],
        out_specs=[row(HEADS * HEAD_PAD), row(HEADS * HEAD_PAD), row(HEADS * MLA_V), row(RW_IN)],
        out_shape=[jax.ShapeDtypeStruct((b, t, HEADS * HEAD_PAD), BF16),
                   jax.ShapeDtypeStruct((b, t, HEADS * HEAD_PAD), BF16),
                   jax.ShapeDtypeStruct((b, t, HEADS * MLA_V), BF16),
                   jax.ShapeDtypeStruct((b, t, RW_IN), F32)],
        compiler_params=_params(("parallel", "parallel")),
        name="l0_in_proj",
    )(xc, modt, win, qn, kvn, wqa, wqb, wuk, wuv, cs, sn)


def _att_kernel(q_ref, k_ref, v_ref, o_ref, *, n_ctx, n_all):
    i = pl.program_id(2)

    def attend(nk):
        lane = lax.broadcasted_iota(jnp.int32, (nk, 2 * MLA_V), 1)
        for pair in range(ATT_HEADS // 2):
            v = v_ref[0, :nk, pair * 2 * MLA_V:(pair + 1) * 2 * MLA_V]
            acc = None
            for hh in range(2):
                hd = 2 * pair + hh
                sl = slice(hd * HEAD_PAD, (hd + 1) * HEAD_PAD)
                q = q_ref[0, :, sl]
                k = k_ref[0, :nk, sl]
                s = lax.dot_general(q, k, (((1,), (1,)), ((), ())), preferred_element_type=F32)
                p = jnp.exp(s - jnp.max(s, axis=-1, keepdims=True))
                l = jnp.sum(p, axis=-1, keepdims=True)
                keep = (lane < MLA_V) if hh == 0 else (lane >= MLA_V)
                vh = jnp.where(keep, v, jnp.zeros_like(v))
                o = jnp.dot(p.astype(BF16), vh, preferred_element_type=F32) / l
                acc = o if acc is None else acc + o
            o_ref[0, :, pair * 2 * MLA_V:(pair + 1) * 2 * MLA_V] = acc.astype(o_ref.dtype)

    @pl.when(i == 0)
    def _():
        attend(n_ctx)

    @pl.when(i > 0)
    def _():
        attend(n_all)


def _attention(q, k, v):
    b, t, _ = q.shape
    return pl.pallas_call(
        functools.partial(_att_kernel, n_ctx=TM, n_all=t),
        grid=(b, HEADS // ATT_HEADS, t // TM),
        in_specs=[pl.BlockSpec((1, TM, ATT_HEADS * HEAD_PAD), lambda bb, hp, i: (bb, i, hp)),
                  pl.BlockSpec((1, t, ATT_HEADS * HEAD_PAD), lambda bb, hp, i: (bb, 0, hp)),
                  pl.BlockSpec((1, t, ATT_HEADS * MLA_V), lambda bb, hp, i: (bb, 0, hp))],
        out_specs=pl.BlockSpec((1, TM, ATT_HEADS * MLA_V), lambda bb, hp, i: (bb, i, hp)),
        out_shape=jax.ShapeDtypeStruct((b, t, HEADS * MLA_V), BF16),
        compiler_params=_params(("parallel", "parallel", "parallel")),
        name="l0_attention",
    )(q, k, v)


def _halo_flags(i, n_tiles):
    return i >= 2, jnp.logical_and(i >= 1, i < n_tiles - 1)


def _rwfeat_kernel(f_ref, fp_ref, fn_ref, mu_ref, w0_ref, w2_ref, a0_ref, a2_ref, g2_ref,
                   kk_ref, ka_ref, rk_ref, ones_ref, sf_ref, g_ref, bonus_ref):
    i = pl.program_id(1)
    has_prev, has_next = _halo_flags(i, pl.num_programs(1))
    f = f_ref[0]
    prow = jnp.where(has_prev, fp_ref[0, SUB - 1:SUB, :], 0.0)
    nrow = jnp.where(has_next, fn_ref[0, 0:1, :], 0.0)
    rid = lax.broadcasted_iota(jnp.int32, (TM, 1), 0)
    prev = jnp.where(rid == 0, prow, pltpu.roll(f, 1, axis=0))
    nxt = jnp.where(rid == TM - 1, nrow, pltpu.roll(f, TM - 1, axis=0))
    f = f + mu_ref[...] * (0.5 * (prev + nxt) - f)
    r = f[:, 0:RW_W]
    k = f[:, RW_W:2 * RW_W]
    v = f[:, 2 * RW_W:3 * RW_W]
    o = 3 * RW_W
    wl = f[:, o:o + 2 * LORA_W]
    al = f[:, o + 2 * LORA_W:o + 2 * LORA_W + 2 * LORA_A]
    gl = f[:, o + 2 * LORA_W + 2 * LORA_A:]
    w_raw = w0_ref[...] + _bdot(jnp.tanh(wl), w2_ref[...])
    decay = jnp.exp(-EXP_NEG_HALF * _sigmoid(w_raw))
    a = _sigmoid(a0_ref[...] + _bdot(al, a2_ref[...]))
    g_ref[0] = _bdot(_sigmoid(gl), g2_ref[...])
    ones_bd = ones_ref[...]
    kk = k * kk_ref[...]
    kk = kk * lax.rsqrt(_head_sum(kk * kk, ones_bd) + 1e-12)
    ka = ka_ref[...]
    kd = [k * (1.0 + (a[:, d * RW_W:(d + 1) * RW_W] - 1.0) * ka) for d in range(2)]
    bonus_ref[0] = _head_sum(r * (kd[0] + kd[1]) * rk_ref[...], ones_bd) * v
    parts = [r, kk, v, decay[:, :RW_W], kd[0], kk * a[:, :RW_W],
             decay[:, RW_W:], kd[1], kk * a[:, RW_W:]]
    pad = jnp.zeros((PITCH_F - RW_HEAD, TM), F32)
    for n, p in enumerate(parts):
        for hp in range(HEADS // 2):
            for half in range(TM // 128):
                pt = p[half * 128:(half + 1) * 128, hp * 128:(hp + 1) * 128].T
                for h2 in range(2):
                    row0 = (2 * hp + h2) * PITCH_F
                    sf_ref[0, n, row0:row0 + RW_HEAD, half * 128:(half + 1) * 128] = pt[h2 * RW_HEAD:(h2 + 1) * RW_HEAD]
        for hd in range(HEADS):
            sf_ref[0, n, hd * PITCH_F + RW_HEAD:(hd + 1) * PITCH_F, :] = pad


def _halo_specs(width, t):
    nb = TM // SUB
    last = t // SUB - 1
    prev = pl.BlockSpec((1, SUB, width), lambda bb, i: (bb, jnp.maximum(i * nb - 1, 0), 0))
    nxt = pl.BlockSpec((1, SUB, width), lambda bb, i: (bb, jnp.minimum((i + 1) * nb, last), 0))
    return prev, nxt


def _rwfeat(frw, mu, w0, w2bd, a0, a2bd, g2, k_k, k_a, r_k, ones_bd):
    b, t, _ = frw.shape
    row = lambda w: pl.BlockSpec((1, TM, w), lambda bb, i: (bb, i, 0))
    prev, nxt = _halo_specs(RW_IN, t)
    consts = (mu, w0, w2bd, a0, a2bd, g2, k_k, k_a, r_k, ones_bd)
    return pl.pallas_call(
        _rwfeat_kernel,
        grid=(b, t // TM),
        in_specs=[row(RW_IN), prev, nxt] + [_const_spec(c.shape) for c in consts],
        out_specs=[pl.BlockSpec((1, 9, HEADS * PITCH_F, TM), lambda bb, i: (bb, 0, 0, i)), row(RW_W), row(RW_W)],
        out_shape=[jax.ShapeDtypeStruct((b, 9, HEADS * PITCH_F, t), F32),
                   jax.ShapeDtypeStruct((b, t, RW_W), F32),
                   jax.ShapeDtypeStruct((b, t, RW_W), F32)],
        compiler_params=_params(("parallel", "parallel")),
        name="l0_rwkv_feat",
    )(frw, frw, frw, *consts)


def _time_mirror(i, n_ctx_tiles, n_tiles):
    return jnp.where(i < n_ctx_tiles, n_ctx_tiles - 1 - i, n_tiles + n_ctx_tiles - 1 - i)


def _reverse_lanes(x, j3):
    hi = x.astype(BF16)
    r1 = x - hi.astype(F32)
    mid = r1.astype(BF16)
    lo = (r1 - mid.astype(F32)).astype(BF16)
    return jnp.dot(jnp.concatenate([hi, mid, lo], axis=1), j3, preferred_element_type=F32)


RELAY_FB = 4


def _relayout_in_kernel(xa_ref, xb_ref, j3_ref, o_ref, *, step_major):
    nb = xa_ref.shape[0]
    tr = xa_ref.shape[3]
    half = nb * HEADS

    def gather(x_ref, f):
        return jnp.concatenate([x_ref[b, 0, pl.ds(f, HEADS, stride=PITCH_F), :] for b in range(nb)], axis=0)

    for f0 in range(0, RW_HEAD, RELAY_FB):
        fs = range(f0, f0 + RELAY_FB)
        rev = _reverse_lanes(jnp.concatenate([gather(xb_ref, f) for f in fs], axis=0), j3_ref[...])
        for n, f in enumerate(fs):
            mt = jnp.concatenate([gather(xa_ref, f), rev[n * half:(n + 1) * half]], axis=0).T
            if step_major:
                o_ref[pl.ds(f, tr, stride=PITCH_F), :] = mt
            else:
                o_ref[0, f] = mt
    if step_major:
        for f in range(RW_HEAD, PITCH_F):
            o_ref[pl.ds(f, tr, stride=PITCH_F), :] = jnp.zeros((tr, 2 * half), F32)


def _relayout_in(sf, j3):
    b, _, rows, t = sf.shape
    chains = 2 * b * HEADS
    nt = t // RELAY_T
    nc = TM // RELAY_T
    mirror = lambda i: _time_mirror(i, nc, nt)
    blk = (b, 1, rows, RELAY_T)
    dirdep = lambda p: jnp.minimum(p // 2, 1)
    x = pl.pallas_call(
        functools.partial(_relayout_in_kernel, step_major=False),
        grid=(nt, 5),
        in_specs=[pl.BlockSpec(blk, lambda i, p: (0, p + dirdep(p), 0, i)),
                  pl.BlockSpec(blk, lambda i, p: (0, p + 4 * dirdep(p), 0, mirror(i))),
                  _const_spec(j3.shape)],
        out_specs=pl.BlockSpec((1, RW_HEAD, RELAY_T, chains), lambda i, p: (p, 0, i, 0)),
        out_shape=jax.ShapeDtypeStruct((5, RW_HEAD, t, chains), F32),
        compiler_params=_params(("parallel", "parallel")),
        name="l0_rwkv_relayout_in",
    )(sf, sf, j3)
    v = pl.pallas_call(
        functools.partial(_relayout_in_kernel, step_major=True),
        grid=(nt,),
        in_specs=[pl.BlockSpec(blk, lambda i: (0, 2, 0, i)),
                  pl.BlockSpec(blk, lambda i: (0, 2, 0, mirror(i))),
                  _const_spec(j3.shape)],
        out_specs=pl.BlockSpec((RELAY_T * PITCH_F, chains), lambda i: (i, 0)),
        out_shape=jax.ShapeDtypeStruct((t * PITCH_F, chains), F32),
        compiler_params=_params(("parallel",)),
        name="l0_rwkv_relayout_v",
    )(sf, sf, j3)
    return x, v


def _rwscan_kernel(x_ref, v_ref, y_ref, s_ref):
    tc = x_ref.shape[2]
    chains = x_ref.shape[3]

    @pl.when(pl.program_id(0) == 0)
    def _():
        s_ref[...] = jnp.zeros_like(s_ref)

    groups = RW_HEAD // SUB
    zeros = tuple(jnp.zeros((SUB, chains), F32) for _ in range(groups))

    def row(a, j, s):
        return jnp.broadcast_to(x_ref[a, j, pl.ds(s, 1), :], (SUB, chains))

    def s_at(j, g):
        return s_ref.at[j, g * SUB:(g + 1) * SUB, :]

    sa0 = list(zeros)
    for j in range(RW_HEAD):
        kk = row(1, j, 0)
        for g in range(groups):
            sa0[g] = sa0[g] + s_at(j, g)[...] * kk

    def step(s, sa):
        base = pl.multiple_of(s * PITCH_F, SUB)
        v = v_ref[pl.ds(base, RW_HEAD), :]
        s_next = jnp.minimum(s + 1, tc - 1)
        y = list(zeros)
        sa_next = list(zeros)
        for j in range(RW_HEAD):
            r, kk_next, w, k, kka = row(0, j, s), row(1, j, s_next), row(2, j, s), row(3, j, s), row(4, j, s)
            for g in range(groups):
                sj = s_at(j, g)[...] * w - sa[g] * kka + v[g * SUB:(g + 1) * SUB] * k
                s_at(j, g)[...] = sj
                y[g] = y[g] + sj * r
                sa_next[g] = sa_next[g] + sj * kk_next
        y_ref[pl.ds(base, RW_HEAD), :] = jnp.concatenate(y, axis=0)
        y_ref[pl.ds(base + RW_HEAD, PITCH_F - RW_HEAD), :] = zeros[0]
        return tuple(sa_next)

    lax.fori_loop(0, tc, step, tuple(sa0))


def _rwscan(x, v):
    _, _, t, chains = x.shape
    tc = SCAN_TC if t % SCAN_TC == 0 else SCAN_TC_SMALL
    return pl.pallas_call(
        _rwscan_kernel,
        grid=(t // tc,),
        in_specs=[pl.BlockSpec((5, RW_HEAD, tc, chains), lambda i: (0, 0, i, 0)),
                  pl.BlockSpec((tc * PITCH_F, chains), lambda i: (i, 0))],
        out_specs=pl.BlockSpec((tc * PITCH_F, chains), lambda i: (i, 0)),
        out_shape=jax.ShapeDtypeStruct((t * PITCH_F, chains), F32),
        scratch_shapes=[pltpu.VMEM((RW_HEAD, RW_HEAD, chains), F32)],
        compiler_params=_params(("arbitrary",)),
        name="l0_rwkv_scan",
    )(x, v)


def _relayout_out_kernel(yf_ref, yr_ref, j3_ref, o_ref, q_ref):
    nb = o_ref.shape[0]
    tr = o_ref.shape[1]
    chains = yf_ref.shape[1]
    half = chains // 2
    pitch_c = chains + SUB
    fwd = lax.broadcasted_iota(jnp.int32, (tr, chains), 1) < half
    for i in range(RW_HEAD):
        rows = pl.ds(i, tr, stride=PITCH_F)
        mt = jnp.where(fwd, yf_ref[rows, :], yr_ref[rows, :]).T
        q_ref[i * pitch_c:i * pitch_c + half, :] = mt[:half]
        q_ref[i * pitch_c + half:i * pitch_c + chains, :] = _reverse_lanes(mt[half:], j3_ref[...])
    for b in range(nb):
        for hp in range(HEADS // 2):
            parts = []
            for h2 in range(2):
                c = b * HEADS + 2 * hp + h2
                parts.append(q_ref[pl.ds(c, RW_HEAD, stride=pitch_c), :]
                             + q_ref[pl.ds(half + c, RW_HEAD, stride=pitch_c), :])
            o_ref[b, :, hp * 128:(hp + 1) * 128] = jnp.concatenate(parts, axis=0).T


def _relayout_out(y, j3, b):
    chains = y.shape[1]
    t = y.shape[0] // PITCH_F
    nt = t // RELAY_T
    nc = TM // RELAY_T
    return pl.pallas_call(
        _relayout_out_kernel,
        grid=(nt,),
        in_specs=[pl.BlockSpec((RELAY_T * PITCH_F, chains), lambda i: (i, 0)),
                  pl.BlockSpec((RELAY_T * PITCH_F, chains), lambda i: (_time_mirror(i, nc, nt), 0)),
                  _const_spec(j3.shape)],
        out_specs=pl.BlockSpec((b, RELAY_T, RW_W), lambda i: (0, i, 0)),
        out_shape=jax.ShapeDtypeStruct((b, t, RW_W), F32),
        scratch_shapes=[pltpu.VMEM((RW_HEAD * (chains + SUB), RELAY_T), F32)],
        compiler_params=_params(("parallel",)),
        name="l0_rwkv_relayout_out",
    )(y, y, j3)


def _rwkv_scan_both(sf):
    k = jnp.arange(RELAY_T)
    anti = (k[:, None] + k[None, :] == RELAY_T - 1).astype(BF16)
    j3 = jnp.concatenate([anti, anti, anti], axis=0)
    y = _rwscan(*_relayout_in(sf, j3))
    return _relayout_out(y, j3, sf.shape[0])


def _out0_kernel(att_ref, ys_ref, bonus_ref, g_ref, x_ref, mod_ref, gnw_ref, gnb_ref, ones_ref,
                 woa_ref, wob_ref, lng_ref, lnb_ref, o_ref):
    ones_bd = ones_ref[...]
    y = ys_ref[0]
    mu = _head_sum(y, ones_bd) * (1.0 / RW_HEAD)
    yc = y - mu
    var = _head_sum(yc * yc, ones_bd) * (1.0 / RW_HEAD)
    yn = yc * lax.rsqrt(var + GN_EPS) * gnw_ref[...] + gnb_ref[...]
    rw = (yn + bonus_ref[0]) * g_ref[0]
    o = (jnp.dot(att_ref[0], woa_ref[...], preferred_element_type=F32)
         + _bdot(rw, wob_ref[...]))
    m = mod_ref[0, 0]
    o_ref[0] = _layer_norm(ALPHA * x_ref[0] + m[2:3] * o, lng_ref[...], lnb_ref[...])


def _out0(att, ys, bonus, g, xc, modt, gnw, gnb, ones_bd, woa, wob, lng, lnb):
    b, t, _ = xc.shape
    row = lambda w: pl.BlockSpec((1, TM, w), lambda bb, i: (bb, i, 0))
    consts = (gnw, gnb, ones_bd, woa, wob, lng, lnb)
    return pl.pallas_call(
        _out0_kernel,
        grid=(b, t // TM),
        in_specs=[row(RW_W), row(RW_W), row(RW_W), row(RW_W), row(D), _mod_spec(True)]
                 + [_const_spec(c.shape) for c in consts],
        out_specs=row(D),
        out_shape=jax.ShapeDtypeStruct((b, t, D), F32),
        compiler_params=_params(("parallel", "parallel")),
        name="l0_out_proj",
    )(att, ys, bonus, g, xc, modt, *consts)


def _mlp_kernel(x_ref, mod_ref, w1_ref, w2_ref, lng_ref, lnb_ref, o_ref):
    x = x_ref[0]
    m = mod_ref[0, 0]
    h = (x * (1.0 + m[4:5]) + m[3:4]).astype(BF16)
    acc = None
    fc = 1024
    for c in range(D_FF // fc):
        u = jnp.dot(h, w1_ref[:, c * fc:(c + 1) * fc], preferred_element_type=F32)
        u = jnp.square(jnp.maximum(u, 0.0)).astype(BF16)
        part = jnp.dot(u, w2_ref[c * fc:(c + 1) * fc, :], preferred_element_type=F32)
        acc = part if acc is None else acc + part
    o_ref[0] = _layer_norm(ALPHA * x + m[5:6] * acc, lng_ref[...], lnb_ref[...])


def _mlp(x, modt, w1, w2, lng, lnb, *, combined, row_off):
    b, tx, _ = x.shape
    nt = tx // TM - row_off
    single = pl.Buffered(1)
    wspec = lambda shape: pl.BlockSpec(shape, lambda *_: (0,) * len(shape), pipeline_mode=single)
    return pl.pallas_call(
        _mlp_kernel,
        grid=(b, nt),
        in_specs=[pl.BlockSpec((1, TM, D), lambda bb, i: (bb, i + row_off, 0)), _mod_spec(combined),
                  wspec(w1.shape), wspec(w2.shape), _const_spec(lng.shape), _const_spec(lnb.shape)],
        out_specs=pl.BlockSpec((1, TM, D), lambda bb, i: (bb, i, 0)),
        out_shape=jax.ShapeDtypeStruct((b, nt * TM, D), F32),
        compiler_params=_params(("parallel", "parallel")),
        name="mlp",
    )(x, modt, w1, w2, lng, lnb)


def _in1_kernel(x_ref, mod_ref, w_ref, gate_ref, xr_ref):
    m = mod_ref[0, 0]
    h = x_ref[0] * (1.0 + m[1:2]) + m[0:1]
    f = _bdot(h, w_ref[...])
    gate_ref[0] = f[:, :LRU_W]
    xr_ref[0] = f[:, LRU_W:]


def _in1(xc, modt, w):
    b, t, _ = xc.shape
    row = lambda wd: pl.BlockSpec((1, TM, wd), lambda bb, i: (bb, i, 0))
    return pl.pallas_call(
        _in1_kernel,
        grid=(b, t // TM),
        in_specs=[row(D), _mod_spec(True), _const_spec(w.shape)],
        out_specs=[row(LRU_W), row(LRU_W)],
        out_shape=[jax.ShapeDtypeStruct((b, t, LRU_W), F32)] * 2,
        compiler_params=_params(("parallel", "parallel")),
        name="l1_in_proj",
    )(xc, modt, w)


def _lru_kernel(xf_ref, xfp_ref, xfn_ref, xr_ref, xrp_ref, xrn_ref, cw_ref, cb_ref, wbd_ref, gb_ref, lam_ref,
                hf_ref, hr_ref, a_scr, u_scr, cf_ref, cr_ref):
    i = pl.program_id(0)
    n = pl.num_programs(0)
    nb, tl, _ = xf_ref.shape
    rows = nb * tl
    nc = TM // tl
    pitch_b = tl + SUB

    @pl.when(i == 0)
    def _():
        cf_ref[...] = jnp.zeros_like(cf_ref)
        cr_ref[...] = jnp.zeros_like(cr_ref)

    lam = lam_ref[...]
    nl = -lam
    softplus = jnp.maximum(nl, 0.0) + jnp.log1p(jnp.exp(-jnp.abs(nl)))
    gb = gb_ref[...]
    cw = cw_ref[...]
    tpos = jnp.bitwise_and(lax.broadcasted_iota(jnp.int32, (rows, 1), 0), tl - 1)
    tile_r = _time_mirror(i, nc, n)
    dirs = ((xf_ref, xfp_ref, xfn_ref, i, hf_ref, cf_ref), (xr_ref, xrp_ref, xrn_ref, tile_r, hr_ref, cr_ref))
    for d, (x_ref, xp_ref, xn_ref, tile, h_ref, c_ref) in enumerate(dirs):
        seg_first = jnp.logical_or(tile == 0, tile == nc)
        seg_last = jnp.logical_or(tile == nc - 1, tile == n - 1)

        def halo(ref, r, edge):
            per_b = [jnp.broadcast_to(ref[b, r:r + 1, :], (tl, LRU_W)) for b in range(nb)]
            return jnp.where(edge, 0.0, jnp.concatenate(per_b, axis=0))

        x = x_ref[...].reshape(rows, LRU_W)
        p1 = halo(xp_ref, SUB - 1, seg_first)
        p2 = halo(xp_ref, SUB - 2, seg_first)
        n1 = halo(xn_ref, 0, seg_last)
        xm1 = jnp.where(tpos == 0, p1, pltpu.roll(x, 1, axis=0))
        xm2 = jnp.where(tpos == 0, p2, jnp.where(tpos == 1, p1, pltpu.roll(x, 2, axis=0)))
        xp1 = jnp.where(tpos == tl - 1, n1, pltpu.roll(x, rows - 1, axis=0))
        xc = cw[0:1] * xm2 + cw[1:2] * xm1 + cw[2:3] * x + cw[3:4] * xp1 + cb_ref[...]
        for blk in range(LRU_BLOCKS):
            sl = slice(blk * LRU_BLOCK, (blk + 1) * LRU_BLOCK)
            xb = xc[:, sl]
            z = _bdot(xb, wbd_ref[blk, :, 2 * d * LRU_BLOCK:(2 * d + 2) * LRU_BLOCK])
            zr = z[:, :LRU_BLOCK] + gb[2 * d:2 * d + 1, sl]
            zi = z[:, LRU_BLOCK:] + gb[2 * d + 1:2 * d + 2, sl]
            log_a = -LRU_C * _sigmoid(zr) * softplus[d:d + 1, sl]
            a = jnp.exp(log_a)
            u = jnp.sqrt(-jnp.tanh(log_a) * (a * a + 1.0)) * (_sigmoid(zi) * xb)
            for b in range(nb):
                a_scr[blk, b * pitch_b:b * pitch_b + tl, :] = a[b * tl:(b + 1) * tl]
                u_scr[blk, b * pitch_b:b * pitch_b + tl, :] = u[b * tl:(b + 1) * tl]
        h = [c_ref[:, blk * LRU_BLOCK:(blk + 1) * LRU_BLOCK] for blk in range(LRU_BLOCKS)]
        for step in range(tl):
            t = step if d == 0 else tl - 1 - step
            for blk in range(LRU_BLOCKS):
                at = a_scr[blk, pl.ds(t, nb, stride=pitch_b), :]
                ut = u_scr[blk, pl.ds(t, nb, stride=pitch_b), :]
                h[blk] = at * h[blk] + ut
                h_ref[t * PITCH_H + blk * SUB:t * PITCH_H + blk * SUB + nb, :] = h[blk]
            h_ref[t * PITCH_H + LRU_BLOCKS * SUB:(t + 1) * PITCH_H, :] = jnp.zeros((SUB, LRU_BLOCK), F32)
        for blk in range(LRU_BLOCKS):
            c_ref[:, blk * LRU_BLOCK:(blk + 1) * LRU_BLOCK] = h[blk]


def _lru(xr, cw, cb, wbd, gb, lam):
    b, t, _ = xr.shape
    assert b <= SUB
    n = t // LRU_TL
    nc = TM // LRU_TL
    hb = LRU_TL // SUB
    last = t // SUB - 1
    mirror = lambda i: _time_mirror(i, nc, n)

    def specs(tile):
        return [pl.BlockSpec((b, LRU_TL, LRU_W), lambda i: (0, tile(i), 0)),
                pl.BlockSpec((b, SUB, LRU_W), lambda i: (0, jnp.maximum(tile(i) * hb - 1, 0), 0)),
                pl.BlockSpec((b, SUB, LRU_W), lambda i: (0, jnp.minimum((tile(i) + 1) * hb, last), 0))]

    consts = (cw, cb, wbd, gb, lam)
    hspec = lambda tile: pl.BlockSpec((LRU_TL * PITCH_H, LRU_BLOCK), lambda i: (tile(i), 0))
    return pl.pallas_call(
        _lru_kernel,
        grid=(n,),
        in_specs=specs(lambda i: i) + specs(mirror) + [_const_spec(c.shape) for c in consts],
        out_specs=[hspec(lambda i: i), hspec(mirror)],
        out_shape=[jax.ShapeDtypeStruct((t * PITCH_H, LRU_BLOCK), F32)] * 2,
        scratch_shapes=[pltpu.VMEM((LRU_BLOCKS, b * (LRU_TL + SUB), LRU_BLOCK), F32),
                        pltpu.VMEM((LRU_BLOCKS, b * (LRU_TL + SUB), LRU_BLOCK), F32),
                        pltpu.VMEM((b, LRU_W), F32), pltpu.VMEM((b, LRU_W), F32)],
        compiler_params=_params(("arbitrary",)),
        name="l1_lru",
    )(xr, xr, xr, xr, xr, xr, *consts)


def _out1_kernel(gate_ref, hf_ref, hr_ref, x_ref, mod_ref, w_ref, lng_ref, lnb_ref, o_ref):
    nb, tl, _ = gate_ref.shape
    per_b = []
    for b in range(nb):
        cols = []
        for blk in range(LRU_BLOCKS):
            rows = pl.ds(blk * SUB + b, tl, stride=PITCH_H)
            cols.append(hf_ref[rows, :] + hr_ref[rows, :])
        per_b.append(jnp.concatenate(cols, axis=1))
    h = jnp.concatenate(per_b, axis=0)
    gate = gate_ref[...].reshape(nb * tl, LRU_W)
    gelu = 0.5 * gate * (1.0 + jnp.tanh(math.sqrt(2.0 / math.pi) * (gate + 0.044715 * gate * gate * gate)))
    o = _bdot(gelu * h, w_ref[...])
    for b in range(nb):
        m = mod_ref[b, 0]
        z = ALPHA * x_ref[b] + m[2:3] * o[b * tl:(b + 1) * tl]
        o_ref[b] = _layer_norm(z, lng_ref[...], lnb_ref[...])


def _out1(gate, hf, hr, xc, modt, w, lng, lnb):
    b, t, _ = xc.shape
    off = TM // LRU_TL
    nt = t // LRU_TL - off
    lat = lambda wd: pl.BlockSpec((b, LRU_TL, wd), lambda i: (0, i + off, 0))
    hspec = pl.BlockSpec((LRU_TL * PITCH_H, LRU_BLOCK), lambda i: (i + off, 0))
    consts = (w, lng, lnb)
    return pl.pallas_call(
        _out1_kernel,
        grid=(nt,),
        in_specs=[lat(LRU_W), hspec, hspec, lat(D),
                  pl.BlockSpec((b, 1, SUB, D), lambda i: (0, 1, 0, 0))]
                 + [_const_spec(c.shape) for c in consts],
        out_specs=pl.BlockSpec((b, LRU_TL, D), lambda i: (0, i, 0)),
        out_shape=jax.ShapeDtypeStruct((b, nt * LRU_TL, D), F32),
        compiler_params=_params(("parallel",)),
        name="l1_out_proj",
    )(gate, hf, hr, xc, modt, *consts)


def _rot_cols(w):
    ws = w.reshape(w.shape[:-1] + (2, 2, ROPE_AXIS // 2))
    return jnp.stack([-ws[..., 1, :], ws[..., 0, :]], axis=-2).reshape(w.shape)


def _rope_tables(n, n_ctx):
    rows_n = n // GRID_W
    rows = jnp.repeat(jnp.arange(rows_n, dtype=F32), GRID_W)
    cols = jnp.tile(jnp.arange(GRID_W, dtype=F32), rows_n)
    inv_freq = ROPE_THETA ** (-jnp.arange(0, ROPE_AXIS, 2, dtype=F32) / ROPE_AXIS)
    ang_r = rows[:, None] * inv_freq
    ang_c = cols[:, None] * inv_freq
    ang = jnp.concatenate([ang_r, ang_r, ang_c, ang_c], axis=-1)
    cos = jnp.concatenate([jnp.ones((n_ctx, MLA_ROPE), F32), jnp.cos(ang)], axis=0)
    sin = jnp.concatenate([jnp.zeros((n_ctx, MLA_ROPE), F32), jnp.sin(ang)], axis=0)
    t = n + n_ctx
    cs = jnp.concatenate([jnp.ones((t, MLA_NOPE), F32), cos, jnp.zeros((t, 32), F32)], axis=-1)
    sn = jnp.concatenate([jnp.zeros((t, MLA_NOPE), F32), sin, jnp.zeros((t, 32), F32)], axis=-1)
    return cs, sn


def _block_diag2(w):
    z = jnp.zeros_like(w[0])
    return jnp.concatenate([jnp.concatenate([w[0], z], axis=1), jnp.concatenate([z, w[1]], axis=1)], axis=0)


def kernel(x, c, ctx, c_ctx, l0_mod_w, l0_mod_b, l0_w_in, l0_mla_q_norm, l0_mla_w_uq, l0_mla_kv_norm, l0_mla_w_uk, l0_mla_w_uv, l0_rwkv_mu, l0_rwkv_w0, l0_rwkv_w2, l0_rwkv_a0, l0_rwkv_a2, l0_rwkv_g2, l0_rwkv_k_k, l0_rwkv_k_a, l0_rwkv_r_k, l0_rwkv_gn_w, l0_rwkv_gn_b, l0_w_out, l0_ln1_g, l0_ln1_b, l0_mlp_w1, l0_mlp_w2, l0_ln2_g, l0_ln2_b, l1_mod_w, l1_mod_b, l1_w_in, l1_conv_w, l1_conv_b, l1_lru_ga_w, l1_lru_ga_b, l1_lru_gx_w, l1_lru_gx_b, l1_lru_lambda, l1_w_out, l1_ln1_g, l1_ln1_b, l1_mlp_w1, l1_mlp_w2, l1_ln2_g, l1_ln2_b):
    b, n, _ = x.shape
    n_ctx = ctx.shape[1]
    assert n_ctx == TM and n % TM == 0 and x.shape[2] == D
    row = lambda v: v.reshape(1, -1)

    xc = jnp.concatenate([ctx, x], axis=1)
    mod0 = _mod_table(c, c_ctx, l0_mod_w, l0_mod_b)
    mod1 = _mod_table(c, c_ctx, l1_mod_w, l1_mod_b)

    o_kv = MLA_Q_RANK
    o_kr = o_kv + MLA_KV_RANK
    o_rw = o_kr + MLA_ROPE
    w_kr = l0_w_in[:, o_kr:o_rw]
    zl = jnp.zeros((D, MLA_NOPE), F32)
    zr = jnp.zeros((D, HEAD_PAD - MLA_NOPE - MLA_ROPE), F32)
    win0 = jnp.concatenate([l0_w_in[:, :o_kr], zl, w_kr, zr, zl, _rot_cols(w_kr), zr,
                            l0_w_in[:, o_rw:]], axis=1).astype(BF16)
    wq = l0_mla_w_uq.reshape(MLA_Q_RANK, HEADS, MLA_NOPE + MLA_ROPE)
    q_nope, q_rope = wq[..., :MLA_NOPE], wq[..., MLA_NOPE:]
    zq = jnp.zeros((MLA_Q_RANK, HEADS, 32), F32)
    wqa = jnp.concatenate([q_nope, q_rope, zq], axis=-1).reshape(MLA_Q_RANK, HEADS * HEAD_PAD).astype(BF16)
    wqb = jnp.concatenate([jnp.zeros_like(q_nope), _rot_cols(q_rope), zq],
                          axis=-1).reshape(MLA_Q_RANK, HEADS * HEAD_PAD).astype(BF16)
    wk = l0_mla_w_uk.reshape(MLA_KV_RANK, HEADS, MLA_NOPE)
    wuk = jnp.concatenate([wk, jnp.zeros_like(wk)], axis=-1).reshape(MLA_KV_RANK, HEADS * HEAD_PAD).astype(BF16)
    wuv = l0_mla_w_uv.astype(BF16)
    cs, sn = _rope_tables(n, n_ctx)
    hid = jnp.arange(RW_W) // RW_HEAD
    ones_bd = (hid[:, None] == hid[None, :]).astype(BF16)

    q, k, v, frw = _in0(xc, mod0, win0, row(l0_mla_q_norm), row(l0_mla_kv_norm), wqa, wqb, wuk, wuv, cs, sn)
    att = _attention(q, k, v)
    sf, g, bonus = _rwfeat(frw, row(l0_rwkv_mu), row(l0_rwkv_w0), _block_diag2(l0_rwkv_w2).astype(BF16),
                           row(l0_rwkv_a0), _block_diag2(l0_rwkv_a2).astype(BF16), l0_rwkv_g2.astype(BF16),
                           row(l0_rwkv_k_k), row(l0_rwkv_k_a), row(l0_rwkv_r_k), ones_bd)
    ys = _rwkv_scan_both(sf)
    wo = l0_w_out.astype(BF16)
    xc = _out0(att, ys, bonus, g, xc, mod0, row(l0_rwkv_gn_w), row(l0_rwkv_gn_b), ones_bd,
               wo[:HEADS * MLA_V], wo[HEADS * MLA_V:], row(l0_ln1_g), row(l0_ln1_b))
    xc = _mlp(xc, mod0, l0_mlp_w1.astype(BF16), l0_mlp_w2.astype(BF16), row(l0_ln2_g), row(l0_ln2_b),
              combined=True, row_off=0)

    gate, xr = _in1(xc, mod1, l1_w_in.astype(BF16))
    wbd = jnp.concatenate([l1_lru_ga_w[0], l1_lru_gx_w[0], l1_lru_ga_w[1], l1_lru_gx_w[1]], axis=-1).astype(BF16)
    gb = jnp.stack([l1_lru_ga_b[0], l1_lru_gx_b[0], l1_lru_ga_b[1], l1_lru_gx_b[1]])
    hf, hr = _lru(xr, l1_conv_w, row(l1_conv_b), wbd, gb, l1_lru_lambda)
    xl = _out1(gate, hf, hr, xc, mod1, l1_w_out.astype(BF16), row(l1_ln1_g), row(l1_ln1_b))
    return _mlp(xl, mod1, l1_mlp_w1.astype(BF16), l1_mlp_w2.astype(BF16), row(l1_ln2_g), row(l1_ln2_b),
                combined=False, row_off=0)
```

```python
import functools
import math

import jax
import jax.numpy as jnp
from jax import lax
from jax.experimental import pallas as pl
from jax.experimental.pallas import tpu as pltpu

F32 = jnp.float32
BF16 = jnp.bfloat16

D = 1024
DEPTH = 2
N_MOD = 6
ALPHA = (2.0 * DEPTH) ** 0.25
LN_EPS = 1e-5
RMS_EPS = 1e-6

HEADS = 8
MLA_NOPE = 64
MLA_ROPE = 32
MLA_V = 64
MLA_Q_RANK = 384
MLA_KV_RANK = 256
ROPE_AXIS = MLA_ROPE // 2
ROPE_THETA = 10000.0
GRID_W = 64
ATT_SCALE = (MLA_NOPE + MLA_ROPE) ** -0.5
HEAD_PAD = 128
ATT_HEADS = 8

RW_HEAD = 64
RW_W = HEADS * RW_HEAD
LORA_W = 64
LORA_A = 64
LORA_G = 128
RW_IN = 3 * RW_W + 2 * LORA_W + 2 * LORA_A + LORA_G
GN_EPS = 64e-5
EXP_NEG_HALF = math.exp(-0.5)

LRU_W = D
LRU_BLOCKS = 8
LRU_BLOCK = LRU_W // LRU_BLOCKS
LRU_C = 8.0
D_FF = 4 * D

TM = 256
SUB = 8
SCAN_TC = 72
SCAN_TC_SMALL = 32
RELAY_T = 128
RELAY_FB = 4
MLP_CHUNK = 1024
PITCH_F = RW_HEAD + SUB
PITCH_H = LRU_BLOCKS * SUB + SUB
LRU_TL = 32
VMEM_LIMIT = 56 * 1024 * 1024

IN0_COLS = MLA_Q_RANK + MLA_KV_RANK + 2 * HEAD_PAD + RW_IN


def _params(sem):
    return pltpu.CompilerParams(dimension_semantics=sem, vmem_limit_bytes=VMEM_LIMIT)


def _bdot(a, w):
    return jnp.dot(a.astype(BF16), w, preferred_element_type=F32)


def _sigmoid(x):
    return 1.0 / (1.0 + jnp.exp(-x))


def _layer_norm(z, g, b):
    mu = jnp.mean(z, axis=-1, keepdims=True)
    zc = z - mu
    var = jnp.mean(zc * zc, axis=-1, keepdims=True)
    return zc * lax.rsqrt(var + LN_EPS) * g + b


def _head_sum(x, ones_bd):
    hi = x.astype(BF16)
    lo = (x - hi.astype(F32)).astype(BF16)
    return (jnp.dot(hi, ones_bd, preferred_element_type=F32)
            + jnp.dot(lo, ones_bd, preferred_element_type=F32))


def _const_spec(shape):
    nd = len(shape)
    return pl.BlockSpec(shape, lambda *_: (0,) * nd)


def _weight_spec(shape):
    nd = len(shape)
    return pl.BlockSpec(shape, lambda *_: (0,) * nd, pipeline_mode=pl.Buffered(1))


def _mod_kernel(c_ref, w_ref, b_ref, o_ref):
    c = c_ref[...]
    s = c * _sigmoid(c)
    o_ref[...] = jnp.dot(s, w_ref[...], precision=lax.Precision.HIGHEST,
                         preferred_element_type=F32) + b_ref[...]


def _mod_table(c, c_ctx, mod_w, mod_b):
    b = c.shape[0]
    rows = 16
    cc = jnp.zeros((rows, D), F32).at[:b].set(c).at[b].set(c_ctx)
    tn = 1024
    out = pl.pallas_call(
        _mod_kernel,
        grid=(N_MOD * D // tn,),
        in_specs=[pl.BlockSpec((rows, D), lambda j: (0, 0)),
                  pl.BlockSpec((D, tn), lambda j: (0, j)),
                  pl.BlockSpec((1, tn), lambda j: (0, j))],
        out_specs=pl.BlockSpec((rows, tn), lambda j: (0, j)),
        out_shape=jax.ShapeDtypeStruct((rows, N_MOD * D), F32),
        compiler_params=_params(("arbitrary",)),
        name="mod_proj",
    )(cc, mod_w, mod_b.reshape(1, -1))
    lat = out[:b].reshape(b, N_MOD, D)
    ctx = jnp.broadcast_to(out[b].reshape(1, N_MOD, D), (b, N_MOD, D))
    tab = jnp.stack([ctx, lat], axis=1)
    return jnp.pad(tab, ((0, 0), (0, 0), (0, SUB - N_MOD), (0, 0)))


def _mod_spec(combined):
    if combined:
        return pl.BlockSpec((1, 1, SUB, D), lambda b, i: (b, jnp.minimum(i, 1), 0, 0))
    return pl.BlockSpec((1, 1, SUB, D), lambda b, i: (b, 1, 0, 0))


def _front0_kernel(x_ref, xp_ref, xn_ref, mod_ref, win_ref, qn_ref, kvn_ref, wqa_ref, wqb_ref, wuk_ref, wuv_ref,
                   cs_ref, sn_ref, mu_ref, w0_ref, w2_ref, a0_ref, a2_ref, g2_ref, kk_ref, ka_ref, rk_ref, ones_ref,
                   q_ref, k_ref, v_ref, sf_ref, g_ref, bonus_ref):
    i = pl.program_id(1)
    has_prev, has_next = _halo_flags(i, pl.num_programs(1))
    m = mod_ref[0, 0]
    xe = jnp.concatenate([x_ref[0], xp_ref[0], xn_ref[0]], axis=0)
    h = xe * (1.0 + m[1:2]) + m[0:1]
    fe = _bdot(h, win_ref[...])
    f = fe[:TM]
    o_kv = MLA_Q_RANK
    o_kr = o_kv + MLA_KV_RANK
    o_rw = o_kr + 2 * HEAD_PAD
    prow = jnp.where(has_prev, fe[TM + SUB - 1:TM + SUB, o_rw:], 0.0)
    nrow = jnp.where(has_next, fe[TM + SUB:TM + SUB + 1, o_rw:], 0.0)
    _rw_features(f[:, o_rw:], prow, nrow, mu_ref, w0_ref, w2_ref, a0_ref, a2_ref, g2_ref, kk_ref, ka_ref, rk_ref,
                 ones_ref, sf_ref, g_ref, bonus_ref)
    fq = f[:, :o_kv]
    fkv = f[:, o_kv:o_kr]
    fkr = f[:, o_kr:o_kr + HEAD_PAD]
    fkr_rot = f[:, o_kr + HEAD_PAD:o_rw]
    qn = fq * lax.rsqrt(jnp.mean(fq * fq, axis=-1, keepdims=True) + RMS_EPS) * qn_ref[...]
    ckv = fkv * lax.rsqrt(jnp.mean(fkv * fkv, axis=-1, keepdims=True) + RMS_EPS) * kvn_ref[...]
    qn = qn.astype(BF16)
    ckv = ckv.astype(BF16)
    qa = jnp.dot(qn, wqa_ref[...], preferred_element_type=F32)
    qb = jnp.dot(qn, wqb_ref[...], preferred_element_type=F32)
    kn = jnp.dot(ckv, wuk_ref[...], preferred_element_type=F32)
    cs = cs_ref[...]
    sn = sn_ref[...]
    kr = fkr * cs + fkr_rot * sn
    for hd in range(HEADS):
        sl = slice(hd * HEAD_PAD, (hd + 1) * HEAD_PAD)
        q_ref[0, :, sl] = ((qa[:, sl] * cs + qb[:, sl] * sn) * ATT_SCALE).astype(BF16)
        k_ref[0, :, sl] = (kn[:, sl] + kr).astype(BF16)
    v_ref[0] = jnp.dot(ckv, wuv_ref[...], preferred_element_type=F32).astype(BF16)


def _front0(xc, modt, mla_consts, cs, sn, rw_consts):
    b, t, _ = xc.shape
    row = lambda w: pl.BlockSpec((1, TM, w), lambda bb, i: (bb, i, 0))
    prev, nxt = _halo_specs(D, t)
    rope = pl.BlockSpec((TM, HEAD_PAD), lambda bb, i: (i, 0))
    return pl.pallas_call(
        _front0_kernel,
        grid=(b, t // TM),
        in_specs=[row(D), prev, nxt, _mod_spec(True)] + [_weight_spec(c.shape) for c in mla_consts]
                 + [rope, rope] + [_weight_spec(c.shape) for c in rw_consts],
        out_specs=[row(HEADS * HEAD_PAD), row(HEADS * HEAD_PAD), row(HEADS * MLA_V),
                   pl.BlockSpec((1, 9, HEADS * PITCH_F, TM), lambda bb, i: (bb, 0, 0, i)), row(RW_W), row(RW_W)],
        out_shape=[jax.ShapeDtypeStruct((b, t, HEADS * HEAD_PAD), BF16),
                   jax.ShapeDtypeStruct((b, t, HEADS * HEAD_PAD), BF16),
                   jax.ShapeDtypeStruct((b, t, HEADS * MLA_V), BF16),
                   jax.ShapeDtypeStruct((b, 9, HEADS * PITCH_F, t), F32),
                   jax.ShapeDtypeStruct((b, t, RW_W), F32),
                   jax.ShapeDtypeStruct((b, t, RW_W), F32)],
        compiler_params=_params(("parallel", "parallel")),
        name="l0_front",
    )(xc, xc, xc, modt, *mla_consts, cs, sn, *rw_consts)


def _att_kernel(q_ref, k_ref, v_ref, o_ref, *, n_ctx, n_all):
    i = pl.program_id(2)

    def attend(nk):
        lane = lax.broadcasted_iota(jnp.int32, (nk, 2 * MLA_V), 1)
        for pair in range(ATT_HEADS // 2):
            v = v_ref[0, :nk, pair * 2 * MLA_V:(pair + 1) * 2 * MLA_V]
            acc = None
            for hh in range(2):
                hd = 2 * pair + hh
                sl = slice(hd * HEAD_PAD, (hd + 1) * HEAD_PAD)
                q = q_ref[0, :, sl]
                k = k_ref[0, :nk, sl]
                s = lax.dot_general(q, k, (((1,), (1,)), ((), ())), preferred_element_type=F32)
                p = jnp.exp(s - jnp.max(s, axis=-1, keepdims=True))
                l = jnp.sum(p, axis=-1, keepdims=True)
                keep = (lane < MLA_V) if hh == 0 else (lane >= MLA_V)
                vh = jnp.where(keep, v, jnp.zeros_like(v))
                o = jnp.dot(p.astype(BF16), vh, preferred_element_type=F32) / l
                acc = o if acc is None else acc + o
            o_ref[0, :, pair * 2 * MLA_V:(pair + 1) * 2 * MLA_V] = acc.astype(o_ref.dtype)

    @pl.when(i == 0)
    def _():
        attend(n_ctx)

    @pl.when(i > 0)
    def _():
        attend(n_all)


def _attention(q, k, v):
    b, t, _ = q.shape
    return pl.pallas_call(
        functools.partial(_att_kernel, n_ctx=TM, n_all=t),
        grid=(b, HEADS // ATT_HEADS, t // TM),
        in_specs=[pl.BlockSpec((1, TM, ATT_HEADS * HEAD_PAD), lambda bb, hp, i: (bb, i, hp)),
                  pl.BlockSpec((1, t, ATT_HEADS * HEAD_PAD), lambda bb, hp, i: (bb, 0, hp)),
                  pl.BlockSpec((1, t, ATT_HEADS * MLA_V), lambda bb, hp, i: (bb, 0, hp))],
        out_specs=pl.BlockSpec((1, TM, ATT_HEADS * MLA_V), lambda bb, hp, i: (bb, i, hp)),
        out_shape=jax.ShapeDtypeStruct((b, t, HEADS * MLA_V), BF16),
        compiler_params=_params(("parallel", "parallel", "parallel")),
        name="l0_attention",
    )(q, k, v)


def _halo_flags(i, n_tiles):
    return i >= 2, jnp.logical_and(i >= 1, i < n_tiles - 1)


def _rw_features(f, prow, nrow, mu_ref, w0_ref, w2_ref, a0_ref, a2_ref, g2_ref,
                 kk_ref, ka_ref, rk_ref, ones_ref, sf_ref, g_ref, bonus_ref):
    rid = lax.broadcasted_iota(jnp.int32, (TM, 1), 0)
    prev = jnp.where(rid == 0, prow, pltpu.roll(f, 1, axis=0))
    nxt = jnp.where(rid == TM - 1, nrow, pltpu.roll(f, TM - 1, axis=0))
    f = f + mu_ref[...] * (0.5 * (prev + nxt) - f)
    r = f[:, 0:RW_W]
    k = f[:, RW_W:2 * RW_W]
    v = f[:, 2 * RW_W:3 * RW_W]
    o = 3 * RW_W
    wl = f[:, o:o + 2 * LORA_W]
    al = f[:, o + 2 * LORA_W:o + 2 * LORA_W + 2 * LORA_A]
    gl = f[:, o + 2 * LORA_W + 2 * LORA_A:]
    w_raw = w0_ref[...] + _bdot(jnp.tanh(wl), w2_ref[...])
    decay = jnp.exp(-EXP_NEG_HALF * _sigmoid(w_raw))
    a = _sigmoid(a0_ref[...] + _bdot(al, a2_ref[...]))
    g_ref[0] = _bdot(_sigmoid(gl), g2_ref[...])
    ones_bd = ones_ref[...]
    kk = k * kk_ref[...]
    kk = kk * lax.rsqrt(_head_sum(kk * kk, ones_bd) + 1e-12)
    ka = ka_ref[...]
    kd = [k * (1.0 + (a[:, d * RW_W:(d + 1) * RW_W] - 1.0) * ka) for d in range(2)]
    bonus_ref[0] = _head_sum(r * (kd[0] + kd[1]) * rk_ref[...], ones_bd) * v
    parts = [r, kk, v, decay[:, :RW_W], kd[0], kk * a[:, :RW_W],
             decay[:, RW_W:], kd[1], kk * a[:, RW_W:]]
    pad = jnp.zeros((PITCH_F - RW_HEAD, TM), F32)
    for n, p in enumerate(parts):
        for hp in range(HEADS // 2):
            for half in range(TM // 128):
                pt = p[half * 128:(half + 1) * 128, hp * 128:(hp + 1) * 128].T
                for h2 in range(2):
                    row0 = (2 * hp + h2) * PITCH_F
                    sf_ref[0, n, row0:row0 + RW_HEAD, half * 128:(half + 1) * 128] = pt[h2 * RW_HEAD:(h2 + 1) * RW_HEAD]
        for hd in range(HEADS):
            sf_ref[0, n, hd * PITCH_F + RW_HEAD:(hd + 1) * PITCH_F, :] = pad


def _halo_specs(width, t):
    nb = TM // SUB
    last = t // SUB - 1
    prev = pl.BlockSpec((1, SUB, width), lambda bb, i: (bb, jnp.maximum(i * nb - 1, 0), 0))
    nxt = pl.BlockSpec((1, SUB, width), lambda bb, i: (bb, jnp.minimum((i + 1) * nb, last), 0))
    return prev, nxt


def _time_mirror(i, n_ctx_tiles, n_tiles):
    return jnp.where(i < n_ctx_tiles, n_ctx_tiles - 1 - i, n_tiles + n_ctx_tiles - 1 - i)


def _reverse_lanes(x, j3):
    hi = x.astype(BF16)
    r1 = x - hi.astype(F32)
    mid = r1.astype(BF16)
    lo = (r1 - mid.astype(F32)).astype(BF16)
    return jnp.dot(jnp.concatenate([hi, mid, lo], axis=1), j3, preferred_element_type=F32)


def _relayout_in_kernel(xa_ref, xb_ref, j3_ref, o_ref, *, step_major):
    nb = xa_ref.shape[0]
    tr = xa_ref.shape[3]
    half = nb * HEADS

    def gather(x_ref, f):
        return jnp.concatenate([x_ref[b, 0, pl.ds(f, HEADS, stride=PITCH_F), :] for b in range(nb)], axis=0)

    for f0 in range(0, RW_HEAD, RELAY_FB):
        fs = range(f0, f0 + RELAY_FB)
        rev = _reverse_lanes(jnp.concatenate([gather(xb_ref, f) for f in fs], axis=0), j3_ref[...])
        for n, f in enumerate(fs):
            mt = jnp.concatenate([gather(xa_ref, f), rev[n * half:(n + 1) * half]], axis=0).T
            if step_major:
                o_ref[pl.ds(f, tr, stride=PITCH_F), :] = mt
            else:
                o_ref[0, f] = mt
    if step_major:
        for f in range(RW_HEAD, PITCH_F):
            o_ref[pl.ds(f, tr, stride=PITCH_F), :] = jnp.zeros((tr, 2 * half), F32)


def _relayout_in(sf, j3):
    b, _, rows, t = sf.shape
    chains = 2 * b * HEADS
    nt = t // RELAY_T
    nc = TM // RELAY_T
    mirror = lambda i: _time_mirror(i, nc, nt)
    blk = (b, 1, rows, RELAY_T)
    dirdep = lambda p: jnp.minimum(p // 2, 1)
    x = pl.pallas_call(
        functools.partial(_relayout_in_kernel, step_major=False),
        grid=(nt, 5),
        in_specs=[pl.BlockSpec(blk, lambda i, p: (0, p + dirdep(p), 0, i)),
                  pl.BlockSpec(blk, lambda i, p: (0, p + 4 * dirdep(p), 0, mirror(i))),
                  _const_spec(j3.shape)],
        out_specs=pl.BlockSpec((1, RW_HEAD, RELAY_T, chains), lambda i, p: (p, 0, i, 0)),
        out_shape=jax.ShapeDtypeStruct((5, RW_HEAD, t, chains), F32),
        compiler_params=_params(("parallel", "parallel")),
        name="l0_rwkv_relayout_in",
    )(sf, sf, j3)
    v = pl.pallas_call(
        functools.partial(_relayout_in_kernel, step_major=True),
        grid=(nt,),
        in_specs=[pl.BlockSpec(blk, lambda i: (0, 2, 0, i)),
                  pl.BlockSpec(blk, lambda i: (0, 2, 0, mirror(i))),
                  _const_spec(j3.shape)],
        out_specs=pl.BlockSpec((RELAY_T * PITCH_F, chains), lambda i: (i, 0)),
        out_shape=jax.ShapeDtypeStruct((t * PITCH_F, chains), F32),
        compiler_params=_params(("parallel",)),
        name="l0_rwkv_relayout_v",
    )(sf, sf, j3)
    return x, v


def _rwscan_kernel(x_ref, v_ref, y_ref, s_ref):
    tc = x_ref.shape[2]
    chains = x_ref.shape[3]

    @pl.when(pl.program_id(0) == 0)
    def _():
        s_ref[...] = jnp.zeros_like(s_ref)

    groups = RW_HEAD // SUB
    zeros = tuple(jnp.zeros((SUB, chains), F32) for _ in range(groups))

    def row(a, j, s):
        return jnp.broadcast_to(x_ref[a, j, pl.ds(s, 1), :], (SUB, chains))

    def s_at(j, g):
        return s_ref.at[j, g * SUB:(g + 1) * SUB, :]

    sa0 = list(zeros)
    for j in range(RW_HEAD):
        kk = row(1, j, 0)
        for g in range(groups):
            sa0[g] = sa0[g] + s_at(j, g)[...] * kk

    def step(s, sa):
        base = pl.multiple_of(s * PITCH_F, SUB)
        v = v_ref[pl.ds(base, RW_HEAD), :]
        s_next = jnp.minimum(s + 1, tc - 1)
        y = list(zeros)
        sa_next = list(zeros)
        for j in range(RW_HEAD):
            r, kk_next, w, k, kka = row(0, j, s), row(1, j, s_next), row(2, j, s), row(3, j, s), row(4, j, s)
            for g in range(groups):
                sj = s_at(j, g)[...] * w - sa[g] * kka + v[g * SUB:(g + 1) * SUB] * k
                s_at(j, g)[...] = sj
                y[g] = y[g] + sj * r
                sa_next[g] = sa_next[g] + sj * kk_next
        y_ref[pl.ds(base, RW_HEAD), :] = jnp.concatenate(y, axis=0)
        y_ref[pl.ds(base + RW_HEAD, PITCH_F - RW_HEAD), :] = zeros[0]
        return tuple(sa_next)

    lax.fori_loop(0, tc, step, tuple(sa0))


def _rwscan(x, v):
    _, _, t, chains = x.shape
    tc = SCAN_TC if t % SCAN_TC == 0 else SCAN_TC_SMALL
    return pl.pallas_call(
        _rwscan_kernel,
        grid=(t // tc,),
        in_specs=[pl.BlockSpec((5, RW_HEAD, tc, chains), lambda i: (0, 0, i, 0)),
                  pl.BlockSpec((tc * PITCH_F, chains), lambda i: (i, 0))],
        out_specs=pl.BlockSpec((tc * PITCH_F, chains), lambda i: (i, 0)),
        out_shape=jax.ShapeDtypeStruct((t * PITCH_F, chains), F32),
        scratch_shapes=[pltpu.VMEM((RW_HEAD, RW_HEAD, chains), F32)],
        compiler_params=_params(("arbitrary",)),
        name="l0_rwkv_scan",
    )(x, v)


def _relayout_out_kernel(yf_ref, yr_ref, j3_ref, o_ref, q_ref):
    nb = o_ref.shape[0]
    tr = o_ref.shape[1]
    chains = yf_ref.shape[1]
    half = chains // 2
    pitch_c = chains + SUB
    fwd = lax.broadcasted_iota(jnp.int32, (tr, chains), 1) < half
    for i in range(RW_HEAD):
        rows = pl.ds(i, tr, stride=PITCH_F)
        mt = jnp.where(fwd, yf_ref[rows, :], yr_ref[rows, :]).T
        q_ref[i * pitch_c:i * pitch_c + half, :] = mt[:half]
        q_ref[i * pitch_c + half:i * pitch_c + chains, :] = _reverse_lanes(mt[half:], j3_ref[...])
    for b in range(nb):
        for hp in range(HEADS // 2):
            parts = []
            for h2 in range(2):
                c = b * HEADS + 2 * hp + h2
                parts.append(q_ref[pl.ds(c, RW_HEAD, stride=pitch_c), :]
                             + q_ref[pl.ds(half + c, RW_HEAD, stride=pitch_c), :])
            o_ref[b, :, hp * 128:(hp + 1) * 128] = jnp.concatenate(parts, axis=0).T


def _relayout_out(y, j3, b):
    chains = y.shape[1]
    t = y.shape[0] // PITCH_F
    nt = t // RELAY_T
    nc = TM // RELAY_T
    return pl.pallas_call(
        _relayout_out_kernel,
        grid=(nt,),
        in_specs=[pl.BlockSpec((RELAY_T * PITCH_F, chains), lambda i: (i, 0)),
                  pl.BlockSpec((RELAY_T * PITCH_F, chains), lambda i: (_time_mirror(i, nc, nt), 0)),
                  _const_spec(j3.shape)],
        out_specs=pl.BlockSpec((b, RELAY_T, RW_W), lambda i: (0, i, 0)),
        out_shape=jax.ShapeDtypeStruct((b, t, RW_W), F32),
        scratch_shapes=[pltpu.VMEM((RW_HEAD * (chains + SUB), RELAY_T), F32)],
        compiler_params=_params(("parallel",)),
        name="l0_rwkv_relayout_out",
    )(y, y, j3)


def _rwkv_scan_both(sf):
    k = jnp.arange(RELAY_T)
    anti = (k[:, None] + k[None, :] == RELAY_T - 1).astype(BF16)
    j3 = jnp.concatenate([anti, anti, anti], axis=0)
    y = _rwscan(*_relayout_in(sf, j3))
    return _relayout_out(y, j3, sf.shape[0])


def _mlp_hidden(h, w1_ref, w2_ref):
    acc = None
    for c in range(D_FF // MLP_CHUNK):
        u = jnp.dot(h, w1_ref[:, c * MLP_CHUNK:(c + 1) * MLP_CHUNK], preferred_element_type=F32)
        u = jnp.square(jnp.maximum(u, 0.0)).astype(BF16)
        part = jnp.dot(u, w2_ref[c * MLP_CHUNK:(c + 1) * MLP_CHUNK, :], preferred_element_type=F32)
        acc = part if acc is None else acc + part
    return acc


def _tail0_kernel(att_ref, ys_ref, bonus_ref, g_ref, x_ref, mod_ref, mod1_ref, gnw_ref, gnb_ref, ones_ref,
                  woa_ref, wob_ref, l1g_ref, l1b_ref, w1_ref, w2_ref, l2g_ref, l2b_ref, win1_ref,
                  x2_ref, gate_ref, xr_ref):
    m = mod_ref[0, 0]
    ones_bd = ones_ref[...]
    y = ys_ref[0]
    mu = _head_sum(y, ones_bd) * (1.0 / RW_HEAD)
    yc = y - mu
    var = _head_sum(yc * yc, ones_bd) * (1.0 / RW_HEAD)
    yn = yc * lax.rsqrt(var + GN_EPS) * gnw_ref[...] + gnb_ref[...]
    rw = (yn + bonus_ref[0]) * g_ref[0]
    o = (jnp.dot(att_ref[0], woa_ref[...], preferred_element_type=F32)
         + _bdot(rw, wob_ref[...]))
    x1 = _layer_norm(ALPHA * x_ref[0] + m[2:3] * o, l1g_ref[...], l1b_ref[...])
    acc = _mlp_hidden((x1 * (1.0 + m[4:5]) + m[3:4]).astype(BF16), w1_ref, w2_ref)
    x2 = _layer_norm(ALPHA * x1 + m[5:6] * acc, l2g_ref[...], l2b_ref[...])
    x2_ref[0] = x2
    m1 = mod1_ref[0, 0]
    f = _bdot(x2 * (1.0 + m1[1:2]) + m1[0:1], win1_ref[...])
    gate_ref[0] = f[:, :LRU_W]
    xr_ref[0] = f[:, LRU_W:]


def _tail0(att, ys, bonus, g, xc, mod0, mod1, consts):
    b, t, _ = xc.shape
    row = lambda w: pl.BlockSpec((1, TM, w), lambda bb, i: (bb, i, 0))
    return pl.pallas_call(
        _tail0_kernel,
        grid=(b, t // TM),
        in_specs=[row(RW_W), row(RW_W), row(RW_W), row(RW_W), row(D), _mod_spec(True), _mod_spec(True)]
                 + [_weight_spec(c.shape) for c in consts],
        out_specs=[row(D), row(LRU_W), row(LRU_W)],
        out_shape=[jax.ShapeDtypeStruct((b, t, D), F32), jax.ShapeDtypeStruct((b, t, LRU_W), F32),
                   jax.ShapeDtypeStruct((b, t, LRU_W), F32)],
        compiler_params=_params(("parallel", "parallel")),
        name="l0_tail",
    )(att, ys, bonus, g, xc, mod0, mod1, *consts)


def _lru_kernel(xf_ref, xfp_ref, xfn_ref, xr_ref, xrp_ref, xrn_ref, cw_ref, cb_ref, wbd_ref, gb_ref, lam_ref,
                hf_ref, hr_ref, a_scr, u_scr, cf_ref, cr_ref):
    i = pl.program_id(0)
    n = pl.num_programs(0)
    nb, tl, _ = xf_ref.shape
    rows = nb * tl
    nc = TM // tl
    pitch_b = tl + SUB

    @pl.when(i == 0)
    def _():
        cf_ref[...] = jnp.zeros_like(cf_ref)
        cr_ref[...] = jnp.zeros_like(cr_ref)

    lam = lam_ref[...]
    nl = -lam
    softplus = jnp.maximum(nl, 0.0) + jnp.log1p(jnp.exp(-jnp.abs(nl)))
    gb = gb_ref[...]
    cw = cw_ref[...]
    tpos = jnp.bitwise_and(lax.broadcasted_iota(jnp.int32, (rows, 1), 0), tl - 1)
    tile_r = _time_mirror(i, nc, n)
    dirs = ((xf_ref, xfp_ref, xfn_ref, i, hf_ref, cf_ref), (xr_ref, xrp_ref, xrn_ref, tile_r, hr_ref, cr_ref))
    for d, (x_ref, xp_ref, xn_ref, tile, h_ref, c_ref) in enumerate(dirs):
        seg_first = jnp.logical_or(tile == 0, tile == nc)
        seg_last = jnp.logical_or(tile == nc - 1, tile == n - 1)

        def halo(ref, r, edge):
            per_b = [jnp.broadcast_to(ref[b, r:r + 1, :], (tl, LRU_W)) for b in range(nb)]
            return jnp.where(edge, 0.0, jnp.concatenate(per_b, axis=0))

        x = x_ref[...].reshape(rows, LRU_W)
        p1 = halo(xp_ref, SUB - 1, seg_first)
        p2 = halo(xp_ref, SUB - 2, seg_first)
        n1 = halo(xn_ref, 0, seg_last)
        xm1 = jnp.where(tpos == 0, p1, pltpu.roll(x, 1, axis=0))
        xm2 = jnp.where(tpos == 0, p2, jnp.where(tpos == 1, p1, pltpu.roll(x, 2, axis=0)))
        xp1 = jnp.where(tpos == tl - 1, n1, pltpu.roll(x, rows - 1, axis=0))
        xc = cw[0:1] * xm2 + cw[1:2] * xm1 + cw[2:3] * x + cw[3:4] * xp1 + cb_ref[...]
        for blk in range(LRU_BLOCKS):
            sl = slice(blk * LRU_BLOCK, (blk + 1) * LRU_BLOCK)
            xb = xc[:, sl]
            z = _bdot(xb, wbd_ref[blk, :, 2 * d * LRU_BLOCK:(2 * d + 2) * LRU_BLOCK])
            zr = z[:, :LRU_BLOCK] + gb[2 * d:2 * d + 1, sl]
            zi = z[:, LRU_BLOCK:] + gb[2 * d + 1:2 * d + 2, sl]
            log_a = -LRU_C * _sigmoid(zr) * softplus[d:d + 1, sl]
            a = jnp.exp(log_a)
            u = jnp.sqrt(-jnp.tanh(log_a) * (a * a + 1.0)) * (_sigmoid(zi) * xb)
            for b in range(nb):
                a_scr[blk, b * pitch_b:b * pitch_b + tl, :] = a[b * tl:(b + 1) * tl]
                u_scr[blk, b * pitch_b:b * pitch_b + tl, :] = u[b * tl:(b + 1) * tl]
        h = [c_ref[:, blk * LRU_BLOCK:(blk + 1) * LRU_BLOCK] for blk in range(LRU_BLOCKS)]
        for step in range(tl):
            t = step if d == 0 else tl - 1 - step
            for blk in range(LRU_BLOCKS):
                at = a_scr[blk, pl.ds(t, nb, stride=pitch_b), :]
                ut = u_scr[blk, pl.ds(t, nb, stride=pitch_b), :]
                h[blk] = at * h[blk] + ut
                h_ref[t * PITCH_H + blk * SUB:t * PITCH_H + blk * SUB + nb, :] = h[blk]
            h_ref[t * PITCH_H + LRU_BLOCKS * SUB:(t + 1) * PITCH_H, :] = jnp.zeros((SUB, LRU_BLOCK), F32)
        for blk in range(LRU_BLOCKS):
            c_ref[:, blk * LRU_BLOCK:(blk + 1) * LRU_BLOCK] = h[blk]


def _lru(xr, cw, cb, wbd, gb, lam):
    b, t, _ = xr.shape
    assert b <= SUB
    n = t // LRU_TL
    nc = TM // LRU_TL
    hb = LRU_TL // SUB
    last = t // SUB - 1
    mirror = lambda i: _time_mirror(i, nc, n)

    def specs(tile):
        return [pl.BlockSpec((b, LRU_TL, LRU_W), lambda i: (0, tile(i), 0)),
                pl.BlockSpec((b, SUB, LRU_W), lambda i: (0, jnp.maximum(tile(i) * hb - 1, 0), 0)),
                pl.BlockSpec((b, SUB, LRU_W), lambda i: (0, jnp.minimum((tile(i) + 1) * hb, last), 0))]

    consts = (cw, cb, wbd, gb, lam)
    hspec = lambda tile: pl.BlockSpec((LRU_TL * PITCH_H, LRU_BLOCK), lambda i: (tile(i), 0))
    return pl.pallas_call(
        _lru_kernel,
        grid=(n,),
        in_specs=specs(lambda i: i) + specs(mirror) + [_const_spec(c.shape) for c in consts],
        out_specs=[hspec(lambda i: i), hspec(mirror)],
        out_shape=[jax.ShapeDtypeStruct((t * PITCH_H, LRU_BLOCK), F32)] * 2,
        scratch_shapes=[pltpu.VMEM((LRU_BLOCKS, b * (LRU_TL + SUB), LRU_BLOCK), F32),
                        pltpu.VMEM((LRU_BLOCKS, b * (LRU_TL + SUB), LRU_BLOCK), F32),
                        pltpu.VMEM((b, LRU_W), F32), pltpu.VMEM((b, LRU_W), F32)],
        compiler_params=_params(("arbitrary",)),
        name="l1_lru",
    )(xr, xr, xr, xr, xr, xr, *consts)


def _tail1_kernel(gate_ref, hf_ref, hr_ref, x_ref, mod_ref, w_ref, l1g_ref, l1b_ref, w1_ref, w2_ref, l2g_ref, l2b_ref,
                  o_ref):
    nb, tl, _ = gate_ref.shape
    per_b = []
    for b in range(nb):
        cols = []
        for blk in range(LRU_BLOCKS):
            rows = pl.ds(blk * SUB + b, tl, stride=PITCH_H)
            cols.append(hf_ref[rows, :] + hr_ref[rows, :])
        per_b.append(jnp.concatenate(cols, axis=1))
    h = jnp.concatenate(per_b, axis=0)
    gate = gate_ref[...].reshape(nb * tl, LRU_W)
    gelu = 0.5 * gate * (1.0 + jnp.tanh(math.sqrt(2.0 / math.pi) * (gate + 0.044715 * gate * gate * gate)))
    o = _bdot(gelu * h, w_ref[...])
    mods = [mod_ref[b, 0] for b in range(nb)]
    x1 = [_layer_norm(ALPHA * x_ref[b] + mods[b][2:3] * o[b * tl:(b + 1) * tl], l1g_ref[...], l1b_ref[...])
          for b in range(nb)]
    hm = jnp.concatenate([x1[b] * (1.0 + mods[b][4:5]) + mods[b][3:4] for b in range(nb)], axis=0)
    acc = _mlp_hidden(hm.astype(BF16), w1_ref, w2_ref)
    for b in range(nb):
        z = ALPHA * x1[b] + mods[b][5:6] * acc[b * tl:(b + 1) * tl]
        o_ref[b] = _layer_norm(z, l2g_ref[...], l2b_ref[...])


def _tail1(gate, hf, hr, xc, modt, consts):
    b, t, _ = xc.shape
    off = TM // LRU_TL
    nt = t // LRU_TL - off
    lat = lambda wd: pl.BlockSpec((b, LRU_TL, wd), lambda i: (0, i + off, 0))
    hspec = pl.BlockSpec((LRU_TL * PITCH_H, LRU_BLOCK), lambda i: (i + off, 0))
    return pl.pallas_call(
        _tail1_kernel,
        grid=(nt,),
        in_specs=[lat(LRU_W), hspec, hspec, lat(D),
                  pl.BlockSpec((b, 1, SUB, D), lambda i: (0, 1, 0, 0))]
                 + [_weight_spec(c.shape) for c in consts],
        out_specs=pl.BlockSpec((b, LRU_TL, D), lambda i: (0, i, 0)),
        out_shape=jax.ShapeDtypeStruct((b, nt * LRU_TL, D), F32),
        compiler_params=_params(("parallel",)),
        name="l1_tail",
    )(gate, hf, hr, xc, modt, *consts)


def _rot_cols(w):
    ws = w.reshape(w.shape[:-1] + (2, 2, ROPE_AXIS // 2))
    return jnp.stack([-ws[..., 1, :], ws[..., 0, :]], axis=-2).reshape(w.shape)


def _rope_tables(n, n_ctx):
    rows_n = n // GRID_W
    rows = jnp.repeat(jnp.arange(rows_n, dtype=F32), GRID_W)
    cols = jnp.tile(jnp.arange(GRID_W, dtype=F32), rows_n)
    inv_freq = ROPE_THETA ** (-jnp.arange(0, ROPE_AXIS, 2, dtype=F32) / ROPE_AXIS)
    ang_r = rows[:, None] * inv_freq
    ang_c = cols[:, None] * inv_freq
    ang = jnp.concatenate([ang_r, ang_r, ang_c, ang_c], axis=-1)
    cos = jnp.concatenate([jnp.ones((n_ctx, MLA_ROPE), F32), jnp.cos(ang)], axis=0)
    sin = jnp.concatenate([jnp.zeros((n_ctx, MLA_ROPE), F32), jnp.sin(ang)], axis=0)
    t = n + n_ctx
    cs = jnp.concatenate([jnp.ones((t, MLA_NOPE), F32), cos, jnp.zeros((t, 32), F32)], axis=-1)
    sn = jnp.concatenate([jnp.zeros((t, MLA_NOPE), F32), sin, jnp.zeros((t, 32), F32)], axis=-1)
    return cs, sn


def _block_diag2(w):
    z = jnp.zeros_like(w[0])
    return jnp.concatenate([jnp.concatenate([w[0], z], axis=1), jnp.concatenate([z, w[1]], axis=1)], axis=0)


def kernel(x, c, ctx, c_ctx, l0_mod_w, l0_mod_b, l0_w_in, l0_mla_q_norm, l0_mla_w_uq, l0_mla_kv_norm, l0_mla_w_uk, l0_mla_w_uv, l0_rwkv_mu, l0_rwkv_w0, l0_rwkv_w2, l0_rwkv_a0, l0_rwkv_a2, l0_rwkv_g2, l0_rwkv_k_k, l0_rwkv_k_a, l0_rwkv_r_k, l0_rwkv_gn_w, l0_rwkv_gn_b, l0_w_out, l0_ln1_g, l0_ln1_b, l0_mlp_w1, l0_mlp_w2, l0_ln2_g, l0_ln2_b, l1_mod_w, l1_mod_b, l1_w_in, l1_conv_w, l1_conv_b, l1_lru_ga_w, l1_lru_ga_b, l1_lru_gx_w, l1_lru_gx_b, l1_lru_lambda, l1_w_out, l1_ln1_g, l1_ln1_b, l1_mlp_w1, l1_mlp_w2, l1_ln2_g, l1_ln2_b):
    b, n, _ = x.shape
    n_ctx = ctx.shape[1]
    assert n_ctx == TM and n % TM == 0 and x.shape[2] == D
    row = lambda v: v.reshape(1, -1)

    xc = jnp.concatenate([ctx, x], axis=1)
    mod0 = _mod_table(c, c_ctx, l0_mod_w, l0_mod_b)
    mod1 = _mod_table(c, c_ctx, l1_mod_w, l1_mod_b)

    o_kv = MLA_Q_RANK
    o_kr = o_kv + MLA_KV_RANK
    o_rw = o_kr + MLA_ROPE
    w_kr = l0_w_in[:, o_kr:o_rw]
    zl = jnp.zeros((D, MLA_NOPE), F32)
    zr = jnp.zeros((D, HEAD_PAD - MLA_NOPE - MLA_ROPE), F32)
    win0 = jnp.concatenate([l0_w_in[:, :o_kr], zl, w_kr, zr, zl, _rot_cols(w_kr), zr,
                            l0_w_in[:, o_rw:]], axis=1).astype(BF16)
    wq = l0_mla_w_uq.reshape(MLA_Q_RANK, HEADS, MLA_NOPE + MLA_ROPE)
    q_nope, q_rope = wq[..., :MLA_NOPE], wq[..., MLA_NOPE:]
    zq = jnp.zeros((MLA_Q_RANK, HEADS, 32), F32)
    wqa = jnp.concatenate([q_nope, q_rope, zq], axis=-1).reshape(MLA_Q_RANK, HEADS * HEAD_PAD).astype(BF16)
    wqb = jnp.concatenate([jnp.zeros_like(q_nope), _rot_cols(q_rope), zq],
                          axis=-1).reshape(MLA_Q_RANK, HEADS * HEAD_PAD).astype(BF16)
    wk = l0_mla_w_uk.reshape(MLA_KV_RANK, HEADS, MLA_NOPE)
    wuk = jnp.concatenate([wk, jnp.zeros_like(wk)], axis=-1).reshape(MLA_KV_RANK, HEADS * HEAD_PAD).astype(BF16)
    wuv = l0_mla_w_uv.astype(BF16)
    cs, sn = _rope_tables(n, n_ctx)
    hid = jnp.arange(RW_W) // RW_HEAD
    ones_bd = (hid[:, None] == hid[None, :]).astype(BF16)

    mla_consts = (win0, row(l0_mla_q_norm), row(l0_mla_kv_norm), wqa, wqb, wuk, wuv)
    rw_consts = (row(l0_rwkv_mu), row(l0_rwkv_w0), _block_diag2(l0_rwkv_w2).astype(BF16), row(l0_rwkv_a0),
                 _block_diag2(l0_rwkv_a2).astype(BF16), l0_rwkv_g2.astype(BF16), row(l0_rwkv_k_k),
                 row(l0_rwkv_k_a), row(l0_rwkv_r_k), ones_bd)
    q, k, v, sf, g, bonus = _front0(xc, mod0, mla_consts, cs, sn, rw_consts)
    att = _attention(q, k, v)
    ys = _rwkv_scan_both(sf)
    wo = l0_w_out.astype(BF16)
    tail0_consts = (row(l0_rwkv_gn_w), row(l0_rwkv_gn_b), ones_bd, wo[:HEADS * MLA_V], wo[HEADS * MLA_V:],
                    row(l0_ln1_g), row(l0_ln1_b), l0_mlp_w1.astype(BF16), l0_mlp_w2.astype(BF16),
                    row(l0_ln2_g), row(l0_ln2_b), l1_w_in.astype(BF16))
    xc, gate, xr = _tail0(att, ys, bonus, g, xc, mod0, mod1, tail0_consts)

    wbd = jnp.concatenate([l1_lru_ga_w[0], l1_lru_gx_w[0], l1_lru_ga_w[1], l1_lru_gx_w[1]], axis=-1).astype(BF16)
    gb = jnp.stack([l1_lru_ga_b[0], l1_lru_gx_b[0], l1_lru_ga_b[1], l1_lru_gx_b[1]])
    hf, hr = _lru(xr, l1_conv_w, row(l1_conv_b), wbd, gb, l1_lru_lambda)
    tail1_consts = (l1_w_out.astype(BF16), row(l1_ln1_g), row(l1_ln1_b), l1_mlp_w1.astype(BF16),
                    l1_mlp_w2.astype(BF16), row(l1_ln2_g), row(l1_ln2_b))
    return _tail1(gate, hf, hr, xc, mod1, tail1_consts)
```

```python
import functools
import math

import jax
import jax.numpy as jnp
from jax import lax
from jax.experimental import pallas as pl
from jax.experimental.pallas import tpu as pltpu

F32 = jnp.float32
BF16 = jnp.bfloat16

D = 1024
DEPTH = 2
N_MOD = 6
ALPHA = (2.0 * DEPTH) ** 0.25
LN_EPS = 1e-5
RMS_EPS = 1e-6

HEADS = 8
MLA_NOPE = 64
MLA_ROPE = 32
MLA_V = 64
MLA_Q_RANK = 384
MLA_KV_RANK = 256
ROPE_AXIS = MLA_ROPE // 2
ROPE_THETA = 10000.0
GRID_W = 64
ATT_SCALE = (MLA_NOPE + MLA_ROPE) ** -0.5
LOG2_E = math.log2(math.e)
HEAD_PAD = 128
ATT_HEADS = 8

RW_HEAD = 64
RW_W = HEADS * RW_HEAD
LORA_W = 64
LORA_A = 64
LORA_G = 128
RW_IN = 3 * RW_W + 2 * LORA_W + 2 * LORA_A + LORA_G
GN_EPS = 64e-5
EXP_NEG_HALF = math.exp(-0.5)

LRU_W = D
LRU_BLOCKS = 8
LRU_BLOCK = LRU_W // LRU_BLOCKS
LRU_C = 8.0
D_FF = 4 * D

TM = 256
SUB = 8
SCAN_TC = 72
SCAN_TC_SMALL = 32
RELAY_T = 128
RELAY_FB = 4
MLP_CHUNK = 1024
PITCH_F = RW_HEAD + SUB
PITCH_H = LRU_BLOCKS * SUB + SUB
LRU_TL = 32
LRU_TB = 128
VMEM_LIMIT = 56 * 1024 * 1024

IN0_COLS = MLA_Q_RANK + MLA_KV_RANK + 2 * HEAD_PAD + RW_IN


def _params(sem):
    return pltpu.CompilerParams(dimension_semantics=sem, vmem_limit_bytes=VMEM_LIMIT)


def _bdot(a, w):
    return jnp.dot(a.astype(BF16), w, preferred_element_type=F32)


def _sigmoid(x):
    return 0.5 * jnp.tanh(0.5 * x) + 0.5


def _layer_norm(z, g, b):
    mu = jnp.mean(z, axis=-1, keepdims=True)
    zc = z - mu
    var = jnp.mean(zc * zc, axis=-1, keepdims=True)
    return zc * lax.rsqrt(var + LN_EPS) * g + b


def _head_sum(x, ones_bd):
    hi = x.astype(BF16)
    lo = (x - hi.astype(F32)).astype(BF16)
    return (jnp.dot(hi, ones_bd, preferred_element_type=F32)
            + jnp.dot(lo, ones_bd, preferred_element_type=F32))


def _const_spec(shape):
    nd = len(shape)
    return pl.BlockSpec(shape, lambda *_: (0,) * nd)


def _weight_spec(shape):
    nd = len(shape)
    return pl.BlockSpec(shape, lambda *_: (0,) * nd, pipeline_mode=pl.Buffered(1))


def _mod_kernel(c_ref, w_ref, b_ref, o_ref):
    c = c_ref[...]
    s = c * _sigmoid(c)
    o_ref[...] = jnp.dot(s, w_ref[...], precision=lax.Precision.HIGHEST,
                         preferred_element_type=F32) + b_ref[...]


def _mod_table(c, c_ctx, mod_w, mod_b):
    b = c.shape[0]
    rows = 16
    cc = jnp.zeros((rows, D), F32).at[:b].set(c).at[b].set(c_ctx)
    tn = 1024
    out = pl.pallas_call(
        _mod_kernel,
        grid=(N_MOD * D // tn,),
        in_specs=[pl.BlockSpec((rows, D), lambda j: (0, 0)),
                  pl.BlockSpec((D, tn), lambda j: (0, j)),
                  pl.BlockSpec((1, tn), lambda j: (0, j))],
        out_specs=pl.BlockSpec((rows, tn), lambda j: (0, j)),
        out_shape=jax.ShapeDtypeStruct((rows, N_MOD * D), F32),
        compiler_params=_params(("arbitrary",)),
        name="mod_proj",
    )(cc, mod_w, mod_b.reshape(1, -1))
    lat = out[:b].reshape(b, N_MOD, D)
    ctx = jnp.broadcast_to(out[b].reshape(1, N_MOD, D), (b, N_MOD, D))
    tab = jnp.stack([ctx, lat], axis=1)
    return jnp.pad(tab, ((0, 0), (0, 0), (0, SUB - N_MOD), (0, 0)))


def _mod_spec(combined):
    if combined:
        return pl.BlockSpec((1, 1, SUB, D), lambda b, i: (b, jnp.minimum(i, 1), 0, 0))
    return pl.BlockSpec((1, 1, SUB, D), lambda b, i: (b, 1, 0, 0))


def _front0_kernel(x_ref, xp_ref, xn_ref, mod_ref, win_ref, qn_ref, kvn_ref, wqa_ref, wqb_ref, wuk_ref, wuv_ref,
                   cs_ref, sn_ref, mu_ref, w0_ref, w2_ref, a0_ref, a2_ref, g2_ref, kk_ref, ka_ref, rk_ref, ones_ref,
                   q_ref, k_ref, v_ref, sf_ref, g_ref, bonus_ref):
    i = pl.program_id(1)
    has_prev, has_next = _halo_flags(i, pl.num_programs(1))
    m = mod_ref[0, 0]
    xe = jnp.concatenate([x_ref[0], xp_ref[0], xn_ref[0]], axis=0)
    h = xe * (1.0 + m[1:2]) + m[0:1]
    fe = _bdot(h, win_ref[...])
    f = fe[:TM]
    o_kv = MLA_Q_RANK
    o_kr = o_kv + MLA_KV_RANK
    o_rw = o_kr + 2 * HEAD_PAD
    prow = jnp.where(has_prev, fe[TM + SUB - 1:TM + SUB, o_rw:], 0.0)
    nrow = jnp.where(has_next, fe[TM + SUB:TM + SUB + 1, o_rw:], 0.0)
    _rw_features(f[:, o_rw:], prow, nrow, mu_ref, w0_ref, w2_ref, a0_ref, a2_ref, g2_ref, kk_ref, ka_ref, rk_ref,
                 ones_ref, sf_ref, g_ref, bonus_ref)
    fq = f[:, :o_kv]
    fkv = f[:, o_kv:o_kr]
    fkr = f[:, o_kr:o_kr + HEAD_PAD]
    fkr_rot = f[:, o_kr + HEAD_PAD:o_rw]
    qn = fq * lax.rsqrt(jnp.mean(fq * fq, axis=-1, keepdims=True) + RMS_EPS) * qn_ref[...]
    ckv = fkv * lax.rsqrt(jnp.mean(fkv * fkv, axis=-1, keepdims=True) + RMS_EPS) * kvn_ref[...]
    qn = qn.astype(BF16)
    ckv = ckv.astype(BF16)
    qa = jnp.dot(qn, wqa_ref[...], preferred_element_type=F32)
    qb = jnp.dot(qn, wqb_ref[...], preferred_element_type=F32)
    kn = jnp.dot(ckv, wuk_ref[...], preferred_element_type=F32)
    cs = cs_ref[...]
    sn = sn_ref[...]
    kr = fkr * cs + fkr_rot * sn
    for hd in range(HEADS):
        sl = slice(hd * HEAD_PAD, (hd + 1) * HEAD_PAD)
        q_ref[0, :, sl] = ((qa[:, sl] * cs + qb[:, sl] * sn) * (ATT_SCALE * LOG2_E)).astype(BF16)
        k_ref[0, :, sl] = (kn[:, sl] + kr).astype(BF16)
    v_ref[0] = jnp.dot(ckv, wuv_ref[...], preferred_element_type=F32).astype(BF16)


def _front0(xc, modt, mla_consts, cs, sn, rw_consts):
    b, t, _ = xc.shape
    row = lambda w: pl.BlockSpec((1, TM, w), lambda bb, i: (bb, i, 0))
    prev, nxt = _halo_specs(D, t)
    rope = pl.BlockSpec((TM, HEAD_PAD), lambda bb, i: (i, 0))
    return pl.pallas_call(
        _front0_kernel,
        grid=(b, t // TM),
        in_specs=[row(D), prev, nxt, _mod_spec(True)] + [_weight_spec(c.shape) for c in mla_consts]
                 + [rope, rope] + [_weight_spec(c.shape) for c in rw_consts],
        out_specs=[row(HEADS * HEAD_PAD), row(HEADS * HEAD_PAD), row(HEADS * MLA_V),
                   pl.BlockSpec((1, 9, HEADS * PITCH_F, TM), lambda bb, i: (bb, 0, 0, i)), row(RW_W), row(RW_W)],
        out_shape=[jax.ShapeDtypeStruct((b, t, HEADS * HEAD_PAD), BF16),
                   jax.ShapeDtypeStruct((b, t, HEADS * HEAD_PAD), BF16),
                   jax.ShapeDtypeStruct((b, t, HEADS * MLA_V), BF16),
                   jax.ShapeDtypeStruct((b, 9, HEADS * PITCH_F, t), F32),
                   jax.ShapeDtypeStruct((b, t, RW_W), F32),
                   jax.ShapeDtypeStruct((b, t, RW_W), F32)],
        compiler_params=_params(("parallel", "parallel")),
        name="l0_front",
    )(xc, xc, xc, modt, *mla_consts, cs, sn, *rw_consts)


def _att_kernel(q_ref, k_ref, v_ref, o_ref, *, n_ctx, n_all):
    i = pl.program_id(2)

    def attend(nk):
        lane = lax.broadcasted_iota(jnp.int32, (nk, 2 * MLA_V), 1)
        for pair in range(ATT_HEADS // 2):
            v = v_ref[0, :nk, pair * 2 * MLA_V:(pair + 1) * 2 * MLA_V]
            acc = None
            for hh in range(2):
                hd = 2 * pair + hh
                sl = slice(hd * HEAD_PAD, (hd + 1) * HEAD_PAD)
                q = q_ref[0, :, sl]
                k = k_ref[0, :nk, sl]
                s = lax.dot_general(q, k, (((1,), (1,)), ((), ())), preferred_element_type=F32)
                p = jnp.exp2(s - jnp.max(s, axis=-1, keepdims=True))
                l = jnp.sum(p, axis=-1, keepdims=True)
                keep = (lane < MLA_V) if hh == 0 else (lane >= MLA_V)
                vh = jnp.where(keep, v, jnp.zeros_like(v))
                o = jnp.dot(p.astype(BF16), vh, preferred_element_type=F32) / l
                acc = o if acc is None else acc + o
            o_ref[0, :, pair * 2 * MLA_V:(pair + 1) * 2 * MLA_V] = acc.astype(o_ref.dtype)

    @pl.when(i == 0)
    def _():
        attend(n_ctx)

    @pl.when(i > 0)
    def _():
        attend(n_all)


def _attention(q, k, v):
    b, t, _ = q.shape
    return pl.pallas_call(
        functools.partial(_att_kernel, n_ctx=TM, n_all=t),
        grid=(b, HEADS // ATT_HEADS, t // TM),
        in_specs=[pl.BlockSpec((1, TM, ATT_HEADS * HEAD_PAD), lambda bb, hp, i: (bb, i, hp)),
                  pl.BlockSpec((1, t, ATT_HEADS * HEAD_PAD), lambda bb, hp, i: (bb, 0, hp)),
                  pl.BlockSpec((1, t, ATT_HEADS * MLA_V), lambda bb, hp, i: (bb, 0, hp))],
        out_specs=pl.BlockSpec((1, TM, ATT_HEADS * MLA_V), lambda bb, hp, i: (bb, i, hp)),
        out_shape=jax.ShapeDtypeStruct((b, t, HEADS * MLA_V), BF16),
        compiler_params=_params(("parallel", "parallel", "parallel")),
        name="l0_attention",
    )(q, k, v)


def _halo_flags(i, n_tiles):
    return i >= 2, jnp.logical_and(i >= 1, i < n_tiles - 1)


def _rw_features(f, prow, nrow, mu_ref, w0_ref, w2_ref, a0_ref, a2_ref, g2_ref,
                 kk_ref, ka_ref, rk_ref, ones_ref, sf_ref, g_ref, bonus_ref):
    rid = lax.broadcasted_iota(jnp.int32, (TM, 1), 0)
    prev = jnp.where(rid == 0, prow, pltpu.roll(f, 1, axis=0))
    nxt = jnp.where(rid == TM - 1, nrow, pltpu.roll(f, TM - 1, axis=0))
    f = f + mu_ref[...] * (0.5 * (prev + nxt) - f)
    r = f[:, 0:RW_W]
    k = f[:, RW_W:2 * RW_W]
    v = f[:, 2 * RW_W:3 * RW_W]
    o = 3 * RW_W
    wl = f[:, o:o + 2 * LORA_W]
    al = f[:, o + 2 * LORA_W:o + 2 * LORA_W + 2 * LORA_A]
    gl = f[:, o + 2 * LORA_W + 2 * LORA_A:]
    w_raw = w0_ref[...] + _bdot(jnp.tanh(wl), w2_ref[...])
    decay = jnp.exp(-EXP_NEG_HALF * _sigmoid(w_raw))
    a = _sigmoid(a0_ref[...] + _bdot(al, a2_ref[...]))
    g_ref[0] = _bdot(_sigmoid(gl), g2_ref[...])
    ones_bd = ones_ref[...]
    kk = k * kk_ref[...]
    kk = kk * lax.rsqrt(_head_sum(kk * kk, ones_bd) + 1e-12)
    ka = ka_ref[...]
    kd = [k * (1.0 + (a[:, d * RW_W:(d + 1) * RW_W] - 1.0) * ka) for d in range(2)]
    bonus_ref[0] = _head_sum(r * (kd[0] + kd[1]) * rk_ref[...], ones_bd) * v
    parts = [r, kk, v, decay[:, :RW_W], kd[0], kk * a[:, :RW_W],
             decay[:, RW_W:], kd[1], kk * a[:, RW_W:]]
    pad = jnp.zeros((PITCH_F - RW_HEAD, TM), F32)
    for n, p in enumerate(parts):
        for hp in range(HEADS // 2):
            for half in range(TM // 128):
                pt = p[half * 128:(half + 1) * 128, hp * 128:(hp + 1) * 128].T
                for h2 in range(2):
                    row0 = (2 * hp + h2) * PITCH_F
                    sf_ref[0, n, row0:row0 + RW_HEAD, half * 128:(half + 1) * 128] = pt[h2 * RW_HEAD:(h2 + 1) * RW_HEAD]
        for hd in range(HEADS):
            sf_ref[0, n, hd * PITCH_F + RW_HEAD:(hd + 1) * PITCH_F, :] = pad


def _halo_specs(width, t):
    nb = TM // SUB
    last = t // SUB - 1
    prev = pl.BlockSpec((1, SUB, width), lambda bb, i: (bb, jnp.maximum(i * nb - 1, 0), 0))
    nxt = pl.BlockSpec((1, SUB, width), lambda bb, i: (bb, jnp.minimum((i + 1) * nb, last), 0))
    return prev, nxt


def _time_mirror(i, n_ctx_tiles, n_tiles):
    return jnp.where(i < n_ctx_tiles, n_ctx_tiles - 1 - i, n_tiles + n_ctx_tiles - 1 - i)


def _reverse_lanes(x, j3):
    hi = x.astype(BF16)
    r1 = x - hi.astype(F32)
    mid = r1.astype(BF16)
    lo = (r1 - mid.astype(F32)).astype(BF16)
    return jnp.dot(jnp.concatenate([hi, mid, lo], axis=1), j3, preferred_element_type=F32)


def _relayout_in_kernel(xa_ref, xb_ref, j3_ref, o_ref, *, step_major):
    nb = xa_ref.shape[0]
    tr = xa_ref.shape[3]
    half = nb * HEADS

    def gather(x_ref, f):
        return jnp.concatenate([x_ref[b, 0, pl.ds(f, HEADS, stride=PITCH_F), :] for b in range(nb)], axis=0)

    for f0 in range(0, RW_HEAD, RELAY_FB):
        fs = range(f0, f0 + RELAY_FB)
        rev = _reverse_lanes(jnp.concatenate([gather(xb_ref, f) for f in fs], axis=0), j3_ref[...])
        for n, f in enumerate(fs):
            mt = jnp.concatenate([gather(xa_ref, f), rev[n * half:(n + 1) * half]], axis=0).T
            if step_major:
                o_ref[pl.ds(f, tr, stride=PITCH_F), :] = mt
            else:
                o_ref[0, f] = mt
    if step_major:
        for f in range(RW_HEAD, PITCH_F):
            o_ref[pl.ds(f, tr, stride=PITCH_F), :] = jnp.zeros((tr, 2 * half), F32)


def _relayout_in(sf, j3):
    b, _, rows, t = sf.shape
    chains = 2 * b * HEADS
    nt = t // RELAY_T
    nc = TM // RELAY_T
    mirror = lambda i: _time_mirror(i, nc, nt)
    blk = (b, 1, rows, RELAY_T)
    dirdep = lambda p: jnp.minimum(p // 2, 1)
    x = pl.pallas_call(
        functools.partial(_relayout_in_kernel, step_major=False),
        grid=(nt, 5),
        in_specs=[pl.BlockSpec(blk, lambda i, p: (0, p + dirdep(p), 0, i)),
                  pl.BlockSpec(blk, lambda i, p: (0, p + 4 * dirdep(p), 0, mirror(i))),
                  _const_spec(j3.shape)],
        out_specs=pl.BlockSpec((1, RW_HEAD, RELAY_T, chains), lambda i, p: (p, 0, i, 0)),
        out_shape=jax.ShapeDtypeStruct((5, RW_HEAD, t, chains), F32),
        compiler_params=_params(("parallel", "parallel")),
        name="l0_rwkv_relayout_in",
    )(sf, sf, j3)
    v = pl.pallas_call(
        functools.partial(_relayout_in_kernel, step_major=True),
        grid=(nt,),
        in_specs=[pl.BlockSpec(blk, lambda i: (0, 2, 0, i)),
                  pl.BlockSpec(blk, lambda i: (0, 2, 0, mirror(i))),
                  _const_spec(j3.shape)],
        out_specs=pl.BlockSpec((RELAY_T * PITCH_F, chains), lambda i: (i, 0)),
        out_shape=jax.ShapeDtypeStruct((t * PITCH_F, chains), F32),
        compiler_params=_params(("parallel",)),
        name="l0_rwkv_relayout_v",
    )(sf, sf, j3)
    return x, v


def _rwscan_kernel(x_ref, v_ref, y_ref, s_ref):
    tc = x_ref.shape[2]
    chains = x_ref.shape[3]

    @pl.when(pl.program_id(0) == 0)
    def _():
        s_ref[...] = jnp.zeros_like(s_ref)

    groups = RW_HEAD // SUB
    zeros = tuple(jnp.zeros((SUB, chains), F32) for _ in range(groups))

    def row(a, j, s):
        return jnp.broadcast_to(x_ref[a, j, pl.ds(s, 1), :], (SUB, chains))

    def s_at(j, g):
        return s_ref.at[j, g * SUB:(g + 1) * SUB, :]

    sa0 = list(zeros)
    for j in range(RW_HEAD):
        kk = row(1, j, 0)
        for g in range(groups):
            sa0[g] = sa0[g] + s_at(j, g)[...] * kk

    def step(s, sa):
        base = pl.multiple_of(s * PITCH_F, SUB)
        v = v_ref[pl.ds(base, RW_HEAD), :]
        s_next = jnp.minimum(s + 1, tc - 1)
        y = list(zeros)
        sa_next = list(zeros)
        for j in range(RW_HEAD):
            r, kk_next, w, k, kka = row(0, j, s), row(1, j, s_next), row(2, j, s), row(3, j, s), row(4, j, s)
            for g in range(groups):
                sj = s_at(j, g)[...] * w - sa[g] * kka + v[g * SUB:(g + 1) * SUB] * k
                s_at(j, g)[...] = sj
                y[g] = y[g] + sj * r
                sa_next[g] = sa_next[g] + sj * kk_next
        y_ref[pl.ds(base, RW_HEAD), :] = jnp.concatenate(y, axis=0)
        y_ref[pl.ds(base + RW_HEAD, PITCH_F - RW_HEAD), :] = zeros[0]
        return tuple(sa_next)

    lax.fori_loop(0, tc, step, tuple(sa0))


def _rwscan(x, v):
    _, _, t, chains = x.shape
    tc = SCAN_TC if t % SCAN_TC == 0 else SCAN_TC_SMALL
    return pl.pallas_call(
        _rwscan_kernel,
        grid=(t // tc,),
        in_specs=[pl.BlockSpec((5, RW_HEAD, tc, chains), lambda i: (0, 0, i, 0)),
                  pl.BlockSpec((tc * PITCH_F, chains), lambda i: (i, 0))],
        out_specs=pl.BlockSpec((tc * PITCH_F, chains), lambda i: (i, 0)),
        out_shape=jax.ShapeDtypeStruct((t * PITCH_F, chains), F32),
        scratch_shapes=[pltpu.VMEM((RW_HEAD, RW_HEAD, chains), F32)],
        compiler_params=_params(("arbitrary",)),
        name="l0_rwkv_scan",
    )(x, v)


def _relayout_out_kernel(yf_ref, yr_ref, j3_ref, o_ref, q_ref):
    nb = o_ref.shape[0]
    tr = o_ref.shape[1]
    chains = yf_ref.shape[1]
    half = chains // 2
    pitch_c = chains + SUB
    fwd = lax.broadcasted_iota(jnp.int32, (tr, chains), 1) < half
    for i in range(RW_HEAD):
        rows = pl.ds(i, tr, stride=PITCH_F)
        mt = jnp.where(fwd, yf_ref[rows, :], yr_ref[rows, :]).T
        q_ref[i * pitch_c:i * pitch_c + half, :] = mt[:half]
        q_ref[i * pitch_c + half:i * pitch_c + chains, :] = _reverse_lanes(mt[half:], j3_ref[...])
    for b in range(nb):
        for hp in range(HEADS // 2):
            parts = []
            for h2 in range(2):
                c = b * HEADS + 2 * hp + h2
                parts.append(q_ref[pl.ds(c, RW_HEAD, stride=pitch_c), :]
                             + q_ref[pl.ds(half + c, RW_HEAD, stride=pitch_c), :])
            o_ref[b, :, hp * 128:(hp + 1) * 128] = jnp.concatenate(parts, axis=0).T


def _relayout_out(y, j3, b):
    chains = y.shape[1]
    t = y.shape[0] // PITCH_F
    nt = t // RELAY_T
    nc = TM // RELAY_T
    return pl.pallas_call(
        _relayout_out_kernel,
        grid=(nt,),
        in_specs=[pl.BlockSpec((RELAY_T * PITCH_F, chains), lambda i: (i, 0)),
                  pl.BlockSpec((RELAY_T * PITCH_F, chains), lambda i: (_time_mirror(i, nc, nt), 0)),
                  _const_spec(j3.shape)],
        out_specs=pl.BlockSpec((b, RELAY_T, RW_W), lambda i: (0, i, 0)),
        out_shape=jax.ShapeDtypeStruct((b, t, RW_W), F32),
        scratch_shapes=[pltpu.VMEM((RW_HEAD * (chains + SUB), RELAY_T), F32)],
        compiler_params=_params(("parallel",)),
        name="l0_rwkv_relayout_out",
    )(y, y, j3)


def _rwkv_scan_both(sf):
    k = jnp.arange(RELAY_T)
    anti = (k[:, None] + k[None, :] == RELAY_T - 1).astype(BF16)
    j3 = jnp.concatenate([anti, anti, anti], axis=0)
    y = _rwscan(*_relayout_in(sf, j3))
    return _relayout_out(y, j3, sf.shape[0])


def _mlp_hidden(h, w1_ref, w2_ref):
    acc = None
    for c in range(D_FF // MLP_CHUNK):
        u = jnp.dot(h, w1_ref[:, c * MLP_CHUNK:(c + 1) * MLP_CHUNK], preferred_element_type=F32)
        u = jnp.square(jnp.maximum(u, 0.0)).astype(BF16)
        part = jnp.dot(u, w2_ref[c * MLP_CHUNK:(c + 1) * MLP_CHUNK, :], preferred_element_type=F32)
        acc = part if acc is None else acc + part
    return acc


def _tail0_kernel(att_ref, ys_ref, bonus_ref, g_ref, x_ref, mod_ref, mod1_ref, gnw_ref, gnb_ref, ones_ref,
                  woa_ref, wob_ref, l1g_ref, l1b_ref, w1_ref, w2_ref, l2g_ref, l2b_ref, win1_ref,
                  x2_ref, gate_ref, xr_ref):
    m = mod_ref[0, 0]
    ones_bd = ones_ref[...]
    y = ys_ref[0]
    mu = _head_sum(y, ones_bd) * (1.0 / RW_HEAD)
    yc = y - mu
    var = _head_sum(yc * yc, ones_bd) * (1.0 / RW_HEAD)
    yn = yc * lax.rsqrt(var + GN_EPS) * gnw_ref[...] + gnb_ref[...]
    rw = (yn + bonus_ref[0]) * g_ref[0]
    o = (jnp.dot(att_ref[0], woa_ref[...], preferred_element_type=F32)
         + _bdot(rw, wob_ref[...]))
    x1 = _layer_norm(ALPHA * x_ref[0] + m[2:3] * o, l1g_ref[...], l1b_ref[...])
    acc = _mlp_hidden((x1 * (1.0 + m[4:5]) + m[3:4]).astype(BF16), w1_ref, w2_ref)
    x2 = _layer_norm(ALPHA * x1 + m[5:6] * acc, l2g_ref[...], l2b_ref[...])
    x2_ref[0] = x2
    m1 = mod1_ref[0, 0]
    f = _bdot(x2 * (1.0 + m1[1:2]) + m1[0:1], win1_ref[...])
    gate_ref[0] = f[:, :LRU_W]
    xr_ref[0] = f[:, LRU_W:]


def _tail0(att, ys, bonus, g, xc, mod0, mod1, consts):
    b, t, _ = xc.shape
    row = lambda w: pl.BlockSpec((1, TM, w), lambda bb, i: (bb, i, 0))
    return pl.pallas_call(
        _tail0_kernel,
        grid=(b, t // TM),
        in_specs=[row(RW_W), row(RW_W), row(RW_W), row(RW_W), row(D), _mod_spec(True), _mod_spec(True)]
                 + [_weight_spec(c.shape) for c in consts],
        out_specs=[row(D), row(LRU_W), row(LRU_W)],
        out_shape=[jax.ShapeDtypeStruct((b, t, D), F32), jax.ShapeDtypeStruct((b, t, LRU_W), F32),
                   jax.ShapeDtypeStruct((b, t, LRU_W), F32)],
        compiler_params=_params(("parallel", "parallel")),
        name="l0_tail",
    )(att, ys, bonus, g, xc, mod0, mod1, *consts)


def _lru_fwd_kernel(x_ref, xp_ref, xn_ref, cw_ref, cb_ref, wbd_ref, gb_ref, lam_ref,
                    hf_ref, a1_ref, u1_ref, a_scr, u_scr, c_ref):
    i = pl.program_id(0)
    n = pl.num_programs(0)
    nb, tl, _ = x_ref.shape
    rows = nb * tl
    nc = TM // tl
    pitch_b = tl + SUB

    @pl.when(i == 0)
    def _():
        c_ref[...] = jnp.zeros_like(c_ref)

    lam = lam_ref[...]
    nl = -lam
    softplus = jnp.maximum(nl, 0.0) + jnp.log1p(jnp.exp(-jnp.abs(nl)))
    gb = gb_ref[...]
    cw = cw_ref[...]
    tpos = jnp.bitwise_and(lax.broadcasted_iota(jnp.int32, (rows, 1), 0), tl - 1)
    seg_first = jnp.logical_or(i == 0, i == nc)
    seg_last = jnp.logical_or(i == nc - 1, i == n - 1)

    def halo(ref, r, edge):
        per_b = [jnp.broadcast_to(ref[b, r:r + 1, :], (tl, LRU_W)) for b in range(nb)]
        return jnp.where(edge, 0.0, jnp.concatenate(per_b, axis=0))

    x = x_ref[...].reshape(rows, LRU_W)
    p1 = halo(xp_ref, SUB - 1, seg_first)
    p2 = halo(xp_ref, SUB - 2, seg_first)
    n1 = halo(xn_ref, 0, seg_last)
    xm1 = jnp.where(tpos == 0, p1, pltpu.roll(x, 1, axis=0))
    xm2 = jnp.where(tpos == 0, p2, jnp.where(tpos == 1, p1, pltpu.roll(x, 2, axis=0)))
    xp1 = jnp.where(tpos == tl - 1, n1, pltpu.roll(x, rows - 1, axis=0))
    xc = cw[0:1] * xm2 + cw[1:2] * xm1 + cw[2:3] * x + cw[3:4] * xp1 + cb_ref[...]
    for blk in range(LRU_BLOCKS):
        sl = slice(blk * LRU_BLOCK, (blk + 1) * LRU_BLOCK)
        xb = xc[:, sl]
        z = _bdot(xb, wbd_ref[blk])
        for d in range(2):
            zr = z[:, 2 * d * LRU_BLOCK:(2 * d + 1) * LRU_BLOCK] + gb[2 * d:2 * d + 1, sl]
            zi = z[:, (2 * d + 1) * LRU_BLOCK:(2 * d + 2) * LRU_BLOCK] + gb[2 * d + 1:2 * d + 2, sl]
            log_a = -LRU_C * _sigmoid(zr) * softplus[d:d + 1, sl]
            a = jnp.exp(log_a)
            u = jnp.sqrt(-jnp.tanh(log_a) * (a * a + 1.0)) * (_sigmoid(zi) * xb)
            for b in range(nb):
                a_scr[d, blk, b * pitch_b:b * pitch_b + tl, :] = a[b * tl:(b + 1) * tl]
                u_scr[d, blk, b * pitch_b:b * pitch_b + tl, :] = u[b * tl:(b + 1) * tl]
    h = [c_ref[:, blk * LRU_BLOCK:(blk + 1) * LRU_BLOCK] for blk in range(LRU_BLOCKS)]
    pad = jnp.zeros((SUB, LRU_BLOCK), F32)
    for t in range(tl):
        for blk in range(LRU_BLOCKS):
            step_rows = pl.ds(t, nb, stride=pitch_b)
            dst = slice(t * PITCH_H + blk * SUB, t * PITCH_H + blk * SUB + nb)
            h[blk] = a_scr[0, blk, step_rows, :] * h[blk] + u_scr[0, blk, step_rows, :]
            hf_ref[dst, :] = h[blk]
            a1_ref[dst, :] = a_scr[1, blk, step_rows, :]
            u1_ref[dst, :] = u_scr[1, blk, step_rows, :]
        for ref in (hf_ref, a1_ref, u1_ref):
            ref[t * PITCH_H + LRU_BLOCKS * SUB:(t + 1) * PITCH_H, :] = pad
    for blk in range(LRU_BLOCKS):
        c_ref[:, blk * LRU_BLOCK:(blk + 1) * LRU_BLOCK] = h[blk]


def _lru_bwd_kernel(a_ref, u_ref, h_ref, c_ref, *, nb):
    tb = a_ref.shape[0] // PITCH_H

    @pl.when(pl.program_id(0) == 0)
    def _():
        c_ref[...] = jnp.zeros_like(c_ref)

    def step(s, h):
        base = pl.multiple_of((tb - 1 - s) * PITCH_H, SUB)
        out = []
        for blk in range(LRU_BLOCKS):
            rows = pl.ds(base + blk * SUB, nb)
            hb = a_ref[rows, :] * h[blk] + u_ref[rows, :]
            h_ref[rows, :] = hb
            out.append(hb)
        h_ref[pl.ds(base + LRU_BLOCKS * SUB, SUB), :] = jnp.zeros((SUB, LRU_BLOCK), F32)
        return tuple(out)

    h0 = tuple(c_ref[blk] for blk in range(LRU_BLOCKS))
    h = lax.fori_loop(0, tb, step, h0)
    for blk in range(LRU_BLOCKS):
        c_ref[blk] = h[blk]


def _lru(xr, cw, cb, wbd, gb, lam):
    b, t, _ = xr.shape
    assert b <= SUB
    n = t // LRU_TL
    hb = LRU_TL // SUB
    last = t // SUB - 1
    consts = (cw, cb, wbd, gb, lam)
    hspec = pl.BlockSpec((LRU_TL * PITCH_H, LRU_BLOCK), lambda i: (i, 0))
    coef = jax.ShapeDtypeStruct((t * PITCH_H, LRU_BLOCK), F32)
    scr = pltpu.VMEM((2, LRU_BLOCKS, b * (LRU_TL + SUB), LRU_BLOCK), F32)
    hf, a1, u1 = pl.pallas_call(
        _lru_fwd_kernel,
        grid=(n,),
        in_specs=[pl.BlockSpec((b, LRU_TL, LRU_W), lambda i: (0, i, 0)),
                  pl.BlockSpec((b, SUB, LRU_W), lambda i: (0, jnp.maximum(i * hb - 1, 0), 0)),
                  pl.BlockSpec((b, SUB, LRU_W), lambda i: (0, jnp.minimum((i + 1) * hb, last), 0))]
                 + [_const_spec(c.shape) for c in consts],
        out_specs=[hspec, hspec, hspec],
        out_shape=[coef, coef, coef],
        scratch_shapes=[scr, scr, pltpu.VMEM((b, LRU_W), F32)],
        compiler_params=_params(("arbitrary",)),
        name="l1_lru_fwd",
    )(xr, xr, xr, *consts)
    nt = t // LRU_TB
    nc = TM // LRU_TB
    bspec = pl.BlockSpec((LRU_TB * PITCH_H, LRU_BLOCK), lambda i: (_time_mirror(i, nc, nt), 0))
    hr = pl.pallas_call(
        functools.partial(_lru_bwd_kernel, nb=b),
        grid=(nt,),
        in_specs=[bspec, bspec],
        out_specs=bspec,
        out_shape=coef,
        scratch_shapes=[pltpu.VMEM((LRU_BLOCKS, b, LRU_BLOCK), F32)],
        compiler_params=_params(("arbitrary",)),
        name="l1_lru_bwd",
    )(a1, u1)
    return hf, hr


def _tail1_kernel(gate_ref, hf_ref, hr_ref, x_ref, mod_ref, w_ref, l1g_ref, l1b_ref, w1_ref, w2_ref, l2g_ref, l2b_ref,
                  o_ref):
    nb, tl, _ = gate_ref.shape
    per_b = []
    for b in range(nb):
        cols = []
        for blk in range(LRU_BLOCKS):
            rows = pl.ds(blk * SUB + b, tl, stride=PITCH_H)
            cols.append(hf_ref[rows, :] + hr_ref[rows, :])
        per_b.append(jnp.concatenate(cols, axis=1))
    h = jnp.concatenate(per_b, axis=0)
    gate = gate_ref[...].reshape(nb * tl, LRU_W)
    gelu = 0.5 * gate * (1.0 + jnp.tanh(math.sqrt(2.0 / math.pi) * (gate + 0.044715 * gate * gate * gate)))
    o = _bdot(gelu * h, w_ref[...])
    mods = [mod_ref[b, 0] for b in range(nb)]
    x1 = [_layer_norm(ALPHA * x_ref[b] + mods[b][2:3] * o[b * tl:(b + 1) * tl], l1g_ref[...], l1b_ref[...])
          for b in range(nb)]
    hm = jnp.concatenate([x1[b] * (1.0 + mods[b][4:5]) + mods[b][3:4] for b in range(nb)], axis=0)
    acc = _mlp_hidden(hm.astype(BF16), w1_ref, w2_ref)
    for b in range(nb):
        z = ALPHA * x1[b] + mods[b][5:6] * acc[b * tl:(b + 1) * tl]
        o_ref[b] = _layer_norm(z, l2g_ref[...], l2b_ref[...])


def _tail1(gate, hf, hr, xc, modt, consts):
    b, t, _ = xc.shape
    off = TM // LRU_TL
    nt = t // LRU_TL - off
    lat = lambda wd: pl.BlockSpec((b, LRU_TL, wd), lambda i: (0, i + off, 0))
    hspec = pl.BlockSpec((LRU_TL * PITCH_H, LRU_BLOCK), lambda i: (i + off, 0))
    return pl.pallas_call(
        _tail1_kernel,
        grid=(nt,),
        in_specs=[lat(LRU_W), hspec, hspec, lat(D),
                  pl.BlockSpec((b, 1, SUB, D), lambda i: (0, 1, 0, 0))]
                 + [_weight_spec(c.shape) for c in consts],
        out_specs=pl.BlockSpec((b, LRU_TL, D), lambda i: (0, i, 0)),
        out_shape=jax.ShapeDtypeStruct((b, nt * LRU_TL, D), F32),
        compiler_params=_params(("parallel",)),
        name="l1_tail",
    )(gate, hf, hr, xc, modt, *consts)


def _rot_cols(w):
    ws = w.reshape(w.shape[:-1] + (2, 2, ROPE_AXIS // 2))
    return jnp.stack([-ws[..., 1, :], ws[..., 0, :]], axis=-2).reshape(w.shape)


def _rope_tables(n, n_ctx):
    rows_n = n // GRID_W
    rows = jnp.repeat(jnp.arange(rows_n, dtype=F32), GRID_W)
    cols = jnp.tile(jnp.arange(GRID_W, dtype=F32), rows_n)
    inv_freq = ROPE_THETA ** (-jnp.arange(0, ROPE_AXIS, 2, dtype=F32) / ROPE_AXIS)
    ang_r = rows[:, None] * inv_freq
    ang_c = cols[:, None] * inv_freq
    ang = jnp.concatenate([ang_r, ang_r, ang_c, ang_c], axis=-1)
    cos = jnp.concatenate([jnp.ones((n_ctx, MLA_ROPE), F32), jnp.cos(ang)], axis=0)
    sin = jnp.concatenate([jnp.zeros((n_ctx, MLA_ROPE), F32), jnp.sin(ang)], axis=0)
    t = n + n_ctx
    cs = jnp.concatenate([jnp.ones((t, MLA_NOPE), F32), cos, jnp.zeros((t, 32), F32)], axis=-1)
    sn = jnp.concatenate([jnp.zeros((t, MLA_NOPE), F32), sin, jnp.zeros((t, 32), F32)], axis=-1)
    return cs, sn


def _block_diag2(w):
    z = jnp.zeros_like(w[0])
    return jnp.concatenate([jnp.concatenate([w[0], z], axis=1), jnp.concatenate([z, w[1]], axis=1)], axis=0)


def kernel(x, c, ctx, c_ctx, l0_mod_w, l0_mod_b, l0_w_in, l0_mla_q_norm, l0_mla_w_uq, l0_mla_kv_norm, l0_mla_w_uk, l0_mla_w_uv, l0_rwkv_mu, l0_rwkv_w0, l0_rwkv_w2, l0_rwkv_a0, l0_rwkv_a2, l0_rwkv_g2, l0_rwkv_k_k, l0_rwkv_k_a, l0_rwkv_r_k, l0_rwkv_gn_w, l0_rwkv_gn_b, l0_w_out, l0_ln1_g, l0_ln1_b, l0_mlp_w1, l0_mlp_w2, l0_ln2_g, l0_ln2_b, l1_mod_w, l1_mod_b, l1_w_in, l1_conv_w, l1_conv_b, l1_lru_ga_w, l1_lru_ga_b, l1_lru_gx_w, l1_lru_gx_b, l1_lru_lambda, l1_w_out, l1_ln1_g, l1_ln1_b, l1_mlp_w1, l1_mlp_w2, l1_ln2_g, l1_ln2_b):
    b, n, _ = x.shape
    n_ctx = ctx.shape[1]
    assert n_ctx == TM and n % TM == 0 and x.shape[2] == D
    row = lambda v: v.reshape(1, -1)

    xc = jnp.concatenate([ctx, x], axis=1)
    mod0 = _mod_table(c, c_ctx, l0_mod_w, l0_mod_b)
    mod1 = _mod_table(c, c_ctx, l1_mod_w, l1_mod_b)

    o_kv = MLA_Q_RANK
    o_kr = o_kv + MLA_KV_RANK
    o_rw = o_kr + MLA_ROPE
    w_kr = l0_w_in[:, o_kr:o_rw]
    zl = jnp.zeros((D, MLA_NOPE), F32)
    zr = jnp.zeros((D, HEAD_PAD - MLA_NOPE - MLA_ROPE), F32)
    win0 = jnp.concatenate([l0_w_in[:, :o_kr], zl, w_kr, zr, zl, _rot_cols(w_kr), zr,
                            l0_w_in[:, o_rw:]], axis=1).astype(BF16)
    wq = l0_mla_w_uq.reshape(MLA_Q_RANK, HEADS, MLA_NOPE + MLA_ROPE)
    q_nope, q_rope = wq[..., :MLA_NOPE], wq[..., MLA_NOPE:]
    zq = jnp.zeros((MLA_Q_RANK, HEADS, 32), F32)
    wqa = jnp.concatenate([q_nope, q_rope, zq], axis=-1).reshape(MLA_Q_RANK, HEADS * HEAD_PAD).astype(BF16)
    wqb = jnp.concatenate([jnp.zeros_like(q_nope), _rot_cols(q_rope), zq],
                          axis=-1).reshape(MLA_Q_RANK, HEADS * HEAD_PAD).astype(BF16)
    wk = l0_mla_w_uk.reshape(MLA_KV_RANK, HEADS, MLA_NOPE)
    wuk = jnp.concatenate([wk, jnp.zeros_like(wk)], axis=-1).reshape(MLA_KV_RANK, HEADS * HEAD_PAD).astype(BF16)
    wuv = l0_mla_w_uv.astype(BF16)
    cs, sn = _rope_tables(n, n_ctx)
    hid = jnp.arange(RW_W) // RW_HEAD
    ones_bd = (hid[:, None] == hid[None, :]).astype(BF16)

    mla_consts = (win0, row(l0_mla_q_norm), row(l0_mla_kv_norm), wqa, wqb, wuk, wuv)
    rw_consts = (row(l0_rwkv_mu), row(l0_rwkv_w0), _block_diag2(l0_rwkv_w2).astype(BF16), row(l0_rwkv_a0),
                 _block_diag2(l0_rwkv_a2).astype(BF16), l0_rwkv_g2.astype(BF16), row(l0_rwkv_k_k),
                 row(l0_rwkv_k_a), row(l0_rwkv_r_k), ones_bd)
    q, k, v, sf, g, bonus = _front0(xc, mod0, mla_consts, cs, sn, rw_consts)
    att = _attention(q, k, v)
    ys = _rwkv_scan_both(sf)
    wo = l0_w_out.astype(BF16)
    tail0_consts = (row(l0_rwkv_gn_w), row(l0_rwkv_gn_b), ones_bd, wo[:HEADS * MLA_V], wo[HEADS * MLA_V:],
                    row(l0_ln1_g), row(l0_ln1_b), l0_mlp_w1.astype(BF16), l0_mlp_w2.astype(BF16),
                    row(l0_ln2_g), row(l0_ln2_b), l1_w_in.astype(BF16))
    xc, gate, xr = _tail0(att, ys, bonus, g, xc, mod0, mod1, tail0_consts)

    wbd = jnp.concatenate([l1_lru_ga_w[0], l1_lru_gx_w[0], l1_lru_ga_w[1], l1_lru_gx_w[1]], axis=-1).astype(BF16)
    gb = jnp.stack([l1_lru_ga_b[0], l1_lru_gx_b[0], l1_lru_ga_b[1], l1_lru_gx_b[1]])
    hf, hr = _lru(xr, l1_conv_w, row(l1_conv_b), wbd, gb, l1_lru_lambda)
    tail1_consts = (l1_w_out.astype(BF16), row(l1_ln1_g), row(l1_ln1_b), l1_mlp_w1.astype(BF16),
                    l1_mlp_w2.astype(BF16), row(l1_ln2_g), row(l1_ln2_b))
    return _tail1(gate, hf, hr, xc, mod1, tail1_consts)
```

```python
import functools
import math

import jax
import jax.numpy as jnp
from jax import lax
from jax.experimental import pallas as pl
from jax.experimental.pallas import tpu as pltpu

F32 = jnp.float32
BF16 = jnp.bfloat16

D = 1024
DEPTH = 2
N_MOD = 6
ALPHA = (2.0 * DEPTH) ** 0.25
LN_EPS = 1e-5
RMS_EPS = 1e-6

HEADS = 8
MLA_NOPE = 64
MLA_ROPE = 32
MLA_V = 64
MLA_Q_RANK = 384
MLA_KV_RANK = 256
ROPE_AXIS = MLA_ROPE // 2
ROPE_THETA = 10000.0
GRID_W = 64
ATT_SCALE = (MLA_NOPE + MLA_ROPE) ** -0.5
LOG2_E = math.log2(math.e)
HEAD_PAD = 128
ATT_HEADS = 8

RW_HEAD = 64
RW_W = HEADS * RW_HEAD
LORA_W = 64
LORA_A = 64
LORA_G = 128
RW_IN = 3 * RW_W + 2 * LORA_W + 2 * LORA_A + LORA_G
GN_EPS = 64e-5
EXP_NEG_HALF = math.exp(-0.5)

LRU_W = D
LRU_BLOCKS = 8
LRU_BLOCK = LRU_W // LRU_BLOCKS
LRU_C = 8.0
D_FF = 4 * D

TM = 256
SUB = 8
SCAN_TC = 72
SCAN_TC_SMALL = 32
RELAY_T = 128
RELAY_FB = 4
MLP_CHUNK = 1024
PITCH_F = RW_HEAD + SUB
PITCH_H = LRU_BLOCKS * SUB + SUB
LRU_TL = 32
LRU_TB = 128
VMEM_LIMIT = 56 * 1024 * 1024

IN0_COLS = MLA_Q_RANK + MLA_KV_RANK + 2 * HEAD_PAD + RW_IN


def _params(sem):
    return pltpu.CompilerParams(dimension_semantics=sem, vmem_limit_bytes=VMEM_LIMIT)


def _bdot(a, w):
    return jnp.dot(a.astype(BF16), w, preferred_element_type=F32)


def _sigmoid(x):
    return 0.5 * jnp.tanh(0.5 * x) + 0.5


def _layer_norm(z, g, b):
    mu = jnp.mean(z, axis=-1, keepdims=True)
    zc = z - mu
    var = jnp.mean(zc * zc, axis=-1, keepdims=True)
    return zc * lax.rsqrt(var + LN_EPS) * g + b


def _head_sum(x, ones_bd):
    hi = x.astype(BF16)
    lo = (x - hi.astype(F32)).astype(BF16)
    return (jnp.dot(hi, ones_bd, preferred_element_type=F32)
            + jnp.dot(lo, ones_bd, preferred_element_type=F32))


def _const_spec(shape):
    nd = len(shape)
    return pl.BlockSpec(shape, lambda *_: (0,) * nd)


def _weight_spec(shape):
    nd = len(shape)
    return pl.BlockSpec(shape, lambda *_: (0,) * nd, pipeline_mode=pl.Buffered(1))


def _mod_kernel(c_ref, w_ref, b_ref, o_ref):
    c = c_ref[...]
    s = c * _sigmoid(c)
    o_ref[...] = jnp.dot(s, w_ref[...], precision=lax.Precision.HIGHEST,
                         preferred_element_type=F32) + b_ref[...]


def _mod_table(c, c_ctx, mod_w, mod_b):
    b = c.shape[0]
    rows = 16
    cc = jnp.zeros((rows, D), F32).at[:b].set(c).at[b].set(c_ctx)
    tn = 1024
    out = pl.pallas_call(
        _mod_kernel,
        grid=(N_MOD * D // tn,),
        in_specs=[pl.BlockSpec((rows, D), lambda j: (0, 0)),
                  pl.BlockSpec((D, tn), lambda j: (0, j)),
                  pl.BlockSpec((1, tn), lambda j: (0, j))],
        out_specs=pl.BlockSpec((rows, tn), lambda j: (0, j)),
        out_shape=jax.ShapeDtypeStruct((rows, N_MOD * D), F32),
        compiler_params=_params(("arbitrary",)),
        name="mod_proj",
    )(cc, mod_w, mod_b.reshape(1, -1))
    lat = out[:b].reshape(b, N_MOD, D)
    ctx = jnp.broadcast_to(out[b].reshape(1, N_MOD, D), (b, N_MOD, D))
    tab = jnp.stack([ctx, lat], axis=1)
    return jnp.pad(tab, ((0, 0), (0, 0), (0, SUB - N_MOD), (0, 0)))


def _mod_spec(combined):
    if combined:
        return pl.BlockSpec((1, 1, SUB, D), lambda b, i: (b, jnp.minimum(i, 1), 0, 0))
    return pl.BlockSpec((1, 1, SUB, D), lambda b, i: (b, 1, 0, 0))


def _front0_kernel(ctx_ref, x_ref, xp_ref, xn_ref, mod_ref, win_ref, qn_ref, kvn_ref, wqa_ref, wqb_ref, wuk_ref, wuv_ref,
                   cs_ref, sn_ref, mu_ref, w0_ref, w2_ref, a0_ref, a2_ref, g2_ref, kk_ref, ka_ref, rk_ref, ones_ref,
                   q_ref, k_ref, v_ref, sf_ref, g_ref, bonus_ref):
    i = pl.program_id(1)
    has_prev, has_next = _halo_flags(i, pl.num_programs(1))
    m = mod_ref[0, 0]
    xt = jnp.where(i == 0, ctx_ref[0], x_ref[0])
    xe = jnp.concatenate([xt, xp_ref[0], xn_ref[0]], axis=0)
    h = xe * (1.0 + m[1:2]) + m[0:1]
    fe = _bdot(h, win_ref[...])
    f = fe[:TM]
    o_kv = MLA_Q_RANK
    o_kr = o_kv + MLA_KV_RANK
    o_rw = o_kr + 2 * HEAD_PAD
    prow = jnp.where(has_prev, fe[TM + SUB - 1:TM + SUB, o_rw:], 0.0)
    nrow = jnp.where(has_next, fe[TM + SUB:TM + SUB + 1, o_rw:], 0.0)
    _rw_features(f[:, o_rw:], prow, nrow, mu_ref, w0_ref, w2_ref, a0_ref, a2_ref, g2_ref, kk_ref, ka_ref, rk_ref,
                 ones_ref, sf_ref, g_ref, bonus_ref)
    fq = f[:, :o_kv]
    fkv = f[:, o_kv:o_kr]
    fkr = f[:, o_kr:o_kr + HEAD_PAD]
    fkr_rot = f[:, o_kr + HEAD_PAD:o_rw]
    qn = fq * lax.rsqrt(jnp.mean(fq * fq, axis=-1, keepdims=True) + RMS_EPS) * qn_ref[...]
    ckv = fkv * lax.rsqrt(jnp.mean(fkv * fkv, axis=-1, keepdims=True) + RMS_EPS) * kvn_ref[...]
    qn = qn.astype(BF16)
    ckv = ckv.astype(BF16)
    qa = jnp.dot(qn, wqa_ref[...], preferred_element_type=F32)
    qb = jnp.dot(qn, wqb_ref[...], preferred_element_type=F32)
    kn = jnp.dot(ckv, wuk_ref[...], preferred_element_type=F32)
    cs = cs_ref[...]
    sn = sn_ref[...]
    kr = fkr * cs + fkr_rot * sn
    for hd in range(HEADS):
        sl = slice(hd * HEAD_PAD, (hd + 1) * HEAD_PAD)
        q_ref[0, :, sl] = ((qa[:, sl] * cs + qb[:, sl] * sn) * (ATT_SCALE * LOG2_E)).astype(BF16)
        k_ref[0, :, sl] = (kn[:, sl] + kr).astype(BF16)
    v_ref[0] = jnp.dot(ckv, wuv_ref[...], preferred_element_type=F32).astype(BF16)


def _seg_specs(n):
    ctx_spec = pl.BlockSpec((1, TM, D), lambda bb, i: (bb, 0, 0))
    lat_spec = pl.BlockSpec((1, TM, D), lambda bb, i: (bb, jnp.maximum(i - 1, 0), 0))
    return ctx_spec, lat_spec


def _front0(ctx, x, modt, mla_consts, cs, sn, rw_consts):
    b, n, _ = x.shape
    t = n + TM
    row = lambda w: pl.BlockSpec((1, TM, w), lambda bb, i: (bb, i, 0))
    ctx_spec, lat_spec = _seg_specs(n)
    hb = TM // SUB
    prev = pl.BlockSpec((1, SUB, D), lambda bb, i: (bb, jnp.maximum((i - 1) * hb - 1, 0), 0))
    nxt = pl.BlockSpec((1, SUB, D), lambda bb, i: (bb, jnp.minimum(i * hb, n // SUB - 1), 0))
    rope = pl.BlockSpec((TM, HEAD_PAD), lambda bb, i: (i, 0))
    return pl.pallas_call(
        _front0_kernel,
        grid=(b, t // TM),
        in_specs=[ctx_spec, lat_spec, prev, nxt, _mod_spec(True)] + [_weight_spec(c.shape) for c in mla_consts]
                 + [rope, rope] + [_weight_spec(c.shape) for c in rw_consts],
        out_specs=[row(HEADS * HEAD_PAD), row(HEADS * HEAD_PAD), row(HEADS * MLA_V),
                   pl.BlockSpec((1, 9, HEADS * PITCH_F, TM), lambda bb, i: (bb, 0, 0, i)), row(RW_W), row(RW_W)],
        out_shape=[jax.ShapeDtypeStruct((b, t, HEADS * HEAD_PAD), BF16),
                   jax.ShapeDtypeStruct((b, t, HEADS * HEAD_PAD), BF16),
                   jax.ShapeDtypeStruct((b, t, HEADS * MLA_V), BF16),
                   jax.ShapeDtypeStruct((b, 9, HEADS * PITCH_F, t), F32),
                   jax.ShapeDtypeStruct((b, t, RW_W), F32),
                   jax.ShapeDtypeStruct((b, t, RW_W), F32)],
        compiler_params=_params(("parallel", "parallel")),
        name="l0_front",
    )(ctx, x, x, x, modt, *mla_consts, cs, sn, *rw_consts)


def _att_kernel(q_ref, k_ref, v_ref, o_ref, *, n_ctx, n_all):
    i = pl.program_id(2)

    def attend(nk):
        lane = lax.broadcasted_iota(jnp.int32, (nk, 2 * MLA_V), 1)
        for pair in range(ATT_HEADS // 2):
            v = v_ref[0, :nk, pair * 2 * MLA_V:(pair + 1) * 2 * MLA_V]
            acc = None
            for hh in range(2):
                hd = 2 * pair + hh
                sl = slice(hd * HEAD_PAD, (hd + 1) * HEAD_PAD)
                q = q_ref[0, :, sl]
                k = k_ref[0, :nk, sl]
                s = lax.dot_general(q, k, (((1,), (1,)), ((), ())), preferred_element_type=F32)
                p = jnp.exp2(s - jnp.max(s, axis=-1, keepdims=True))
                l = jnp.sum(p, axis=-1, keepdims=True)
                keep = (lane < MLA_V) if hh == 0 else (lane >= MLA_V)
                vh = jnp.where(keep, v, jnp.zeros_like(v))
                o = jnp.dot(p.astype(BF16), vh, preferred_element_type=F32) / l
                acc = o if acc is None else acc + o
            o_ref[0, :, pair * 2 * MLA_V:(pair + 1) * 2 * MLA_V] = acc.astype(o_ref.dtype)

    @pl.when(i == 0)
    def _():
        attend(n_ctx)

    @pl.when(i > 0)
    def _():
        attend(n_all)


def _attention(q, k, v):
    b, t, _ = q.shape
    return pl.pallas_call(
        functools.partial(_att_kernel, n_ctx=TM, n_all=t),
        grid=(b, HEADS // ATT_HEADS, t // TM),
        in_specs=[pl.BlockSpec((1, TM, ATT_HEADS * HEAD_PAD), lambda bb, hp, i: (bb, i, hp)),
                  pl.BlockSpec((1, t, ATT_HEADS * HEAD_PAD), lambda bb, hp, i: (bb, 0, hp)),
                  pl.BlockSpec((1, t, ATT_HEADS * MLA_V), lambda bb, hp, i: (bb, 0, hp))],
        out_specs=pl.BlockSpec((1, TM, ATT_HEADS * MLA_V), lambda bb, hp, i: (bb, i, hp)),
        out_shape=jax.ShapeDtypeStruct((b, t, HEADS * MLA_V), BF16),
        compiler_params=_params(("parallel", "parallel", "parallel")),
        name="l0_attention",
    )(q, k, v)


def _halo_flags(i, n_tiles):
    return i >= 2, jnp.logical_and(i >= 1, i < n_tiles - 1)


def _rw_features(f, prow, nrow, mu_ref, w0_ref, w2_ref, a0_ref, a2_ref, g2_ref,
                 kk_ref, ka_ref, rk_ref, ones_ref, sf_ref, g_ref, bonus_ref):
    rid = lax.broadcasted_iota(jnp.int32, (TM, 1), 0)
    prev = jnp.where(rid == 0, prow, pltpu.roll(f, 1, axis=0))
    nxt = jnp.where(rid == TM - 1, nrow, pltpu.roll(f, TM - 1, axis=0))
    f = f + mu_ref[...] * (0.5 * (prev + nxt) - f)
    r = f[:, 0:RW_W]
    k = f[:, RW_W:2 * RW_W]
    v = f[:, 2 * RW_W:3 * RW_W]
    o = 3 * RW_W
    wl = f[:, o:o + 2 * LORA_W]
    al = f[:, o + 2 * LORA_W:o + 2 * LORA_W + 2 * LORA_A]
    gl = f[:, o + 2 * LORA_W + 2 * LORA_A:]
    w_raw = w0_ref[...] + _bdot(jnp.tanh(wl), w2_ref[...])
    decay = jnp.exp(-EXP_NEG_HALF * _sigmoid(w_raw))
    a = _sigmoid(a0_ref[...] + _bdot(al, a2_ref[...]))
    g_ref[0] = _bdot(_sigmoid(gl), g2_ref[...])
    ones_bd = ones_ref[...]
    kk = k * kk_ref[...]
    kk = kk * lax.rsqrt(_head_sum(kk * kk, ones_bd) + 1e-12)
    ka = ka_ref[...]
    kd = [k * (1.0 + (a[:, d * RW_W:(d + 1) * RW_W] - 1.0) * ka) for d in range(2)]
    bonus_ref[0] = _head_sum(r * (kd[0] + kd[1]) * rk_ref[...], ones_bd) * v
    parts = [r, kk, v, decay[:, :RW_W], kd[0], kk * a[:, :RW_W],
             decay[:, RW_W:], kd[1], kk * a[:, RW_W:]]
    pad = jnp.zeros((PITCH_F - RW_HEAD, TM), F32)
    for n, p in enumerate(parts):
        for hp in range(HEADS // 2):
            for half in range(TM // 128):
                pt = p[half * 128:(half + 1) * 128, hp * 128:(hp + 1) * 128].T
                for h2 in range(2):
                    row0 = (2 * hp + h2) * PITCH_F
                    sf_ref[0, n, row0:row0 + RW_HEAD, half * 128:(half + 1) * 128] = pt[h2 * RW_HEAD:(h2 + 1) * RW_HEAD]
        for hd in range(HEADS):
            sf_ref[0, n, hd * PITCH_F + RW_HEAD:(hd + 1) * PITCH_F, :] = pad


def _time_mirror(i, n_ctx_tiles, n_tiles):
    return jnp.where(i < n_ctx_tiles, n_ctx_tiles - 1 - i, n_tiles + n_ctx_tiles - 1 - i)


def _reverse_lanes(x, j3):
    hi = x.astype(BF16)
    r1 = x - hi.astype(F32)
    mid = r1.astype(BF16)
    lo = (r1 - mid.astype(F32)).astype(BF16)
    return jnp.dot(jnp.concatenate([hi, mid, lo], axis=1), j3, preferred_element_type=F32)


def _relayout_in_kernel(xa_ref, xb_ref, j3_ref, o_ref, *, step_major):
    nb = xa_ref.shape[0]
    tr = xa_ref.shape[3]
    half = nb * HEADS

    def gather(x_ref, f):
        return jnp.concatenate([x_ref[b, 0, pl.ds(f, HEADS, stride=PITCH_F), :] for b in range(nb)], axis=0)

    for f0 in range(0, RW_HEAD, RELAY_FB):
        fs = range(f0, f0 + RELAY_FB)
        rev = _reverse_lanes(jnp.concatenate([gather(xb_ref, f) for f in fs], axis=0), j3_ref[...])
        for n, f in enumerate(fs):
            mt = jnp.concatenate([gather(xa_ref, f), rev[n * half:(n + 1) * half]], axis=0).T
            if step_major:
                o_ref[pl.ds(f, tr, stride=PITCH_F), :] = mt
            else:
                o_ref[0, f] = mt
    if step_major:
        for f in range(RW_HEAD, PITCH_F):
            o_ref[pl.ds(f, tr, stride=PITCH_F), :] = jnp.zeros((tr, 2 * half), F32)


def _relayout_in(sf, j3):
    b, _, rows, t = sf.shape
    chains = 2 * b * HEADS
    nt = t // RELAY_T
    nc = TM // RELAY_T
    mirror = lambda i: _time_mirror(i, nc, nt)
    blk = (b, 1, rows, RELAY_T)
    dirdep = lambda p: jnp.minimum(p // 2, 1)
    x = pl.pallas_call(
        functools.partial(_relayout_in_kernel, step_major=False),
        grid=(nt, 5),
        in_specs=[pl.BlockSpec(blk, lambda i, p: (0, p + dirdep(p), 0, i)),
                  pl.BlockSpec(blk, lambda i, p: (0, p + 4 * dirdep(p), 0, mirror(i))),
                  _const_spec(j3.shape)],
        out_specs=pl.BlockSpec((1, RW_HEAD, RELAY_T, chains), lambda i, p: (p, 0, i, 0)),
        out_shape=jax.ShapeDtypeStruct((5, RW_HEAD, t, chains), F32),
        compiler_params=_params(("parallel", "parallel")),
        name="l0_rwkv_relayout_in",
    )(sf, sf, j3)
    v = pl.pallas_call(
        functools.partial(_relayout_in_kernel, step_major=True),
        grid=(nt,),
        in_specs=[pl.BlockSpec(blk, lambda i: (0, 2, 0, i)),
                  pl.BlockSpec(blk, lambda i: (0, 2, 0, mirror(i))),
                  _const_spec(j3.shape)],
        out_specs=pl.BlockSpec((RELAY_T * PITCH_F, chains), lambda i: (i, 0)),
        out_shape=jax.ShapeDtypeStruct((t * PITCH_F, chains), F32),
        compiler_params=_params(("parallel",)),
        name="l0_rwkv_relayout_v",
    )(sf, sf, j3)
    return x, v


def _rwscan_kernel(x_ref, v_ref, y_ref, s_ref):
    tc = x_ref.shape[2]
    chains = x_ref.shape[3]

    @pl.when(pl.program_id(0) == 0)
    def _():
        s_ref[...] = jnp.zeros_like(s_ref)

    groups = RW_HEAD // SUB
    zeros = tuple(jnp.zeros((SUB, chains), F32) for _ in range(groups))

    def row(a, j, s):
        return jnp.broadcast_to(x_ref[a, j, pl.ds(s, 1), :], (SUB, chains))

    def s_at(j, g):
        return s_ref.at[j, g * SUB:(g + 1) * SUB, :]

    sa0 = list(zeros)
    for j in range(RW_HEAD):
        kk = row(1, j, 0)
        for g in range(groups):
            sa0[g] = sa0[g] + s_at(j, g)[...] * kk

    def step(s, sa):
        base = pl.multiple_of(s * PITCH_F, SUB)
        v = v_ref[pl.ds(base, RW_HEAD), :]
        s_next = jnp.minimum(s + 1, tc - 1)
        y = list(zeros)
        sa_next = list(zeros)
        for j in range(RW_HEAD):
            r, kk_next, w, k, kka = row(0, j, s), row(1, j, s_next), row(2, j, s), row(3, j, s), row(4, j, s)
            for g in range(groups):
                sj = s_at(j, g)[...] * w - sa[g] * kka + v[g * SUB:(g + 1) * SUB] * k
                s_at(j, g)[...] = sj
                y[g] = y[g] + sj * r
                sa_next[g] = sa_next[g] + sj * kk_next
        y_ref[pl.ds(base, RW_HEAD), :] = jnp.concatenate(y, axis=0)
        y_ref[pl.ds(base + RW_HEAD, PITCH_F - RW_HEAD), :] = zeros[0]
        return tuple(sa_next)

    lax.fori_loop(0, tc, step, tuple(sa0))


def _rwscan(x, v):
    _, _, t, chains = x.shape
    tc = SCAN_TC if t % SCAN_TC == 0 else SCAN_TC_SMALL
    return pl.pallas_call(
        _rwscan_kernel,
        grid=(t // tc,),
        in_specs=[pl.BlockSpec((5, RW_HEAD, tc, chains), lambda i: (0, 0, i, 0)),
                  pl.BlockSpec((tc * PITCH_F, chains), lambda i: (i, 0))],
        out_specs=pl.BlockSpec((tc * PITCH_F, chains), lambda i: (i, 0)),
        out_shape=jax.ShapeDtypeStruct((t * PITCH_F, chains), F32),
        scratch_shapes=[pltpu.VMEM((RW_HEAD, RW_HEAD, chains), F32)],
        compiler_params=_params(("arbitrary",)),
        name="l0_rwkv_scan",
    )(x, v)


def _relayout_out_kernel(yf_ref, yr_ref, j3_ref, o_ref, q_ref):
    nb = o_ref.shape[0]
    tr = o_ref.shape[1]
    chains = yf_ref.shape[1]
    half = chains // 2
    pitch_c = chains + SUB
    fwd = lax.broadcasted_iota(jnp.int32, (tr, chains), 1) < half
    for i in range(RW_HEAD):
        rows = pl.ds(i, tr, stride=PITCH_F)
        mt = jnp.where(fwd, yf_ref[rows, :], yr_ref[rows, :]).T
        q_ref[i * pitch_c:i * pitch_c + half, :] = mt[:half]
        q_ref[i * pitch_c + half:i * pitch_c + chains, :] = _reverse_lanes(mt[half:], j3_ref[...])
    for b in range(nb):
        for hp in range(HEADS // 2):
            parts = []
            for h2 in range(2):
                c = b * HEADS + 2 * hp + h2
                parts.append(q_ref[pl.ds(c, RW_HEAD, stride=pitch_c), :]
                             + q_ref[pl.ds(half + c, RW_HEAD, stride=pitch_c), :])
            o_ref[b, :, hp * 128:(hp + 1) * 128] = jnp.concatenate(parts, axis=0).T


def _relayout_out(y, j3, b):
    chains = y.shape[1]
    t = y.shape[0] // PITCH_F
    nt = t // RELAY_T
    nc = TM // RELAY_T
    return pl.pallas_call(
        _relayout_out_kernel,
        grid=(nt,),
        in_specs=[pl.BlockSpec((RELAY_T * PITCH_F, chains), lambda i: (i, 0)),
                  pl.BlockSpec((RELAY_T * PITCH_F, chains), lambda i: (_time_mirror(i, nc, nt), 0)),
                  _const_spec(j3.shape)],
        out_specs=pl.BlockSpec((b, RELAY_T, RW_W), lambda i: (0, i, 0)),
        out_shape=jax.ShapeDtypeStruct((b, t, RW_W), F32),
        scratch_shapes=[pltpu.VMEM((RW_HEAD * (chains + SUB), RELAY_T), F32)],
        compiler_params=_params(("parallel",)),
        name="l0_rwkv_relayout_out",
    )(y, y, j3)


def _rwkv_scan_both(sf):
    k = jnp.arange(RELAY_T)
    anti = (k[:, None] + k[None, :] == RELAY_T - 1).astype(BF16)
    j3 = jnp.concatenate([anti, anti, anti], axis=0)
    y = _rwscan(*_relayout_in(sf, j3))
    return _relayout_out(y, j3, sf.shape[0])


def _mlp_hidden(h, w1_ref, w2_ref):
    acc = None
    for c in range(D_FF // MLP_CHUNK):
        u = jnp.dot(h, w1_ref[:, c * MLP_CHUNK:(c + 1) * MLP_CHUNK], preferred_element_type=F32)
        u = jnp.square(jnp.maximum(u, 0.0)).astype(BF16)
        part = jnp.dot(u, w2_ref[c * MLP_CHUNK:(c + 1) * MLP_CHUNK, :], preferred_element_type=F32)
        acc = part if acc is None else acc + part
    return acc


def _tail0_kernel(att_ref, ys_ref, bonus_ref, g_ref, ctx_ref, x_ref, mod_ref, mod1_ref, gnw_ref, gnb_ref, ones_ref,
                  woa_ref, wob_ref, l1g_ref, l1b_ref, w1_ref, w2_ref, l2g_ref, l2b_ref, win1_ref,
                  x2_ref, gate_ref, xr_ref):
    m = mod_ref[0, 0]
    ones_bd = ones_ref[...]
    y = ys_ref[0]
    mu = _head_sum(y, ones_bd) * (1.0 / RW_HEAD)
    yc = y - mu
    var = _head_sum(yc * yc, ones_bd) * (1.0 / RW_HEAD)
    yn = yc * lax.rsqrt(var + GN_EPS) * gnw_ref[...] + gnb_ref[...]
    rw = (yn + bonus_ref[0]) * g_ref[0]
    o = (jnp.dot(att_ref[0], woa_ref[...], preferred_element_type=F32)
         + _bdot(rw, wob_ref[...]))
    x0 = jnp.where(pl.program_id(1) == 0, ctx_ref[0], x_ref[0])
    x1 = _layer_norm(ALPHA * x0 + m[2:3] * o, l1g_ref[...], l1b_ref[...])
    acc = _mlp_hidden((x1 * (1.0 + m[4:5]) + m[3:4]).astype(BF16), w1_ref, w2_ref)
    x2 = _layer_norm(ALPHA * x1 + m[5:6] * acc, l2g_ref[...], l2b_ref[...])
    x2_ref[0] = x2
    m1 = mod1_ref[0, 0]
    f = _bdot(x2 * (1.0 + m1[1:2]) + m1[0:1], win1_ref[...])
    gate_ref[0] = f[:, :LRU_W]
    xr_ref[0] = f[:, LRU_W:]


def _tail0(att, ys, bonus, g, ctx, x, mod0, mod1, consts):
    b, n, _ = x.shape
    t = n + TM
    row = lambda w: pl.BlockSpec((1, TM, w), lambda bb, i: (bb, i, 0))
    ctx_spec, lat_spec = _seg_specs(n)
    return pl.pallas_call(
        _tail0_kernel,
        grid=(b, t // TM),
        in_specs=[row(RW_W), row(RW_W), row(RW_W), row(RW_W), ctx_spec, lat_spec, _mod_spec(True), _mod_spec(True)]
                 + [_weight_spec(c.shape) for c in consts],
        out_specs=[row(D), row(LRU_W), row(LRU_W)],
        out_shape=[jax.ShapeDtypeStruct((b, t, D), F32), jax.ShapeDtypeStruct((b, t, LRU_W), F32),
                   jax.ShapeDtypeStruct((b, t, LRU_W), F32)],
        compiler_params=_params(("parallel", "parallel")),
        name="l0_tail",
    )(att, ys, bonus, g, ctx, x, mod0, mod1, *consts)


def _lru_fwd_kernel(x_ref, xp_ref, xn_ref, cw_ref, cb_ref, wbd_ref, gb_ref, lam_ref,
                    hf_ref, a1_ref, u1_ref, a_scr, u_scr, c_ref):
    i = pl.program_id(0)
    n = pl.num_programs(0)
    nb, tl, _ = x_ref.shape
    rows = nb * tl
    nc = TM // tl
    pitch_b = tl + SUB

    @pl.when(i == 0)
    def _():
        c_ref[...] = jnp.zeros_like(c_ref)

    lam = lam_ref[...]
    nl = -lam
    softplus = jnp.maximum(nl, 0.0) + jnp.log1p(jnp.exp(-jnp.abs(nl)))
    gb = gb_ref[...]
    cw = cw_ref[...]
    tpos = jnp.bitwise_and(lax.broadcasted_iota(jnp.int32, (rows, 1), 0), tl - 1)
    seg_first = jnp.logical_or(i == 0, i == nc)
    seg_last = jnp.logical_or(i == nc - 1, i == n - 1)

    def halo(ref, r, edge):
        per_b = [jnp.broadcast_to(ref[b, r:r + 1, :], (tl, LRU_W)) for b in range(nb)]
        return jnp.where(edge, 0.0, jnp.concatenate(per_b, axis=0))

    x = x_ref[...].reshape(rows, LRU_W)
    p1 = halo(xp_ref, SUB - 1, seg_first)
    p2 = halo(xp_ref, SUB - 2, seg_first)
    n1 = halo(xn_ref, 0, seg_last)
    xm1 = jnp.where(tpos == 0, p1, pltpu.roll(x, 1, axis=0))
    xm2 = jnp.where(tpos == 0, p2, jnp.where(tpos == 1, p1, pltpu.roll(x, 2, axis=0)))
    xp1 = jnp.where(tpos == tl - 1, n1, pltpu.roll(x, rows - 1, axis=0))
    xc = cw[0:1] * xm2 + cw[1:2] * xm1 + cw[2:3] * x + cw[3:4] * xp1 + cb_ref[...]
    for blk in range(LRU_BLOCKS):
        sl = slice(blk * LRU_BLOCK, (blk + 1) * LRU_BLOCK)
        xb = xc[:, sl]
        z = _bdot(xb, wbd_ref[blk])
        for d in range(2):
            zr = z[:, 2 * d * LRU_BLOCK:(2 * d + 1) * LRU_BLOCK] + gb[2 * d:2 * d + 1, sl]
            zi = z[:, (2 * d + 1) * LRU_BLOCK:(2 * d + 2) * LRU_BLOCK] + gb[2 * d + 1:2 * d + 2, sl]
            log_a = -LRU_C * _sigmoid(zr) * softplus[d:d + 1, sl]
            a = jnp.exp(log_a)
            u = jnp.sqrt(-jnp.tanh(log_a) * (a * a + 1.0)) * (_sigmoid(zi) * xb)
            for b in range(nb):
                a_scr[d, blk, b * pitch_b:b * pitch_b + tl, :] = a[b * tl:(b + 1) * tl]
                u_scr[d, blk, b * pitch_b:b * pitch_b + tl, :] = u[b * tl:(b + 1) * tl]
    h = [c_ref[:, blk * LRU_BLOCK:(blk + 1) * LRU_BLOCK] for blk in range(LRU_BLOCKS)]
    pad = jnp.zeros((SUB, LRU_BLOCK), F32)
    for t in range(tl):
        for blk in range(LRU_BLOCKS):
            step_rows = pl.ds(t, nb, stride=pitch_b)
            dst = slice(t * PITCH_H + blk * SUB, t * PITCH_H + blk * SUB + nb)
            h[blk] = a_scr[0, blk, step_rows, :] * h[blk] + u_scr[0, blk, step_rows, :]
            hf_ref[dst, :] = h[blk]
            a1_ref[dst, :] = a_scr[1, blk, step_rows, :]
            u1_ref[dst, :] = u_scr[1, blk, step_rows, :]
        for ref in (hf_ref, a1_ref, u1_ref):
            ref[t * PITCH_H + LRU_BLOCKS * SUB:(t + 1) * PITCH_H, :] = pad
    for blk in range(LRU_BLOCKS):
        c_ref[:, blk * LRU_BLOCK:(blk + 1) * LRU_BLOCK] = h[blk]


def _lru_bwd_kernel(a_ref, u_ref, h_ref, c_ref, *, nb):
    tb = a_ref.shape[0] // PITCH_H

    @pl.when(pl.program_id(0) == 0)
    def _():
        c_ref[...] = jnp.zeros_like(c_ref)

    def step(s, h):
        base = pl.multiple_of((tb - 1 - s) * PITCH_H, SUB)
        out = []
        for blk in range(LRU_BLOCKS):
            rows = pl.ds(base + blk * SUB, nb)
            hb = a_ref[rows, :] * h[blk] + u_ref[rows, :]
            h_ref[rows, :] = hb
            out.append(hb)
        h_ref[pl.ds(base + LRU_BLOCKS * SUB, SUB), :] = jnp.zeros((SUB, LRU_BLOCK), F32)
        return tuple(out)

    h0 = tuple(c_ref[blk] for blk in range(LRU_BLOCKS))
    h = lax.fori_loop(0, tb, step, h0)
    for blk in range(LRU_BLOCKS):
        c_ref[blk] = h[blk]


def _lru(xr, cw, cb, wbd, gb, lam):
    b, t, _ = xr.shape
    assert b <= SUB
    n = t // LRU_TL
    hb = LRU_TL // SUB
    last = t // SUB - 1
    consts = (cw, cb, wbd, gb, lam)
    hspec = pl.BlockSpec((LRU_TL * PITCH_H, LRU_BLOCK), lambda i: (i, 0))
    coef = jax.ShapeDtypeStruct((t * PITCH_H, LRU_BLOCK), F32)
    scr = pltpu.VMEM((2, LRU_BLOCKS, b * (LRU_TL + SUB), LRU_BLOCK), F32)
    hf, a1, u1 = pl.pallas_call(
        _lru_fwd_kernel,
        grid=(n,),
        in_specs=[pl.BlockSpec((b, LRU_TL, LRU_W), lambda i: (0, i, 0)),
                  pl.BlockSpec((b, SUB, LRU_W), lambda i: (0, jnp.maximum(i * hb - 1, 0), 0)),
                  pl.BlockSpec((b, SUB, LRU_W), lambda i: (0, jnp.minimum((i + 1) * hb, last), 0))]
                 + [_const_spec(c.shape) for c in consts],
        out_specs=[hspec, hspec, hspec],
        out_shape=[coef, coef, coef],
        scratch_shapes=[scr, scr, pltpu.VMEM((b, LRU_W), F32)],
        compiler_params=_params(("arbitrary",)),
        name="l1_lru_fwd",
    )(xr, xr, xr, *consts)
    nt = t // LRU_TB
    nc = TM // LRU_TB
    bspec = pl.BlockSpec((LRU_TB * PITCH_H, LRU_BLOCK), lambda i: (_time_mirror(i, nc, nt), 0))
    hr = pl.pallas_call(
        functools.partial(_lru_bwd_kernel, nb=b),
        grid=(nt,),
        in_specs=[bspec, bspec],
        out_specs=bspec,
        out_shape=coef,
        scratch_shapes=[pltpu.VMEM((LRU_BLOCKS, b, LRU_BLOCK), F32)],
        compiler_params=_params(("arbitrary",)),
        name="l1_lru_bwd",
    )(a1, u1)
    return hf, hr


def _tail1_kernel(gate_ref, hf_ref, hr_ref, x_ref, mod_ref, w_ref, l1g_ref, l1b_ref, w1_ref, w2_ref, l2g_ref, l2b_ref,
                  o_ref):
    nb, tl, _ = gate_ref.shape
    per_b = []
    for b in range(nb):
        cols = []
        for blk in range(LRU_BLOCKS):
            rows = pl.ds(blk * SUB + b, tl, stride=PITCH_H)
            cols.append(hf_ref[rows, :] + hr_ref[rows, :])
        per_b.append(jnp.concatenate(cols, axis=1))
    h = jnp.concatenate(per_b, axis=0)
    gate = gate_ref[...].reshape(nb * tl, LRU_W)
    gelu = 0.5 * gate * (1.0 + jnp.tanh(math.sqrt(2.0 / math.pi) * (gate + 0.044715 * gate * gate * gate)))
    o = _bdot(gelu * h, w_ref[...])
    mods = [mod_ref[b, 0] for b in range(nb)]
    x1 = [_layer_norm(ALPHA * x_ref[b] + mods[b][2:3] * o[b * tl:(b + 1) * tl], l1g_ref[...], l1b_ref[...])
          for b in range(nb)]
    hm = jnp.concatenate([x1[b] * (1.0 + mods[b][4:5]) + mods[b][3:4] for b in range(nb)], axis=0)
    acc = _mlp_hidden(hm.astype(BF16), w1_ref, w2_ref)
    for b in range(nb):
        z = ALPHA * x1[b] + mods[b][5:6] * acc[b * tl:(b + 1) * tl]
        o_ref[b] = _layer_norm(z, l2g_ref[...], l2b_ref[...])


def _tail1(gate, hf, hr, xc, modt, consts):
    b, t, _ = xc.shape
    off = TM // LRU_TL
    nt = t // LRU_TL - off
    lat = lambda wd: pl.BlockSpec((b, LRU_TL, wd), lambda i: (0, i + off, 0))
    hspec = pl.BlockSpec((LRU_TL * PITCH_H, LRU_BLOCK), lambda i: (i + off, 0))
    return pl.pallas_call(
        _tail1_kernel,
        grid=(nt,),
        in_specs=[lat(LRU_W), hspec, hspec, lat(D),
                  pl.BlockSpec((b, 1, SUB, D), lambda i: (0, 1, 0, 0))]
                 + [_weight_spec(c.shape) for c in consts],
        out_specs=pl.BlockSpec((b, LRU_TL, D), lambda i: (0, i, 0)),
        out_shape=jax.ShapeDtypeStruct((b, nt * LRU_TL, D), F32),
        compiler_params=_params(("parallel",)),
        name="l1_tail",
    )(gate, hf, hr, xc, modt, *consts)


def _rot_cols(w):
    ws = w.reshape(w.shape[:-1] + (2, 2, ROPE_AXIS // 2))
    return jnp.stack([-ws[..., 1, :], ws[..., 0, :]], axis=-2).reshape(w.shape)


def _rope_tables(n, n_ctx):
    rows_n = n // GRID_W
    rows = jnp.repeat(jnp.arange(rows_n, dtype=F32), GRID_W)
    cols = jnp.tile(jnp.arange(GRID_W, dtype=F32), rows_n)
    inv_freq = ROPE_THETA ** (-jnp.arange(0, ROPE_AXIS, 2, dtype=F32) / ROPE_AXIS)
    ang_r = rows[:, None] * inv_freq
    ang_c = cols[:, None] * inv_freq
    ang = jnp.concatenate([ang_r, ang_r, ang_c, ang_c], axis=-1)
    cos = jnp.concatenate([jnp.ones((n_ctx, MLA_ROPE), F32), jnp.cos(ang)], axis=0)
    sin = jnp.concatenate([jnp.zeros((n_ctx, MLA_ROPE), F32), jnp.sin(ang)], axis=0)
    t = n + n_ctx
    cs = jnp.concatenate([jnp.ones((t, MLA_NOPE), F32), cos, jnp.zeros((t, 32), F32)], axis=-1)
    sn = jnp.concatenate([jnp.zeros((t, MLA_NOPE), F32), sin, jnp.zeros((t, 32), F32)], axis=-1)
    return cs, sn


def _block_diag2(w):
    z = jnp.zeros_like(w[0])
    return jnp.concatenate([jnp.concatenate([w[0], z], axis=1), jnp.concatenate([z, w[1]], axis=1)], axis=0)


def kernel(x, c, ctx, c_ctx, l0_mod_w, l0_mod_b, l0_w_in, l0_mla_q_norm, l0_mla_w_uq, l0_mla_kv_norm, l0_mla_w_uk, l0_mla_w_uv, l0_rwkv_mu, l0_rwkv_w0, l0_rwkv_w2, l0_rwkv_a0, l0_rwkv_a2, l0_rwkv_g2, l0_rwkv_k_k, l0_rwkv_k_a, l0_rwkv_r_k, l0_rwkv_gn_w, l0_rwkv_gn_b, l0_w_out, l0_ln1_g, l0_ln1_b, l0_mlp_w1, l0_mlp_w2, l0_ln2_g, l0_ln2_b, l1_mod_w, l1_mod_b, l1_w_in, l1_conv_w, l1_conv_b, l1_lru_ga_w, l1_lru_ga_b, l1_lru_gx_w, l1_lru_gx_b, l1_lru_lambda, l1_w_out, l1_ln1_g, l1_ln1_b, l1_mlp_w1, l1_mlp_w2, l1_ln2_g, l1_ln2_b):
    b, n, _ = x.shape
    n_ctx = ctx.shape[1]
    assert n_ctx == TM and n % TM == 0 and x.shape[2] == D
    row = lambda v: v.reshape(1, -1)

    mod0 = _mod_table(c, c_ctx, l0_mod_w, l0_mod_b)
    mod1 = _mod_table(c, c_ctx, l1_mod_w, l1_mod_b)

    o_kv = MLA_Q_RANK
    o_kr = o_kv + MLA_KV_RANK
    o_rw = o_kr + MLA_ROPE
    w_kr = l0_w_in[:, o_kr:o_rw]
    zl = jnp.zeros((D, MLA_NOPE), F32)
    zr = jnp.zeros((D, HEAD_PAD - MLA_NOPE - MLA_ROPE), F32)
    win0 = jnp.concatenate([l0_w_in[:, :o_kr], zl, w_kr, zr, zl, _rot_cols(w_kr), zr,
                            l0_w_in[:, o_rw:]], axis=1).astype(BF16)
    wq = l0_mla_w_uq.reshape(MLA_Q_RANK, HEADS, MLA_NOPE + MLA_ROPE)
    q_nope, q_rope = wq[..., :MLA_NOPE], wq[..., MLA_NOPE:]
    zq = jnp.zeros((MLA_Q_RANK, HEADS, 32), F32)
    wqa = jnp.concatenate([q_nope, q_rope, zq], axis=-1).reshape(MLA_Q_RANK, HEADS * HEAD_PAD).astype(BF16)
    wqb = jnp.concatenate([jnp.zeros_like(q_nope), _rot_cols(q_rope), zq],
                          axis=-1).reshape(MLA_Q_RANK, HEADS * HEAD_PAD).astype(BF16)
    wk = l0_mla_w_uk.reshape(MLA_KV_RANK, HEADS, MLA_NOPE)
    wuk = jnp.concatenate([wk, jnp.zeros_like(wk)], axis=-1).reshape(MLA_KV_RANK, HEADS * HEAD_PAD).astype(BF16)
    wuv = l0_mla_w_uv.astype(BF16)
    cs, sn = _rope_tables(n, n_ctx)
    hid = jnp.arange(RW_W) // RW_HEAD
    ones_bd = (hid[:, None] == hid[None, :]).astype(BF16)

    mla_consts = (win0, row(l0_mla_q_norm), row(l0_mla_kv_norm), wqa, wqb, wuk, wuv)
    rw_consts = (row(l0_rwkv_mu), row(l0_rwkv_w0), _block_diag2(l0_rwkv_w2).astype(BF16), row(l0_rwkv_a0),
                 _block_diag2(l0_rwkv_a2).astype(BF16), l0_rwkv_g2.astype(BF16), row(l0_rwkv_k_k),
                 row(l0_rwkv_k_a), row(l0_rwkv_r_k), ones_bd)
    q, k, v, sf, g, bonus = _front0(ctx, x, mod0, mla_consts, cs, sn, rw_consts)
    att = _attention(q, k, v)
    ys = _rwkv_scan_both(sf)
    wo = l0_w_out.astype(BF16)
    tail0_consts = (row(l0_rwkv_gn_w), row(l0_rwkv_gn_b), ones_bd, wo[:HEADS * MLA_V], wo[HEADS * MLA_V:],
                    row(l0_ln1_g), row(l0_ln1_b), l0_mlp_w1.astype(BF16), l0_mlp_w2.astype(BF16),
                    row(l0_ln2_g), row(l0_ln2_b), l1_w_in.astype(BF16))
    xc, gate, xr = _tail0(att, ys, bonus, g, ctx, x, mod0, mod1, tail0_consts)

    wbd = jnp.concatenate([l1_lru_ga_w[0], l1_lru_gx_w[0], l1_lru_ga_w[1], l1_lru_gx_w[1]], axis=-1).astype(BF16)
    gb = jnp.stack([l1_lru_ga_b[0], l1_lru_gx_b[0], l1_lru_ga_b[1], l1_lru_gx_b[1]])
    hf, hr = _lru(xr, l1_conv_w, row(l1_conv_b), wbd, gb, l1_lru_lambda)
    tail1_consts = (l1_w_out.astype(BF16), row(l1_ln1_g), row(l1_ln1_b), l1_mlp_w1.astype(BF16),
                    l1_mlp_w2.astype(BF16), row(l1_ln2_g), row(l1_ln2_b))
    return _tail1(gate, hf, hr, xc, mod1, tail1_consts)
```

```python
import functools
import math

import jax
import jax.numpy as jnp
from jax import lax
from jax.experimental import pallas as pl
from jax.experimental.pallas import tpu as pltpu

F32 = jnp.float32
BF16 = jnp.bfloat16

D = 1024
DEPTH = 2
N_MOD = 6
ALPHA = (2.0 * DEPTH) ** 0.25
LN_EPS = 1e-5
RMS_EPS = 1e-6

HEADS = 8
MLA_NOPE = 64
MLA_ROPE = 32
MLA_V = 64
MLA_Q_RANK = 384
MLA_KV_RANK = 256
ROPE_AXIS = MLA_ROPE // 2
ROPE_THETA = 10000.0
GRID_W = 64
ATT_SCALE = (MLA_NOPE + MLA_ROPE) ** -0.5
LOG2_E = math.log2(math.e)
HEAD_PAD = 128
ATT_HEADS = 8

RW_HEAD = 64
RW_W = HEADS * RW_HEAD
LORA_W = 64
LORA_A = 64
LORA_G = 128
RW_IN = 3 * RW_W + 2 * LORA_W + 2 * LORA_A + LORA_G
GN_EPS = 64e-5
EXP_NEG_HALF = math.exp(-0.5)

LRU_W = D
LRU_BLOCKS = 8
LRU_BLOCK = LRU_W // LRU_BLOCKS
LRU_C = 8.0
D_FF = 4 * D

TM = 256
SUB = 8
SCAN_TC = 72
SCAN_TC_SMALL = 32
RELAY_T = 128
RELAY_FB = 4
MLP_CHUNK = 1024
PITCH_F = RW_HEAD + SUB
PITCH_H = LRU_BLOCKS * SUB + SUB
LRU_TL = 32
LRU_TB = 128
VMEM_LIMIT = 56 * 1024 * 1024

IN0_COLS = MLA_Q_RANK + MLA_KV_RANK + 2 * HEAD_PAD + RW_IN


def _params(sem):
    return pltpu.CompilerParams(dimension_semantics=sem, vmem_limit_bytes=VMEM_LIMIT)


def _bdot(a, w):
    return jnp.dot(a.astype(BF16), w, preferred_element_type=F32)


def _sigmoid(x):
    return 0.5 * jnp.tanh(0.5 * x) + 0.5


def _layer_norm(z, g, b):
    mu = jnp.mean(z, axis=-1, keepdims=True)
    zc = z - mu
    var = jnp.mean(zc * zc, axis=-1, keepdims=True)
    return zc * lax.rsqrt(var + LN_EPS) * g + b


def _head_sum(x, ones_bd):
    hi = x.astype(BF16)
    lo = (x - hi.astype(F32)).astype(BF16)
    return (jnp.dot(hi, ones_bd, preferred_element_type=F32)
            + jnp.dot(lo, ones_bd, preferred_element_type=F32))


def _const_spec(shape):
    nd = len(shape)
    return pl.BlockSpec(shape, lambda *_: (0,) * nd)


def _weight_spec(shape):
    nd = len(shape)
    return pl.BlockSpec(shape, lambda *_: (0,) * nd, pipeline_mode=pl.Buffered(1))


def _mod_kernel(c_ref, w_ref, b_ref, o_ref):
    c = c_ref[...]
    s = c * _sigmoid(c)
    o_ref[...] = jnp.dot(s, w_ref[...], precision=lax.Precision.HIGHEST,
                         preferred_element_type=F32) + b_ref[...]


def _mod_table(c, c_ctx, mod_w, mod_b):
    b = c.shape[0]
    rows = 16
    cc = jnp.zeros((rows, D), F32).at[:b].set(c).at[b].set(c_ctx)
    tn = 1024
    out = pl.pallas_call(
        _mod_kernel,
        grid=(N_MOD * D // tn,),
        in_specs=[pl.BlockSpec((rows, D), lambda j: (0, 0)),
                  pl.BlockSpec((D, tn), lambda j: (0, j)),
                  pl.BlockSpec((1, tn), lambda j: (0, j))],
        out_specs=pl.BlockSpec((rows, tn), lambda j: (0, j)),
        out_shape=jax.ShapeDtypeStruct((rows, N_MOD * D), F32),
        compiler_params=_params(("arbitrary",)),
        name="mod_proj",
    )(cc, mod_w, mod_b.reshape(1, -1))
    lat = out[:b].reshape(b, N_MOD, D)
    ctx = jnp.broadcast_to(out[b].reshape(1, N_MOD, D), (b, N_MOD, D))
    tab = jnp.stack([ctx, lat], axis=1)
    return jnp.pad(tab, ((0, 0), (0, 0), (0, SUB - N_MOD), (0, 0)))


def _mod_spec(combined):
    if combined:
        return pl.BlockSpec((1, 1, SUB, D), lambda b, i: (b, jnp.minimum(i, 1), 0, 0))
    return pl.BlockSpec((1, 1, SUB, D), lambda b, i: (b, 1, 0, 0))


def _front0_kernel(ctx_ref, x_ref, xp_ref, xn_ref, mod_ref, win_ref, qn_ref, kvn_ref, wqa_ref, wqb_ref, wuk_ref, wuv_ref,
                   cs_ref, sn_ref, mu_ref, w0_ref, w2_ref, a0_ref, a2_ref, g2_ref, kk_ref, ka_ref, rk_ref, ones_ref,
                   q_ref, k_ref, v_ref, sf_ref, g_ref, bonus_ref):
    i = pl.program_id(1)
    has_prev, has_next = _halo_flags(i, pl.num_programs(1))
    m = mod_ref[0, 0]
    xt = jnp.where(i == 0, ctx_ref[0], x_ref[0])
    xe = jnp.concatenate([xt, xp_ref[0], xn_ref[0]], axis=0)
    h = xe * (1.0 + m[1:2]) + m[0:1]
    fe = _bdot(h, win_ref[...])
    f = fe[:TM]
    o_kv = MLA_Q_RANK
    o_kr = o_kv + MLA_KV_RANK
    o_rw = o_kr + 2 * HEAD_PAD
    prow = jnp.where(has_prev, fe[TM + SUB - 1:TM + SUB, o_rw:], 0.0)
    nrow = jnp.where(has_next, fe[TM + SUB:TM + SUB + 1, o_rw:], 0.0)
    _rw_features(f[:, o_rw:], prow, nrow, mu_ref, w0_ref, w2_ref, a0_ref, a2_ref, g2_ref, kk_ref, ka_ref, rk_ref,
                 ones_ref, sf_ref, g_ref, bonus_ref)
    fq = f[:, :o_kv]
    fkv = f[:, o_kv:o_kr]
    fkr = f[:, o_kr:o_kr + HEAD_PAD]
    fkr_rot = f[:, o_kr + HEAD_PAD:o_rw]
    qn = fq * lax.rsqrt(jnp.mean(fq * fq, axis=-1, keepdims=True) + RMS_EPS) * qn_ref[...]
    ckv = fkv * lax.rsqrt(jnp.mean(fkv * fkv, axis=-1, keepdims=True) + RMS_EPS) * kvn_ref[...]
    qn = qn.astype(BF16)
    ckv = ckv.astype(BF16)
    qa = jnp.dot(qn, wqa_ref[...], preferred_element_type=F32)
    qb = jnp.dot(qn, wqb_ref[...], preferred_element_type=F32)
    kn = jnp.dot(ckv, wuk_ref[...], preferred_element_type=F32)
    cs = cs_ref[...]
    sn = sn_ref[...]
    kr = fkr * cs + fkr_rot * sn
    for hd in range(HEADS):
        sl = slice(hd * HEAD_PAD, (hd + 1) * HEAD_PAD)
        q_ref[0, :, sl] = ((qa[:, sl] * cs + qb[:, sl] * sn) * (ATT_SCALE * LOG2_E)).astype(BF16)
        k_ref[0, :, sl] = (kn[:, sl] + kr).astype(BF16)
    v_ref[0] = jnp.dot(ckv, wuv_ref[...], preferred_element_type=F32).astype(BF16)


def _seg_specs(n):
    ctx_spec = pl.BlockSpec((1, TM, D), lambda bb, i: (bb, 0, 0))
    lat_spec = pl.BlockSpec((1, TM, D), lambda bb, i: (bb, jnp.maximum(i - 1, 0), 0))
    return ctx_spec, lat_spec


def _front0(ctx, x, modt, mla_consts, cs, sn, rw_consts):
    b, n, _ = x.shape
    t = n + TM
    row = lambda w: pl.BlockSpec((1, TM, w), lambda bb, i: (bb, i, 0))
    ctx_spec, lat_spec = _seg_specs(n)
    hb = TM // SUB
    prev = pl.BlockSpec((1, SUB, D), lambda bb, i: (bb, jnp.maximum((i - 1) * hb - 1, 0), 0))
    nxt = pl.BlockSpec((1, SUB, D), lambda bb, i: (bb, jnp.minimum(i * hb, n // SUB - 1), 0))
    rope = pl.BlockSpec((TM, HEAD_PAD), lambda bb, i: (i, 0))
    return pl.pallas_call(
        _front0_kernel,
        grid=(b, t // TM),
        in_specs=[ctx_spec, lat_spec, prev, nxt, _mod_spec(True)] + [_weight_spec(c.shape) for c in mla_consts]
                 + [rope, rope] + [_weight_spec(c.shape) for c in rw_consts],
        out_specs=[row(HEADS * HEAD_PAD), row(HEADS * HEAD_PAD), row(HEADS * MLA_V),
                   pl.BlockSpec((1, 9, HEADS * PITCH_F, TM), lambda bb, i: (bb, 0, 0, i)), row(RW_W), row(RW_W)],
        out_shape=[jax.ShapeDtypeStruct((b, t, HEADS * HEAD_PAD), BF16),
                   jax.ShapeDtypeStruct((b, t, HEADS * HEAD_PAD), BF16),
                   jax.ShapeDtypeStruct((b, t, HEADS * MLA_V), BF16),
                   jax.ShapeDtypeStruct((b, 9, HEADS * PITCH_F, t), F32),
                   jax.ShapeDtypeStruct((b, t, RW_W), F32),
                   jax.ShapeDtypeStruct((b, t, RW_W), F32)],
        compiler_params=_params(("parallel", "parallel")),
        name="l0_front",
    )(ctx, x, x, x, modt, *mla_consts, cs, sn, *rw_consts)


def _att_kernel(q_ref, k_ref, v_ref, o_ref, *, n_ctx, n_all):
    i = pl.program_id(2)

    def attend(nk):
        lane = lax.broadcasted_iota(jnp.int32, (nk, 2 * MLA_V), 1)
        for pair in range(ATT_HEADS // 2):
            v = v_ref[0, :nk, pair * 2 * MLA_V:(pair + 1) * 2 * MLA_V]
            acc = None
            for hh in range(2):
                hd = 2 * pair + hh
                sl = slice(hd * HEAD_PAD, (hd + 1) * HEAD_PAD)
                q = q_ref[0, :, sl]
                k = k_ref[0, :nk, sl]
                s = lax.dot_general(q, k, (((1,), (1,)), ((), ())), preferred_element_type=F32)
                p = jnp.exp2(s - jnp.max(s, axis=-1, keepdims=True))
                l = jnp.sum(p, axis=-1, keepdims=True)
                keep = (lane < MLA_V) if hh == 0 else (lane >= MLA_V)
                vh = jnp.where(keep, v, jnp.zeros_like(v))
                o = jnp.dot(p.astype(BF16), vh, preferred_element_type=F32) / l
                acc = o if acc is None else acc + o
            o_ref[0, :, pair * 2 * MLA_V:(pair + 1) * 2 * MLA_V] = acc.astype(o_ref.dtype)

    @pl.when(i == 0)
    def _():
        attend(n_ctx)

    @pl.when(i > 0)
    def _():
        attend(n_all)


def _attention(q, k, v):
    b, t, _ = q.shape
    return pl.pallas_call(
        functools.partial(_att_kernel, n_ctx=TM, n_all=t),
        grid=(b, HEADS // ATT_HEADS, t // TM),
        in_specs=[pl.BlockSpec((1, TM, ATT_HEADS * HEAD_PAD), lambda bb, hp, i: (bb, i, hp)),
                  pl.BlockSpec((1, t, ATT_HEADS * HEAD_PAD), lambda bb, hp, i: (bb, 0, hp)),
                  pl.BlockSpec((1, t, ATT_HEADS * MLA_V), lambda bb, hp, i: (bb, 0, hp))],
        out_specs=pl.BlockSpec((1, TM, ATT_HEADS * MLA_V), lambda bb, hp, i: (bb, i, hp)),
        out_shape=jax.ShapeDtypeStruct((b, t, HEADS * MLA_V), BF16),
        compiler_params=_params(("parallel", "parallel", "parallel")),
        name="l0_attention",
    )(q, k, v)


def _halo_flags(i, n_tiles):
    return i >= 2, jnp.logical_and(i >= 1, i < n_tiles - 1)


def _rw_features(f, prow, nrow, mu_ref, w0_ref, w2_ref, a0_ref, a2_ref, g2_ref,
                 kk_ref, ka_ref, rk_ref, ones_ref, sf_ref, g_ref, bonus_ref):
    rid = lax.broadcasted_iota(jnp.int32, (TM, 1), 0)
    prev = jnp.where(rid == 0, prow, pltpu.roll(f, 1, axis=0))
    nxt = jnp.where(rid == TM - 1, nrow, pltpu.roll(f, TM - 1, axis=0))
    f = f + mu_ref[...] * (0.5 * (prev + nxt) - f)
    r = f[:, 0:RW_W]
    k = f[:, RW_W:2 * RW_W]
    v = f[:, 2 * RW_W:3 * RW_W]
    o = 3 * RW_W
    wl = f[:, o:o + 2 * LORA_W]
    al = f[:, o + 2 * LORA_W:o + 2 * LORA_W + 2 * LORA_A]
    gl = f[:, o + 2 * LORA_W + 2 * LORA_A:]
    w_raw = w0_ref[...] + _bdot(jnp.tanh(wl), w2_ref[...])
    decay = jnp.exp(-EXP_NEG_HALF * _sigmoid(w_raw))
    a = _sigmoid(a0_ref[...] + _bdot(al, a2_ref[...]))
    g_ref[0] = _bdot(_sigmoid(gl), g2_ref[...])
    ones_bd = ones_ref[...]
    kk = k * kk_ref[...]
    kk = kk * lax.rsqrt(_head_sum(kk * kk, ones_bd) + 1e-12)
    ka = ka_ref[...]
    kd = [k * (1.0 + (a[:, d * RW_W:(d + 1) * RW_W] - 1.0) * ka) for d in range(2)]
    bonus_ref[0] = _head_sum(r * (kd[0] + kd[1]) * rk_ref[...], ones_bd) * v
    parts = [r, kk, v, decay[:, :RW_W], kd[0], kk * a[:, :RW_W],
             decay[:, RW_W:], kd[1], kk * a[:, RW_W:]]
    pad = jnp.zeros((PITCH_F - RW_HEAD, TM), F32)
    for n, p in enumerate(parts):
        for hp in range(HEADS // 2):
            for half in range(TM // 128):
                pt = p[half * 128:(half + 1) * 128, hp * 128:(hp + 1) * 128].T
                for h2 in range(2):
                    row0 = (2 * hp + h2) * PITCH_F
                    sf_ref[0, n, row0:row0 + RW_HEAD, half * 128:(half + 1) * 128] = pt[h2 * RW_HEAD:(h2 + 1) * RW_HEAD]
        for hd in range(HEADS):
            sf_ref[0, n, hd * PITCH_F + RW_HEAD:(hd + 1) * PITCH_F, :] = pad


def _time_mirror(i, n_ctx_tiles, n_tiles):
    return jnp.where(i < n_ctx_tiles, n_ctx_tiles - 1 - i, n_tiles + n_ctx_tiles - 1 - i)


def _reverse_lanes(x, j3):
    hi = x.astype(BF16)
    r1 = x - hi.astype(F32)
    mid = r1.astype(BF16)
    lo = (r1 - mid.astype(F32)).astype(BF16)
    return jnp.dot(jnp.concatenate([hi, mid, lo], axis=1), j3, preferred_element_type=F32)


def _relayout_in_kernel(xa_ref, xb_ref, j3_ref, o_ref, *, step_major):
    nb = xa_ref.shape[0]
    tr = xa_ref.shape[3]
    half = nb * HEADS

    def gather(x_ref, f):
        return jnp.concatenate([x_ref[b, 0, pl.ds(f, HEADS, stride=PITCH_F), :] for b in range(nb)], axis=0)

    for f0 in range(0, RW_HEAD, RELAY_FB):
        fs = range(f0, f0 + RELAY_FB)
        rev = _reverse_lanes(jnp.concatenate([gather(xb_ref, f) for f in fs], axis=0), j3_ref[...])
        for n, f in enumerate(fs):
            mt = jnp.concatenate([gather(xa_ref, f), rev[n * half:(n + 1) * half]], axis=0).T
            if step_major:
                o_ref[pl.ds(f, tr, stride=PITCH_F), :] = mt
            else:
                o_ref[0, f] = mt
    if step_major:
        for f in range(RW_HEAD, PITCH_F):
            o_ref[pl.ds(f, tr, stride=PITCH_F), :] = jnp.zeros((tr, 2 * half), F32)


def _relayout_in(sf, j3):
    b, _, rows, t = sf.shape
    chains = 2 * b * HEADS
    nt = t // RELAY_T
    nc = TM // RELAY_T
    mirror = lambda i: _time_mirror(i, nc, nt)
    blk = (b, 1, rows, RELAY_T)
    dirdep = lambda p: jnp.minimum(p // 2, 1)
    x = pl.pallas_call(
        functools.partial(_relayout_in_kernel, step_major=False),
        grid=(nt, 5),
        in_specs=[pl.BlockSpec(blk, lambda i, p: (0, p + dirdep(p), 0, i)),
                  pl.BlockSpec(blk, lambda i, p: (0, p + 4 * dirdep(p), 0, mirror(i))),
                  _const_spec(j3.shape)],
        out_specs=pl.BlockSpec((1, RW_HEAD, RELAY_T, chains), lambda i, p: (p, 0, i, 0)),
        out_shape=jax.ShapeDtypeStruct((5, RW_HEAD, t, chains), F32),
        compiler_params=_params(("parallel", "parallel")),
        name="l0_rwkv_relayout_in",
    )(sf, sf, j3)
    v = pl.pallas_call(
        functools.partial(_relayout_in_kernel, step_major=True),
        grid=(nt,),
        in_specs=[pl.BlockSpec(blk, lambda i: (0, 2, 0, i)),
                  pl.BlockSpec(blk, lambda i: (0, 2, 0, mirror(i))),
                  _const_spec(j3.shape)],
        out_specs=pl.BlockSpec((RELAY_T * PITCH_F, chains), lambda i: (i, 0)),
        out_shape=jax.ShapeDtypeStruct((t * PITCH_F, chains), F32),
        compiler_params=_params(("parallel",)),
        name="l0_rwkv_relayout_v",
    )(sf, sf, j3)
    return x, v


def _rwscan_kernel(x_ref, v_ref, y_ref, s_ref, st_ref):
    tc = x_ref.shape[2]
    chains = x_ref.shape[3]
    n_groups = tc // SUB

    @pl.when(pl.program_id(0) == 0)
    def _():
        s_ref[...] = jnp.zeros_like(s_ref)

    parts = RW_HEAD // SUB
    zeros = tuple(jnp.zeros((SUB, chains), F32) for _ in range(parts))

    def row(a, j, u):
        return jnp.broadcast_to(st_ref[a, j, u:u + 1, :], (SUB, chains))

    def s_at(j, g):
        return s_ref.at[j, g * SUB:(g + 1) * SUB, :]

    sa0 = list(zeros)
    for j in range(RW_HEAD):
        kk = jnp.broadcast_to(x_ref[1, j, 0:1, :], (SUB, chains))
        for g in range(parts):
            sa0[g] = sa0[g] + s_at(j, g)[...] * kk

    def group(gi, sa):
        cur = pl.multiple_of(gi * SUB, SUB)
        nxt = pl.multiple_of(jnp.minimum(gi + 1, n_groups - 1) * SUB, SUB)
        for j in range(RW_HEAD):
            for a in range(5):
                st_ref[a, j, 0:SUB, :] = x_ref[a, j, pl.ds(cur, SUB), :]
            st_ref[1, j, SUB:2 * SUB, :] = x_ref[1, j, pl.ds(nxt, SUB), :]
        for u in range(SUB):
            base = pl.multiple_of((gi * SUB + u) * PITCH_F, SUB)
            v = v_ref[pl.ds(base, RW_HEAD), :]
            y = list(zeros)
            sa_next = list(zeros)
            for j in range(RW_HEAD):
                r, kk_next, w, k, kka = row(0, j, u), row(1, j, u + 1), row(2, j, u), row(3, j, u), row(4, j, u)
                for g in range(parts):
                    sj = s_at(j, g)[...] * w - sa[g] * kka + v[g * SUB:(g + 1) * SUB] * k
                    s_at(j, g)[...] = sj
                    y[g] = y[g] + sj * r
                    sa_next[g] = sa_next[g] + sj * kk_next
            y_ref[pl.ds(base, RW_HEAD), :] = jnp.concatenate(y, axis=0)
            y_ref[pl.ds(base + RW_HEAD, PITCH_F - RW_HEAD), :] = zeros[0]
            sa = tuple(sa_next)
        return sa

    lax.fori_loop(0, n_groups, group, tuple(sa0))


def _rwscan(x, v):
    _, _, t, chains = x.shape
    tc = SCAN_TC if t % SCAN_TC == 0 else SCAN_TC_SMALL
    return pl.pallas_call(
        _rwscan_kernel,
        grid=(t // tc,),
        in_specs=[pl.BlockSpec((5, RW_HEAD, tc, chains), lambda i: (0, 0, i, 0)),
                  pl.BlockSpec((tc * PITCH_F, chains), lambda i: (i, 0))],
        out_specs=pl.BlockSpec((tc * PITCH_F, chains), lambda i: (i, 0)),
        out_shape=jax.ShapeDtypeStruct((t * PITCH_F, chains), F32),
        scratch_shapes=[pltpu.VMEM((RW_HEAD, RW_HEAD, chains), F32),
                        pltpu.VMEM((5, RW_HEAD, 2 * SUB, chains), F32)],
        compiler_params=_params(("arbitrary",)),
        name="l0_rwkv_scan",
    )(x, v)


def _relayout_out_kernel(yf_ref, yr_ref, j3_ref, o_ref, q_ref):
    nb = o_ref.shape[0]
    tr = o_ref.shape[1]
    chains = yf_ref.shape[1]
    half = chains // 2
    pitch_c = chains + SUB
    fwd = lax.broadcasted_iota(jnp.int32, (tr, chains), 1) < half
    for i in range(RW_HEAD):
        rows = pl.ds(i, tr, stride=PITCH_F)
        mt = jnp.where(fwd, yf_ref[rows, :], yr_ref[rows, :]).T
        q_ref[i * pitch_c:i * pitch_c + half, :] = mt[:half]
        q_ref[i * pitch_c + half:i * pitch_c + chains, :] = _reverse_lanes(mt[half:], j3_ref[...])
    for b in range(nb):
        for hp in range(HEADS // 2):
            parts = []
            for h2 in range(2):
                c = b * HEADS + 2 * hp + h2
                parts.append(q_ref[pl.ds(c, RW_HEAD, stride=pitch_c), :]
                             + q_ref[pl.ds(half + c, RW_HEAD, stride=pitch_c), :])
            o_ref[b, :, hp * 128:(hp + 1) * 128] = jnp.concatenate(parts, axis=0).T


def _relayout_out(y, j3, b):
    chains = y.shape[1]
    t = y.shape[0] // PITCH_F
    nt = t // RELAY_T
    nc = TM // RELAY_T
    return pl.pallas_call(
        _relayout_out_kernel,
        grid=(nt,),
        in_specs=[pl.BlockSpec((RELAY_T * PITCH_F, chains), lambda i: (i, 0)),
                  pl.BlockSpec((RELAY_T * PITCH_F, chains), lambda i: (_time_mirror(i, nc, nt), 0)),
                  _const_spec(j3.shape)],
        out_specs=pl.BlockSpec((b, RELAY_T, RW_W), lambda i: (0, i, 0)),
        out_shape=jax.ShapeDtypeStruct((b, t, RW_W), F32),
        scratch_shapes=[pltpu.VMEM((RW_HEAD * (chains + SUB), RELAY_T), F32)],
        compiler_params=_params(("parallel",)),
        name="l0_rwkv_relayout_out",
    )(y, y, j3)


def _rwkv_scan_both(sf):
    k = jnp.arange(RELAY_T)
    anti = (k[:, None] + k[None, :] == RELAY_T - 1).astype(BF16)
    j3 = jnp.concatenate([anti, anti, anti], axis=0)
    y = _rwscan(*_relayout_in(sf, j3))
    return _relayout_out(y, j3, sf.shape[0])


def _mlp_hidden(h, w1_ref, w2_ref):
    acc = None
    for c in range(D_FF // MLP_CHUNK):
        u = jnp.dot(h, w1_ref[:, c * MLP_CHUNK:(c + 1) * MLP_CHUNK], preferred_element_type=F32)
        u = jnp.square(jnp.maximum(u, 0.0)).astype(BF16)
        part = jnp.dot(u, w2_ref[c * MLP_CHUNK:(c + 1) * MLP_CHUNK, :], preferred_element_type=F32)
        acc = part if acc is None else acc + part
    return acc


def _tail0_kernel(att_ref, ys_ref, bonus_ref, g_ref, ctx_ref, x_ref, mod_ref, mod1_ref, gnw_ref, gnb_ref, ones_ref,
                  woa_ref, wob_ref, l1g_ref, l1b_ref, w1_ref, w2_ref, l2g_ref, l2b_ref, win1_ref,
                  x2_ref, gate_ref, xr_ref):
    m = mod_ref[0, 0]
    ones_bd = ones_ref[...]
    y = ys_ref[0]
    mu = _head_sum(y, ones_bd) * (1.0 / RW_HEAD)
    yc = y - mu
    var = _head_sum(yc * yc, ones_bd) * (1.0 / RW_HEAD)
    yn = yc * lax.rsqrt(var + GN_EPS) * gnw_ref[...] + gnb_ref[...]
    rw = (yn + bonus_ref[0]) * g_ref[0]
    o = (jnp.dot(att_ref[0], woa_ref[...], preferred_element_type=F32)
         + _bdot(rw, wob_ref[...]))
    x0 = jnp.where(pl.program_id(1) == 0, ctx_ref[0], x_ref[0])
    x1 = _layer_norm(ALPHA * x0 + m[2:3] * o, l1g_ref[...], l1b_ref[...])
    acc = _mlp_hidden((x1 * (1.0 + m[4:5]) + m[3:4]).astype(BF16), w1_ref, w2_ref)
    x2 = _layer_norm(ALPHA * x1 + m[5:6] * acc, l2g_ref[...], l2b_ref[...])
    x2_ref[0] = x2
    m1 = mod1_ref[0, 0]
    f = _bdot(x2 * (1.0 + m1[1:2]) + m1[0:1], win1_ref[...])
    gate_ref[0] = f[:, :LRU_W]
    xr_ref[0] = f[:, LRU_W:]


def _tail0(att, ys, bonus, g, ctx, x, mod0, mod1, consts):
    b, n, _ = x.shape
    t = n + TM
    row = lambda w: pl.BlockSpec((1, TM, w), lambda bb, i: (bb, i, 0))
    ctx_spec, lat_spec = _seg_specs(n)
    return pl.pallas_call(
        _tail0_kernel,
        grid=(b, t // TM),
        in_specs=[row(RW_W), row(RW_W), row(RW_W), row(RW_W), ctx_spec, lat_spec, _mod_spec(True), _mod_spec(True)]
                 + [_weight_spec(c.shape) for c in consts],
        out_specs=[row(D), row(LRU_W), row(LRU_W)],
        out_shape=[jax.ShapeDtypeStruct((b, t, D), F32), jax.ShapeDtypeStruct((b, t, LRU_W), F32),
                   jax.ShapeDtypeStruct((b, t, LRU_W), F32)],
        compiler_params=_params(("parallel", "parallel")),
        name="l0_tail",
    )(att, ys, bonus, g, ctx, x, mod0, mod1, *consts)


def _lru_fwd_kernel(x_ref, xp_ref, xn_ref, cw_ref, cb_ref, wbd_ref, gb_ref, lam_ref,
                    hf_ref, a1_ref, u1_ref, a_scr, u_scr, c_ref):
    i = pl.program_id(0)
    n = pl.num_programs(0)
    nb, tl, _ = x_ref.shape
    rows = nb * tl
    nc = TM // tl
    pitch_b = tl + SUB

    @pl.when(i == 0)
    def _():
        c_ref[...] = jnp.zeros_like(c_ref)

    lam = lam_ref[...]
    nl = -lam
    softplus = jnp.maximum(nl, 0.0) + jnp.log1p(jnp.exp(-jnp.abs(nl)))
    gb = gb_ref[...]
    cw = cw_ref[...]
    tpos = jnp.bitwise_and(lax.broadcasted_iota(jnp.int32, (rows, 1), 0), tl - 1)
    seg_first = jnp.logical_or(i == 0, i == nc)
    seg_last = jnp.logical_or(i == nc - 1, i == n - 1)

    def halo(ref, r, edge):
        per_b = [jnp.broadcast_to(ref[b, r:r + 1, :], (tl, LRU_W)) for b in range(nb)]
        return jnp.where(edge, 0.0, jnp.concatenate(per_b, axis=0))

    x = x_ref[...].reshape(rows, LRU_W)
    p1 = halo(xp_ref, SUB - 1, seg_first)
    p2 = halo(xp_ref, SUB - 2, seg_first)
    n1 = halo(xn_ref, 0, seg_last)
    xm1 = jnp.where(tpos == 0, p1, pltpu.roll(x, 1, axis=0))
    xm2 = jnp.where(tpos == 0, p2, jnp.where(tpos == 1, p1, pltpu.roll(x, 2, axis=0)))
    xp1 = jnp.where(tpos == tl - 1, n1, pltpu.roll(x, rows - 1, axis=0))
    xc = cw[0:1] * xm2 + cw[1:2] * xm1 + cw[2:3] * x + cw[3:4] * xp1 + cb_ref[...]
    for blk in range(LRU_BLOCKS):
        sl = slice(blk * LRU_BLOCK, (blk + 1) * LRU_BLOCK)
        xb = xc[:, sl]
        z = _bdot(xb, wbd_ref[blk])
        for d in range(2):
            zr = z[:, 2 * d * LRU_BLOCK:(2 * d + 1) * LRU_BLOCK] + gb[2 * d:2 * d + 1, sl]
            zi = z[:, (2 * d + 1) * LRU_BLOCK:(2 * d + 2) * LRU_BLOCK] + gb[2 * d + 1:2 * d + 2, sl]
            log_a = -LRU_C * _sigmoid(zr) * softplus[d:d + 1, sl]
            a = jnp.exp(log_a)
            u = jnp.sqrt(-jnp.tanh(log_a) * (a * a + 1.0)) * (_sigmoid(zi) * xb)
            for b in range(nb):
                a_scr[d, blk, b * pitch_b:b * pitch_b + tl, :] = a[b * tl:(b + 1) * tl]
                u_scr[d, blk, b * pitch_b:b * pitch_b + tl, :] = u[b * tl:(b + 1) * tl]
    h = [c_ref[:, blk * LRU_BLOCK:(blk + 1) * LRU_BLOCK] for blk in range(LRU_BLOCKS)]
    pad = jnp.zeros((SUB, LRU_BLOCK), F32)
    for t in range(tl):
        for blk in range(LRU_BLOCKS):
            step_rows = pl.ds(t, nb, stride=pitch_b)
            dst = slice(t * PITCH_H + blk * SUB, t * PITCH_H + blk * SUB + nb)
            h[blk] = a_scr[0, blk, step_rows, :] * h[blk] + u_scr[0, blk, step_rows, :]
            hf_ref[dst, :] = h[blk]
            a1_ref[dst, :] = a_scr[1, blk, step_rows, :]
            u1_ref[dst, :] = u_scr[1, blk, step_rows, :]
        for ref in (hf_ref, a1_ref, u1_ref):
            ref[t * PITCH_H + LRU_BLOCKS * SUB:(t + 1) * PITCH_H, :] = pad
    for blk in range(LRU_BLOCKS):
        c_ref[:, blk * LRU_BLOCK:(blk + 1) * LRU_BLOCK] = h[blk]


def _lru_bwd_kernel(a_ref, u_ref, h_ref, c_ref, *, nb):
    tb = a_ref.shape[0] // PITCH_H

    @pl.when(pl.program_id(0) == 0)
    def _():
        c_ref[...] = jnp.zeros_like(c_ref)

    def step(s, h):
        base = pl.multiple_of((tb - 1 - s) * PITCH_H, SUB)
        out = []
        for blk in range(LRU_BLOCKS):
            rows = pl.ds(base + blk * SUB, nb)
            hb = a_ref[rows, :] * h[blk] + u_ref[rows, :]
            h_ref[rows, :] = hb
            out.append(hb)
        h_ref[pl.ds(base + LRU_BLOCKS * SUB, SUB), :] = jnp.zeros((SUB, LRU_BLOCK), F32)
        return tuple(out)

    h0 = tuple(c_ref[blk] for blk in range(LRU_BLOCKS))
    h = lax.fori_loop(0, tb, step, h0)
    for blk in range(LRU_BLOCKS):
        c_ref[blk] = h[blk]


def _lru(xr, cw, cb, wbd, gb, lam):
    b, t, _ = xr.shape
    assert b <= SUB
    n = t // LRU_TL
    hb = LRU_TL // SUB
    last = t // SUB - 1
    consts = (cw, cb, wbd, gb, lam)
    hspec = pl.BlockSpec((LRU_TL * PITCH_H, LRU_BLOCK), lambda i: (i, 0))
    coef = jax.ShapeDtypeStruct((t * PITCH_H, LRU_BLOCK), F32)
    scr = pltpu.VMEM((2, LRU_BLOCKS, b * (LRU_TL + SUB), LRU_BLOCK), F32)
    hf, a1, u1 = pl.pallas_call(
        _lru_fwd_kernel,
        grid=(n,),
        in_specs=[pl.BlockSpec((b, LRU_TL, LRU_W), lambda i: (0, i, 0)),
                  pl.BlockSpec((b, SUB, LRU_W), lambda i: (0, jnp.maximum(i * hb - 1, 0), 0)),
                  pl.BlockSpec((b, SUB, LRU_W), lambda i: (0, jnp.minimum((i + 1) * hb, last), 0))]
                 + [_const_spec(c.shape) for c in consts],
        out_specs=[hspec, hspec, hspec],
        out_shape=[coef, coef, coef],
        scratch_shapes=[scr, scr, pltpu.VMEM((b, LRU_W), F32)],
        compiler_params=_params(("arbitrary",)),
        name="l1_lru_fwd",
    )(xr, xr, xr, *consts)
    nt = t // LRU_TB
    nc = TM // LRU_TB
    bspec = pl.BlockSpec((LRU_TB * PITCH_H, LRU_BLOCK), lambda i: (_time_mirror(i, nc, nt), 0))
    hr = pl.pallas_call(
        functools.partial(_lru_bwd_kernel, nb=b),
        grid=(nt,),
        in_specs=[bspec, bspec],
        out_specs=bspec,
        out_shape=coef,
        scratch_shapes=[pltpu.VMEM((LRU_BLOCKS, b, LRU_BLOCK), F32)],
        compiler_params=_params(("arbitrary",)),
        name="l1_lru_bwd",
    )(a1, u1)
    return hf, hr


def _tail1_kernel(gate_ref, hf_ref, hr_ref, x_ref, mod_ref, w_ref, l1g_ref, l1b_ref, w1_ref, w2_ref, l2g_ref, l2b_ref,
                  o_ref):
    nb, tl, _ = gate_ref.shape
    per_b = []
    for b in range(nb):
        cols = []
        for blk in range(LRU_BLOCKS):
            rows = pl.ds(blk * SUB + b, tl, stride=PITCH_H)
            cols.append(hf_ref[rows, :] + hr_ref[rows, :])
        per_b.append(jnp.concatenate(cols, axis=1))
    h = jnp.concatenate(per_b, axis=0)
    gate = gate_ref[...].reshape(nb * tl, LRU_W)
    gelu = 0.5 * gate * (1.0 + jnp.tanh(math.sqrt(2.0 / math.pi) * (gate + 0.044715 * gate * gate * gate)))
    o = _bdot(gelu * h, w_ref[...])
    mods = [mod_ref[b, 0] for b in range(nb)]
    x1 = [_layer_norm(ALPHA * x_ref[b] + mods[b][2:3] * o[b * tl:(b + 1) * tl], l1g_ref[...], l1b_ref[...])
          for b in range(nb)]
    hm = jnp.concatenate([x1[b] * (1.0 + mods[b][4:5]) + mods[b][3:4] for b in range(nb)], axis=0)
    acc = _mlp_hidden(hm.astype(BF16), w1_ref, w2_ref)
    for b in range(nb):
        z = ALPHA * x1[b] + mods[b][5:6] * acc[b * tl:(b + 1) * tl]
        o_ref[b] = _layer_norm(z, l2g_ref[...], l2b_ref[...])


def _tail1(gate, hf, hr, xc, modt, consts):
    b, t, _ = xc.shape
    off = TM // LRU_TL
    nt = t // LRU_TL - off
    lat = lambda wd: pl.BlockSpec((b, LRU_TL, wd), lambda i: (0, i + off, 0))
    hspec = pl.BlockSpec((LRU_TL * PITCH_H, LRU_BLOCK), lambda i: (i + off, 0))
    return pl.pallas_call(
        _tail1_kernel,
        grid=(nt,),
        in_specs=[lat(LRU_W), hspec, hspec, lat(D),
                  pl.BlockSpec((b, 1, SUB, D), lambda i: (0, 1, 0, 0))]
                 + [_weight_spec(c.shape) for c in consts],
        out_specs=pl.BlockSpec((b, LRU_TL, D), lambda i: (0, i, 0)),
        out_shape=jax.ShapeDtypeStruct((b, nt * LRU_TL, D), F32),
        compiler_params=_params(("parallel",)),
        name="l1_tail",
    )(gate, hf, hr, xc, modt, *consts)


def _rot_cols(w):
    ws = w.reshape(w.shape[:-1] + (2, 2, ROPE_AXIS // 2))
    return jnp.stack([-ws[..., 1, :], ws[..., 0, :]], axis=-2).reshape(w.shape)


def _rope_tables(n, n_ctx):
    rows_n = n // GRID_W
    rows = jnp.repeat(jnp.arange(rows_n, dtype=F32), GRID_W)
    cols = jnp.tile(jnp.arange(GRID_W, dtype=F32), rows_n)
    inv_freq = ROPE_THETA ** (-jnp.arange(0, ROPE_AXIS, 2, dtype=F32) / ROPE_AXIS)
    ang_r = rows[:, None] * inv_freq
    ang_c = cols[:, None] * inv_freq
    ang = jnp.concatenate([ang_r, ang_r, ang_c, ang_c], axis=-1)
    cos = jnp.concatenate([jnp.ones((n_ctx, MLA_ROPE), F32), jnp.cos(ang)], axis=0)
    sin = jnp.concatenate([jnp.zeros((n_ctx, MLA_ROPE), F32), jnp.sin(ang)], axis=0)
    t = n + n_ctx
    cs = jnp.concatenate([jnp.ones((t, MLA_NOPE), F32), cos, jnp.zeros((t, 32), F32)], axis=-1)
    sn = jnp.concatenate([jnp.zeros((t, MLA_NOPE), F32), sin, jnp.zeros((t, 32), F32)], axis=-1)
    return cs, sn


def _block_diag2(w):
    z = jnp.zeros_like(w[0])
    return jnp.concatenate([jnp.concatenate([w[0], z], axis=1), jnp.concatenate([z, w[1]], axis=1)], axis=0)


def kernel(x, c, ctx, c_ctx, l0_mod_w, l0_mod_b, l0_w_in, l0_mla_q_norm, l0_mla_w_uq, l0_mla_kv_norm, l0_mla_w_uk, l0_mla_w_uv, l0_rwkv_mu, l0_rwkv_w0, l0_rwkv_w2, l0_rwkv_a0, l0_rwkv_a2, l0_rwkv_g2, l0_rwkv_k_k, l0_rwkv_k_a, l0_rwkv_r_k, l0_rwkv_gn_w, l0_rwkv_gn_b, l0_w_out, l0_ln1_g, l0_ln1_b, l0_mlp_w1, l0_mlp_w2, l0_ln2_g, l0_ln2_b, l1_mod_w, l1_mod_b, l1_w_in, l1_conv_w, l1_conv_b, l1_lru_ga_w, l1_lru_ga_b, l1_lru_gx_w, l1_lru_gx_b, l1_lru_lambda, l1_w_out, l1_ln1_g, l1_ln1_b, l1_mlp_w1, l1_mlp_w2, l1_ln2_g, l1_ln2_b):
    b, n, _ = x.shape
    n_ctx = ctx.shape[1]
    assert n_ctx == TM and n % TM == 0 and x.shape[2] == D
    row = lambda v: v.reshape(1, -1)

    mod0 = _mod_table(c, c_ctx, l0_mod_w, l0_mod_b)
    mod1 = _mod_table(c, c_ctx, l1_mod_w, l1_mod_b)

    o_kv = MLA_Q_RANK
    o_kr = o_kv + MLA_KV_RANK
    o_rw = o_kr + MLA_ROPE
    w_kr = l0_w_in[:, o_kr:o_rw]
    zl = jnp.zeros((D, MLA_NOPE), F32)
    zr = jnp.zeros((D, HEAD_PAD - MLA_NOPE - MLA_ROPE), F32)
    win0 = jnp.concatenate([l0_w_in[:, :o_kr], zl, w_kr, zr, zl, _rot_cols(w_kr), zr,
                            l0_w_in[:, o_rw:]], axis=1).astype(BF16)
    wq = l0_mla_w_uq.reshape(MLA_Q_RANK, HEADS, MLA_NOPE + MLA_ROPE)
    q_nope, q_rope = wq[..., :MLA_NOPE], wq[..., MLA_NOPE:]
    zq = jnp.zeros((MLA_Q_RANK, HEADS, 32), F32)
    wqa = jnp.concatenate([q_nope, q_rope, zq], axis=-1).reshape(MLA_Q_RANK, HEADS * HEAD_PAD).astype(BF16)
    wqb = jnp.concatenate([jnp.zeros_like(q_nope), _rot_cols(q_rope), zq],
                          axis=-1).reshape(MLA_Q_RANK, HEADS * HEAD_PAD).astype(BF16)
    wk = l0_mla_w_uk.reshape(MLA_KV_RANK, HEADS, MLA_NOPE)
    wuk = jnp.concatenate([wk, jnp.zeros_like(wk)], axis=-1).reshape(MLA_KV_RANK, HEADS * HEAD_PAD).astype(BF16)
    wuv = l0_mla_w_uv.astype(BF16)
    cs, sn = _rope_tables(n, n_ctx)
    hid = jnp.arange(RW_W) // RW_HEAD
    ones_bd = (hid[:, None] == hid[None, :]).astype(BF16)

    mla_consts = (win0, row(l0_mla_q_norm), row(l0_mla_kv_norm), wqa, wqb, wuk, wuv)
    rw_consts = (row(l0_rwkv_mu), row(l0_rwkv_w0), _block_diag2(l0_rwkv_w2).astype(BF16), row(l0_rwkv_a0),
                 _block_diag2(l0_rwkv_a2).astype(BF16), l0_rwkv_g2.astype(BF16), row(l0_rwkv_k_k),
                 row(l0_rwkv_k_a), row(l0_rwkv_r_k), ones_bd)
    q, k, v, sf, g, bonus = _front0(ctx, x, mod0, mla_consts, cs, sn, rw_consts)
    att = _attention(q, k, v)
    ys = _rwkv_scan_both(sf)
    wo = l0_w_out.astype(BF16)
    tail0_consts = (row(l0_rwkv_gn_w), row(l0_rwkv_gn_b), ones_bd, wo[:HEADS * MLA_V], wo[HEADS * MLA_V:],
                    row(l0_ln1_g), row(l0_ln1_b), l0_mlp_w1.astype(BF16), l0_mlp_w2.astype(BF16),
                    row(l0_ln2_g), row(l0_ln2_b), l1_w_in.astype(BF16))
    xc, gate, xr = _tail0(att, ys, bonus, g, ctx, x, mod0, mod1, tail0_consts)

    wbd = jnp.concatenate([l1_lru_ga_w[0], l1_lru_gx_w[0], l1_lru_ga_w[1], l1_lru_gx_w[1]], axis=-1).astype(BF16)
    gb = jnp.stack([l1_lru_ga_b[0], l1_lru_gx_b[0], l1_lru_ga_b[1], l1_lru_gx_b[1]])
    hf, hr = _lru(xr, l1_conv_w, row(l1_conv_b), wbd, gb, l1_lru_lambda)
    tail1_consts = (l1_w_out.astype(BF16), row(l1_ln1_g), row(l1_ln1_b), l1_mlp_w1.astype(BF16),
                    l1_mlp_w2.astype(BF16), row(l1_ln2_g), row(l1_ln2_b))
    return _tail1(gate, hf, hr, xc, mod1, tail1_consts)
```

```python
import functools
import math

import jax
import jax.numpy as jnp
from jax import lax
from jax.experimental import pallas as pl
from jax.experimental.pallas import tpu as pltpu

F32 = jnp.float32
BF16 = jnp.bfloat16

D = 1024
DEPTH = 2
N_MOD = 6
ALPHA = (2.0 * DEPTH) ** 0.25
LN_EPS = 1e-5
RMS_EPS = 1e-6

HEADS = 8
MLA_NOPE = 64
MLA_ROPE = 32
MLA_V = 64
MLA_Q_RANK = 384
MLA_KV_RANK = 256
ROPE_AXIS = MLA_ROPE // 2
ROPE_THETA = 10000.0
GRID_W = 64
ATT_SCALE = (MLA_NOPE + MLA_ROPE) ** -0.5
LOG2_E = math.log2(math.e)
HEAD_PAD = 128
ATT_HEADS = 8

RW_HEAD = 64
RW_W = HEADS * RW_HEAD
LORA_W = 64
LORA_A = 64
LORA_G = 128
RW_IN = 3 * RW_W + 2 * LORA_W + 2 * LORA_A + LORA_G
GN_EPS = 64e-5
EXP_NEG_HALF = math.exp(-0.5)

LRU_W = D
LRU_BLOCKS = 8
LRU_BLOCK = LRU_W // LRU_BLOCKS
LRU_C = 8.0
D_FF = 4 * D

TM = 256
SUB = 8
SCAN_TC = 72
SCAN_TC_SMALL = 32
RELAY_T = 128
RELAY_FB = 4
MLP_CHUNK = 1024
PITCH_F = RW_HEAD + SUB
PITCH_H = LRU_BLOCKS * SUB + SUB
LRU_TL = 32
LRU_TB = 128
VMEM_LIMIT = 56 * 1024 * 1024

IN0_COLS = MLA_Q_RANK + MLA_KV_RANK + 2 * HEAD_PAD + RW_IN


def _params(sem):
    return pltpu.CompilerParams(dimension_semantics=sem, vmem_limit_bytes=VMEM_LIMIT)


def _bdot(a, w):
    return jnp.dot(a.astype(BF16), w, preferred_element_type=F32)


def _sigmoid(x):
    return 0.5 * jnp.tanh(0.5 * x) + 0.5


def _layer_norm(z, g, b):
    mu = jnp.mean(z, axis=-1, keepdims=True)
    zc = z - mu
    var = jnp.mean(zc * zc, axis=-1, keepdims=True)
    return zc * lax.rsqrt(var + LN_EPS) * g + b


def _head_sum(x, ones_bd):
    hi = x.astype(BF16)
    lo = (x - hi.astype(F32)).astype(BF16)
    return (jnp.dot(hi, ones_bd, preferred_element_type=F32)
            + jnp.dot(lo, ones_bd, preferred_element_type=F32))


def _const_spec(shape):
    nd = len(shape)
    return pl.BlockSpec(shape, lambda *_: (0,) * nd)


def _weight_spec(shape):
    nd = len(shape)
    return pl.BlockSpec(shape, lambda *_: (0,) * nd, pipeline_mode=pl.Buffered(1))


def _mod_kernel(c_ref, w_ref, b_ref, o_ref):
    c = c_ref[...]
    s = c * _sigmoid(c)
    o_ref[...] = jnp.dot(s, w_ref[...], precision=lax.Precision.HIGHEST,
                         preferred_element_type=F32) + b_ref[...]


def _mod_table(c, c_ctx, mod_w, mod_b):
    b = c.shape[0]
    rows = 16
    cc = jnp.zeros((rows, D), F32).at[:b].set(c).at[b].set(c_ctx)
    tn = 1024
    out = pl.pallas_call(
        _mod_kernel,
        grid=(N_MOD * D // tn,),
        in_specs=[pl.BlockSpec((rows, D), lambda j: (0, 0)),
                  pl.BlockSpec((D, tn), lambda j: (0, j)),
                  pl.BlockSpec((1, tn), lambda j: (0, j))],
        out_specs=pl.BlockSpec((rows, tn), lambda j: (0, j)),
        out_shape=jax.ShapeDtypeStruct((rows, N_MOD * D), F32),
        compiler_params=_params(("arbitrary",)),
        name="mod_proj",
    )(cc, mod_w, mod_b.reshape(1, -1))
    lat = out[:b].reshape(b, N_MOD, D)
    ctx = jnp.broadcast_to(out[b].reshape(1, N_MOD, D), (b, N_MOD, D))
    tab = jnp.stack([ctx, lat], axis=1)
    return jnp.pad(tab, ((0, 0), (0, 0), (0, SUB - N_MOD), (0, 0)))


def _mod_spec(combined):
    if combined:
        return pl.BlockSpec((1, 1, SUB, D), lambda b, i: (b, jnp.minimum(i, 1), 0, 0))
    return pl.BlockSpec((1, 1, SUB, D), lambda b, i: (b, 1, 0, 0))


def _front0_kernel(ctx_ref, x_ref, xp_ref, xn_ref, mod_ref, win_ref, qn_ref, kvn_ref, wqa_ref, wqb_ref, wuk_ref, wuv_ref,
                   cs_ref, sn_ref, mu_ref, w0_ref, w2_ref, a0_ref, a2_ref, g2_ref, kk_ref, ka_ref, rk_ref, ones_ref,
                   q_ref, k_ref, v_ref, sf_ref, g_ref, bonus_ref):
    i = pl.program_id(1)
    has_prev, has_next = _halo_flags(i, pl.num_programs(1))
    m = mod_ref[0, 0]
    xt = jnp.where(i == 0, ctx_ref[0], x_ref[0])
    xe = jnp.concatenate([xt, xp_ref[0], xn_ref[0]], axis=0)
    h = xe * (1.0 + m[1:2]) + m[0:1]
    fe = _bdot(h, win_ref[...])
    f = fe[:TM]
    o_kv = MLA_Q_RANK
    o_kr = o_kv + MLA_KV_RANK
    o_rw = o_kr + 2 * HEAD_PAD
    prow = jnp.where(has_prev, fe[TM + SUB - 1:TM + SUB, o_rw:], 0.0)
    nrow = jnp.where(has_next, fe[TM + SUB:TM + SUB + 1, o_rw:], 0.0)
    _rw_features(f[:, o_rw:], prow, nrow, mu_ref, w0_ref, w2_ref, a0_ref, a2_ref, g2_ref, kk_ref, ka_ref, rk_ref,
                 ones_ref, sf_ref, g_ref, bonus_ref)
    fq = f[:, :o_kv]
    fkv = f[:, o_kv:o_kr]
    fkr = f[:, o_kr:o_kr + HEAD_PAD]
    fkr_rot = f[:, o_kr + HEAD_PAD:o_rw]
    qn = fq * lax.rsqrt(jnp.mean(fq * fq, axis=-1, keepdims=True) + RMS_EPS) * qn_ref[...]
    ckv = fkv * lax.rsqrt(jnp.mean(fkv * fkv, axis=-1, keepdims=True) + RMS_EPS) * kvn_ref[...]
    qn = qn.astype(BF16)
    ckv = ckv.astype(BF16)
    qa = jnp.dot(qn, wqa_ref[...], preferred_element_type=F32)
    qb = jnp.dot(qn, wqb_ref[...], preferred_element_type=F32)
    kn = jnp.dot(ckv, wuk_ref[...], preferred_element_type=F32)
    cs = cs_ref[...]
    sn = sn_ref[...]
    kr = fkr * cs + fkr_rot * sn
    for hd in range(HEADS):
        sl = slice(hd * HEAD_PAD, (hd + 1) * HEAD_PAD)
        q_ref[0, :, sl] = ((qa[:, sl] * cs + qb[:, sl] * sn) * (ATT_SCALE * LOG2_E)).astype(BF16)
        k_ref[0, :, sl] = (kn[:, sl] + kr).astype(BF16)
    v_ref[0] = jnp.dot(ckv, wuv_ref[...], preferred_element_type=F32).astype(BF16)


def _seg_specs(n):
    ctx_spec = pl.BlockSpec((1, TM, D), lambda bb, i: (bb, 0, 0))
    lat_spec = pl.BlockSpec((1, TM, D), lambda bb, i: (bb, jnp.maximum(i - 1, 0), 0))
    return ctx_spec, lat_spec


def _front0(ctx, x, modt, mla_consts, cs, sn, rw_consts):
    b, n, _ = x.shape
    t = n + TM
    row = lambda w: pl.BlockSpec((1, TM, w), lambda bb, i: (bb, i, 0))
    ctx_spec, lat_spec = _seg_specs(n)
    hb = TM // SUB
    prev = pl.BlockSpec((1, SUB, D), lambda bb, i: (bb, jnp.maximum((i - 1) * hb - 1, 0), 0))
    nxt = pl.BlockSpec((1, SUB, D), lambda bb, i: (bb, jnp.minimum(i * hb, n // SUB - 1), 0))
    rope = pl.BlockSpec((TM, HEAD_PAD), lambda bb, i: (i, 0))
    return pl.pallas_call(
        _front0_kernel,
        grid=(b, t // TM),
        in_specs=[ctx_spec, lat_spec, prev, nxt, _mod_spec(True)] + [_weight_spec(c.shape) for c in mla_consts]
                 + [rope, rope] + [_weight_spec(c.shape) for c in rw_consts],
        out_specs=[row(HEADS * HEAD_PAD), row(HEADS * HEAD_PAD), row(HEADS * MLA_V),
                   pl.BlockSpec((1, 9, TM // RELAY_T, HEADS * PITCH_F, RELAY_T), lambda bb, i: (bb, 0, i, 0, 0)),
                   row(RW_W), row(RW_W)],
        out_shape=[jax.ShapeDtypeStruct((b, t, HEADS * HEAD_PAD), BF16),
                   jax.ShapeDtypeStruct((b, t, HEADS * HEAD_PAD), BF16),
                   jax.ShapeDtypeStruct((b, t, HEADS * MLA_V), BF16),
                   jax.ShapeDtypeStruct((b, 9, t // RELAY_T, HEADS * PITCH_F, RELAY_T), F32),
                   jax.ShapeDtypeStruct((b, t, RW_W), F32),
                   jax.ShapeDtypeStruct((b, t, RW_W), F32)],
        compiler_params=_params(("parallel", "parallel")),
        name="l0_front",
    )(ctx, x, x, x, modt, *mla_consts, cs, sn, *rw_consts)


def _att_kernel(q_ref, k_ref, v_ref, o_ref, *, n_ctx, n_all):
    i = pl.program_id(2)

    def attend(nk):
        lane = lax.broadcasted_iota(jnp.int32, (nk, 2 * MLA_V), 1)
        for pair in range(ATT_HEADS // 2):
            v = v_ref[0, :nk, pair * 2 * MLA_V:(pair + 1) * 2 * MLA_V]
            acc = None
            for hh in range(2):
                hd = 2 * pair + hh
                sl = slice(hd * HEAD_PAD, (hd + 1) * HEAD_PAD)
                q = q_ref[0, :, sl]
                k = k_ref[0, :nk, sl]
                s = lax.dot_general(q, k, (((1,), (1,)), ((), ())), preferred_element_type=F32)
                p = jnp.exp2(s - jnp.max(s, axis=-1, keepdims=True))
                l = jnp.sum(p, axis=-1, keepdims=True)
                keep = (lane < MLA_V) if hh == 0 else (lane >= MLA_V)
                vh = jnp.where(keep, v, jnp.zeros_like(v))
                o = jnp.dot(p.astype(BF16), vh, preferred_element_type=F32) / l
                acc = o if acc is None else acc + o
            o_ref[0, :, pair * 2 * MLA_V:(pair + 1) * 2 * MLA_V] = acc.astype(o_ref.dtype)

    @pl.when(i == 0)
    def _():
        attend(n_ctx)

    @pl.when(i > 0)
    def _():
        attend(n_all)


def _attention(q, k, v):
    b, t, _ = q.shape
    return pl.pallas_call(
        functools.partial(_att_kernel, n_ctx=TM, n_all=t),
        grid=(b, HEADS // ATT_HEADS, t // TM),
        in_specs=[pl.BlockSpec((1, TM, ATT_HEADS * HEAD_PAD), lambda bb, hp, i: (bb, i, hp)),
                  pl.BlockSpec((1, t, ATT_HEADS * HEAD_PAD), lambda bb, hp, i: (bb, 0, hp)),
                  pl.BlockSpec((1, t, ATT_HEADS * MLA_V), lambda bb, hp, i: (bb, 0, hp))],
        out_specs=pl.BlockSpec((1, TM, ATT_HEADS * MLA_V), lambda bb, hp, i: (bb, i, hp)),
        out_shape=jax.ShapeDtypeStruct((b, t, HEADS * MLA_V), BF16),
        compiler_params=_params(("parallel", "parallel", "parallel")),
        name="l0_attention",
    )(q, k, v)


def _halo_flags(i, n_tiles):
    return i >= 2, jnp.logical_and(i >= 1, i < n_tiles - 1)


def _rw_features(f, prow, nrow, mu_ref, w0_ref, w2_ref, a0_ref, a2_ref, g2_ref,
                 kk_ref, ka_ref, rk_ref, ones_ref, sf_ref, g_ref, bonus_ref):
    rid = lax.broadcasted_iota(jnp.int32, (TM, 1), 0)
    prev = jnp.where(rid == 0, prow, pltpu.roll(f, 1, axis=0))
    nxt = jnp.where(rid == TM - 1, nrow, pltpu.roll(f, TM - 1, axis=0))
    f = f + mu_ref[...] * (0.5 * (prev + nxt) - f)
    r = f[:, 0:RW_W]
    k = f[:, RW_W:2 * RW_W]
    v = f[:, 2 * RW_W:3 * RW_W]
    o = 3 * RW_W
    wl = f[:, o:o + 2 * LORA_W]
    al = f[:, o + 2 * LORA_W:o + 2 * LORA_W + 2 * LORA_A]
    gl = f[:, o + 2 * LORA_W + 2 * LORA_A:]
    w_raw = w0_ref[...] + _bdot(jnp.tanh(wl), w2_ref[...])
    decay = jnp.exp(-EXP_NEG_HALF * _sigmoid(w_raw))
    a = _sigmoid(a0_ref[...] + _bdot(al, a2_ref[...]))
    g_ref[0] = _bdot(_sigmoid(gl), g2_ref[...])
    ones_bd = ones_ref[...]
    kk = k * kk_ref[...]
    kk = kk * lax.rsqrt(_head_sum(kk * kk, ones_bd) + 1e-12)
    ka = ka_ref[...]
    kd = [k * (1.0 + (a[:, d * RW_W:(d + 1) * RW_W] - 1.0) * ka) for d in range(2)]
    bonus_ref[0] = _head_sum(r * (kd[0] + kd[1]) * rk_ref[...], ones_bd) * v
    parts = [r, kk, v, decay[:, :RW_W], kd[0], kk * a[:, :RW_W],
             decay[:, RW_W:], kd[1], kk * a[:, RW_W:]]
    pad = jnp.zeros((PITCH_F - RW_HEAD, RELAY_T), F32)
    for n, p in enumerate(parts):
        for half in range(TM // RELAY_T):
            for hp in range(HEADS // 2):
                pt = p[half * RELAY_T:(half + 1) * RELAY_T, hp * 128:(hp + 1) * 128].T
                for h2 in range(2):
                    row0 = (2 * hp + h2) * PITCH_F
                    sf_ref[0, n, half, row0:row0 + RW_HEAD, :] = pt[h2 * RW_HEAD:(h2 + 1) * RW_HEAD]
            for hd in range(HEADS):
                sf_ref[0, n, half, hd * PITCH_F + RW_HEAD:(hd + 1) * PITCH_F, :] = pad


def _time_mirror(i, n_ctx_tiles, n_tiles):
    return jnp.where(i < n_ctx_tiles, n_ctx_tiles - 1 - i, n_tiles + n_ctx_tiles - 1 - i)


def _reverse_lanes(x, j3):
    hi = x.astype(BF16)
    r1 = x - hi.astype(F32)
    mid = r1.astype(BF16)
    lo = (r1 - mid.astype(F32)).astype(BF16)
    return jnp.dot(jnp.concatenate([hi, mid, lo], axis=1), j3, preferred_element_type=F32)


def _relayout_in_kernel(xa_ref, xb_ref, j3_ref, o_ref, *, step_major):
    nb = xa_ref.shape[0]
    tr = xa_ref.shape[4]
    half = nb * HEADS

    def gather(x_ref, f):
        return jnp.concatenate([x_ref[b, 0, 0, pl.ds(f, HEADS, stride=PITCH_F), :] for b in range(nb)], axis=0)

    for f0 in range(0, RW_HEAD, RELAY_FB):
        fs = range(f0, f0 + RELAY_FB)
        rev = _reverse_lanes(jnp.concatenate([gather(xb_ref, f) for f in fs], axis=0), j3_ref[...])
        for n, f in enumerate(fs):
            mt = jnp.concatenate([gather(xa_ref, f), rev[n * half:(n + 1) * half]], axis=0).T
            if step_major:
                o_ref[pl.ds(f, tr, stride=PITCH_F), :] = mt
            else:
                o_ref[0, f] = mt
    if step_major:
        for f in range(RW_HEAD, PITCH_F):
            o_ref[pl.ds(f, tr, stride=PITCH_F), :] = jnp.zeros((tr, 2 * half), F32)


def _relayout_in(sf, j3):
    b, _, nt, rows, _ = sf.shape
    t = nt * RELAY_T
    chains = 2 * b * HEADS
    nc = TM // RELAY_T
    mirror = lambda i: _time_mirror(i, nc, nt)
    blk = (b, 1, 1, rows, RELAY_T)
    dirdep = lambda p: jnp.minimum(p // 2, 1)
    x = pl.pallas_call(
        functools.partial(_relayout_in_kernel, step_major=False),
        grid=(nt, 5),
        in_specs=[pl.BlockSpec(blk, lambda i, p: (0, p + dirdep(p), i, 0, 0)),
                  pl.BlockSpec(blk, lambda i, p: (0, p + 4 * dirdep(p), mirror(i), 0, 0)),
                  _const_spec(j3.shape)],
        out_specs=pl.BlockSpec((1, RW_HEAD, RELAY_T, chains), lambda i, p: (p, 0, i, 0)),
        out_shape=jax.ShapeDtypeStruct((5, RW_HEAD, t, chains), F32),
        compiler_params=_params(("parallel", "parallel")),
        name="l0_rwkv_relayout_in",
    )(sf, sf, j3)
    v = pl.pallas_call(
        functools.partial(_relayout_in_kernel, step_major=True),
        grid=(nt,),
        in_specs=[pl.BlockSpec(blk, lambda i: (0, 2, i, 0, 0)),
                  pl.BlockSpec(blk, lambda i: (0, 2, mirror(i), 0, 0)),
                  _const_spec(j3.shape)],
        out_specs=pl.BlockSpec((RELAY_T * PITCH_F, chains), lambda i: (i, 0)),
        out_shape=jax.ShapeDtypeStruct((t * PITCH_F, chains), F32),
        compiler_params=_params(("parallel",)),
        name="l0_rwkv_relayout_v",
    )(sf, sf, j3)
    return x, v


def _rwscan_kernel(x_ref, v_ref, y_ref, s_ref):
    tc = x_ref.shape[2]
    chains = x_ref.shape[3]

    @pl.when(pl.program_id(0) == 0)
    def _():
        s_ref[...] = jnp.zeros_like(s_ref)

    groups = RW_HEAD // SUB
    zeros = tuple(jnp.zeros((SUB, chains), F32) for _ in range(groups))

    def row(a, j, s):
        return jnp.broadcast_to(x_ref[a, j, pl.ds(s, 1), :], (SUB, chains))

    def s_at(j, g):
        return s_ref.at[j, g * SUB:(g + 1) * SUB, :]

    sa0 = list(zeros)
    for j in range(RW_HEAD):
        kk = row(1, j, 0)
        for g in range(groups):
            sa0[g] = sa0[g] + s_at(j, g)[...] * kk

    def step(s, sa):
        base = pl.multiple_of(s * PITCH_F, SUB)
        v = v_ref[pl.ds(base, RW_HEAD), :]
        s_next = jnp.minimum(s + 1, tc - 1)
        y = list(zeros)
        sa_next = list(zeros)
        for j in range(RW_HEAD):
            r, kk_next, w, k, kka = row(0, j, s), row(1, j, s_next), row(2, j, s), row(3, j, s), row(4, j, s)
            for g in range(groups):
                sj = s_at(j, g)[...] * w - sa[g] * kka + v[g * SUB:(g + 1) * SUB] * k
                s_at(j, g)[...] = sj
                y[g] = y[g] + sj * r
                sa_next[g] = sa_next[g] + sj * kk_next
        y_ref[pl.ds(base, RW_HEAD), :] = jnp.concatenate(y, axis=0)
        y_ref[pl.ds(base + RW_HEAD, PITCH_F - RW_HEAD), :] = zeros[0]
        return tuple(sa_next)

    lax.fori_loop(0, tc, step, tuple(sa0))


def _rwscan(x, v):
    _, _, t, chains = x.shape
    tc = SCAN_TC if t % SCAN_TC == 0 else SCAN_TC_SMALL
    return pl.pallas_call(
        _rwscan_kernel,
        grid=(t // tc,),
        in_specs=[pl.BlockSpec((5, RW_HEAD, tc, chains), lambda i: (0, 0, i, 0)),
                  pl.BlockSpec((tc * PITCH_F, chains), lambda i: (i, 0))],
        out_specs=pl.BlockSpec((tc * PITCH_F, chains), lambda i: (i, 0)),
        out_shape=jax.ShapeDtypeStruct((t * PITCH_F, chains), F32),
        scratch_shapes=[pltpu.VMEM((RW_HEAD, RW_HEAD, chains), F32)],
        compiler_params=_params(("arbitrary",)),
        name="l0_rwkv_scan",
    )(x, v)


def _relayout_out_kernel(yf_ref, yr_ref, j3_ref, o_ref, q_ref):
    nb = o_ref.shape[0]
    tr = o_ref.shape[1]
    chains = yf_ref.shape[1]
    half = chains // 2
    pitch_c = chains + SUB
    fwd = lax.broadcasted_iota(jnp.int32, (tr, chains), 1) < half
    for i in range(RW_HEAD):
        rows = pl.ds(i, tr, stride=PITCH_F)
        mt = jnp.where(fwd, yf_ref[rows, :], yr_ref[rows, :]).T
        q_ref[i * pitch_c:i * pitch_c + half, :] = mt[:half]
        q_ref[i * pitch_c + half:i * pitch_c + chains, :] = _reverse_lanes(mt[half:], j3_ref[...])
    for b in range(nb):
        for hp in range(HEADS // 2):
            parts = []
            for h2 in range(2):
                c = b * HEADS + 2 * hp + h2
                parts.append(q_ref[pl.ds(c, RW_HEAD, stride=pitch_c), :]
                             + q_ref[pl.ds(half + c, RW_HEAD, stride=pitch_c), :])
            o_ref[b, :, hp * 128:(hp + 1) * 128] = jnp.concatenate(parts, axis=0).T


def _relayout_out(y, j3, b):
    chains = y.shape[1]
    t = y.shape[0] // PITCH_F
    nt = t // RELAY_T
    nc = TM // RELAY_T
    return pl.pallas_call(
        _relayout_out_kernel,
        grid=(nt,),
        in_specs=[pl.BlockSpec((RELAY_T * PITCH_F, chains), lambda i: (i, 0)),
                  pl.BlockSpec((RELAY_T * PITCH_F, chains), lambda i: (_time_mirror(i, nc, nt), 0)),
                  _const_spec(j3.shape)],
        out_specs=pl.BlockSpec((b, RELAY_T, RW_W), lambda i: (0, i, 0)),
        out_shape=jax.ShapeDtypeStruct((b, t, RW_W), F32),
        scratch_shapes=[pltpu.VMEM((RW_HEAD * (chains + SUB), RELAY_T), F32)],
        compiler_params=_params(("parallel",)),
        name="l0_rwkv_relayout_out",
    )(y, y, j3)


def _rwkv_scan_both(sf):
    k = jnp.arange(RELAY_T)
    anti = (k[:, None] + k[None, :] == RELAY_T - 1).astype(BF16)
    j3 = jnp.concatenate([anti, anti, anti], axis=0)
    y = _rwscan(*_relayout_in(sf, j3))
    return _relayout_out(y, j3, sf.shape[0])


def _mlp_hidden(h, w1_ref, w2_ref):
    acc = None
    for c in range(D_FF // MLP_CHUNK):
        u = jnp.dot(h, w1_ref[:, c * MLP_CHUNK:(c + 1) * MLP_CHUNK], preferred_element_type=F32)
        u = jnp.square(jnp.maximum(u, 0.0)).astype(BF16)
        part = jnp.dot(u, w2_ref[c * MLP_CHUNK:(c + 1) * MLP_CHUNK, :], preferred_element_type=F32)
        acc = part if acc is None else acc + part
    return acc


def _tail0_kernel(att_ref, ys_ref, bonus_ref, g_ref, ctx_ref, x_ref, mod_ref, mod1_ref, gnw_ref, gnb_ref, ones_ref,
                  woa_ref, wob_ref, l1g_ref, l1b_ref, w1_ref, w2_ref, l2g_ref, l2b_ref, win1_ref,
                  x2_ref, gate_ref, xr_ref):
    m = mod_ref[0, 0]
    ones_bd = ones_ref[...]
    y = ys_ref[0]
    mu = _head_sum(y, ones_bd) * (1.0 / RW_HEAD)
    yc = y - mu
    var = _head_sum(yc * yc, ones_bd) * (1.0 / RW_HEAD)
    yn = yc * lax.rsqrt(var + GN_EPS) * gnw_ref[...] + gnb_ref[...]
    rw = (yn + bonus_ref[0]) * g_ref[0]
    o = (jnp.dot(att_ref[0], woa_ref[...], preferred_element_type=F32)
         + _bdot(rw, wob_ref[...]))
    x0 = jnp.where(pl.program_id(1) == 0, ctx_ref[0], x_ref[0])
    x1 = _layer_norm(ALPHA * x0 + m[2:3] * o, l1g_ref[...], l1b_ref[...])
    acc = _mlp_hidden((x1 * (1.0 + m[4:5]) + m[3:4]).astype(BF16), w1_ref, w2_ref)
    x2 = _layer_norm(ALPHA * x1 + m[5:6] * acc, l2g_ref[...], l2b_ref[...])
    x2_ref[0] = x2
    m1 = mod1_ref[0, 0]
    f = _bdot(x2 * (1.0 + m1[1:2]) + m1[0:1], win1_ref[...])
    gate_ref[0] = f[:, :LRU_W]
    xr_ref[0] = f[:, LRU_W:]


def _tail0(att, ys, bonus, g, ctx, x, mod0, mod1, consts):
    b, n, _ = x.shape
    t = n + TM
    row = lambda w: pl.BlockSpec((1, TM, w), lambda bb, i: (bb, i, 0))
    ctx_spec, lat_spec = _seg_specs(n)
    return pl.pallas_call(
        _tail0_kernel,
        grid=(b, t // TM),
        in_specs=[row(RW_W), row(RW_W), row(RW_W), row(RW_W), ctx_spec, lat_spec, _mod_spec(True), _mod_spec(True)]
                 + [_weight_spec(c.shape) for c in consts],
        out_specs=[row(D), row(LRU_W), row(LRU_W)],
        out_shape=[jax.ShapeDtypeStruct((b, t, D), F32), jax.ShapeDtypeStruct((b, t, LRU_W), F32),
                   jax.ShapeDtypeStruct((b, t, LRU_W), F32)],
        compiler_params=_params(("parallel", "parallel")),
        name="l0_tail",
    )(att, ys, bonus, g, ctx, x, mod0, mod1, *consts)


def _lru_fwd_kernel(x_ref, xp_ref, xn_ref, cw_ref, cb_ref, wbd_ref, gb_ref, lam_ref,
                    hf_ref, a1_ref, u1_ref, a_scr, u_scr, c_ref):
    i = pl.program_id(0)
    n = pl.num_programs(0)
    nb, tl, _ = x_ref.shape
    rows = nb * tl
    nc = TM // tl
    pitch_b = tl + SUB

    @pl.when(i == 0)
    def _():
        c_ref[...] = jnp.zeros_like(c_ref)

    lam = lam_ref[...]
    nl = -lam
    softplus = jnp.maximum(nl, 0.0) + jnp.log1p(jnp.exp(-jnp.abs(nl)))
    gb = gb_ref[...]
    cw = cw_ref[...]
    tpos = jnp.bitwise_and(lax.broadcasted_iota(jnp.int32, (rows, 1), 0), tl - 1)
    seg_first = jnp.logical_or(i == 0, i == nc)
    seg_last = jnp.logical_or(i == nc - 1, i == n - 1)

    def halo(ref, r, edge):
        per_b = [jnp.broadcast_to(ref[b, r:r + 1, :], (tl, LRU_W)) for b in range(nb)]
        return jnp.where(edge, 0.0, jnp.concatenate(per_b, axis=0))

    x = x_ref[...].reshape(rows, LRU_W)
    p1 = halo(xp_ref, SUB - 1, seg_first)
    p2 = halo(xp_ref, SUB - 2, seg_first)
    n1 = halo(xn_ref, 0, seg_last)
    xm1 = jnp.where(tpos == 0, p1, pltpu.roll(x, 1, axis=0))
    xm2 = jnp.where(tpos == 0, p2, jnp.where(tpos == 1, p1, pltpu.roll(x, 2, axis=0)))
    xp1 = jnp.where(tpos == tl - 1, n1, pltpu.roll(x, rows - 1, axis=0))
    xc = cw[0:1] * xm2 + cw[1:2] * xm1 + cw[2:3] * x + cw[3:4] * xp1 + cb_ref[...]
    for blk in range(LRU_BLOCKS):
        sl = slice(blk * LRU_BLOCK, (blk + 1) * LRU_BLOCK)
        xb = xc[:, sl]
        z = _bdot(xb, wbd_ref[blk])
        for d in range(2):
            zr = z[:, 2 * d * LRU_BLOCK:(2 * d + 1) * LRU_BLOCK] + gb[2 * d:2 * d + 1, sl]
            zi = z[:, (2 * d + 1) * LRU_BLOCK:(2 * d + 2) * LRU_BLOCK] + gb[2 * d + 1:2 * d + 2, sl]
            log_a = -LRU_C * _sigmoid(zr) * softplus[d:d + 1, sl]
            a = jnp.exp(log_a)
            u = jnp.sqrt(-jnp.tanh(log_a) * (a * a + 1.0)) * (_sigmoid(zi) * xb)
            for b in range(nb):
                a_scr[d, blk, b * pitch_b:b * pitch_b + tl, :] = a[b * tl:(b + 1) * tl]
                u_scr[d, blk, b * pitch_b:b * pitch_b + tl, :] = u[b * tl:(b + 1) * tl]
    h = [c_ref[:, blk * LRU_BLOCK:(blk + 1) * LRU_BLOCK] for blk in range(LRU_BLOCKS)]
    pad = jnp.zeros((SUB, LRU_BLOCK), F32)
    for t in range(tl):
        for blk in range(LRU_BLOCKS):
            step_rows = pl.ds(t, nb, stride=pitch_b)
            dst = slice(t * PITCH_H + blk * SUB, t * PITCH_H + blk * SUB + nb)
            h[blk] = a_scr[0, blk, step_rows, :] * h[blk] + u_scr[0, blk, step_rows, :]
            hf_ref[dst, :] = h[blk]
            a1_ref[dst, :] = a_scr[1, blk, step_rows, :]
            u1_ref[dst, :] = u_scr[1, blk, step_rows, :]
        for ref in (hf_ref, a1_ref, u1_ref):
            ref[t * PITCH_H + LRU_BLOCKS * SUB:(t + 1) * PITCH_H, :] = pad
    for blk in range(LRU_BLOCKS):
        c_ref[:, blk * LRU_BLOCK:(blk + 1) * LRU_BLOCK] = h[blk]


def _lru_bwd_kernel(a_ref, u_ref, h_ref, c_ref, *, nb):
    tb = a_ref.shape[0] // PITCH_H

    @pl.when(pl.program_id(0) == 0)
    def _():
        c_ref[...] = jnp.zeros_like(c_ref)

    def step(s, h):
        base = pl.multiple_of((tb - 1 - s) * PITCH_H, SUB)
        out = []
        for blk in range(LRU_BLOCKS):
            rows = pl.ds(base + blk * SUB, nb)
            hb = a_ref[rows, :] * h[blk] + u_ref[rows, :]
            h_ref[rows, :] = hb
            out.append(hb)
        h_ref[pl.ds(base + LRU_BLOCKS * SUB, SUB), :] = jnp.zeros((SUB, LRU_BLOCK), F32)
        return tuple(out)

    h0 = tuple(c_ref[blk] for blk in range(LRU_BLOCKS))
    h = lax.fori_loop(0, tb, step, h0)
    for blk in range(LRU_BLOCKS):
        c_ref[blk] = h[blk]


def _lru(xr, cw, cb, wbd, gb, lam):
    b, t, _ = xr.shape
    assert b <= SUB
    n = t // LRU_TL
    hb = LRU_TL // SUB
    last = t // SUB - 1
    consts = (cw, cb, wbd, gb, lam)
    hspec = pl.BlockSpec((LRU_TL * PITCH_H, LRU_BLOCK), lambda i: (i, 0))
    coef = jax.ShapeDtypeStruct((t * PITCH_H, LRU_BLOCK), F32)
    scr = pltpu.VMEM((2, LRU_BLOCKS, b * (LRU_TL + SUB), LRU_BLOCK), F32)
    hf, a1, u1 = pl.pallas_call(
        _lru_fwd_kernel,
        grid=(n,),
        in_specs=[pl.BlockSpec((b, LRU_TL, LRU_W), lambda i: (0, i, 0)),
                  pl.BlockSpec((b, SUB, LRU_W), lambda i: (0, jnp.maximum(i * hb - 1, 0), 0)),
                  pl.BlockSpec((b, SUB, LRU_W), lambda i: (0, jnp.minimum((i + 1) * hb, last), 0))]
                 + [_const_spec(c.shape) for c in consts],
        out_specs=[hspec, hspec, hspec],
        out_shape=[coef, coef, coef],
        scratch_shapes=[scr, scr, pltpu.VMEM((b, LRU_W), F32)],
        compiler_params=_params(("arbitrary",)),
        name="l1_lru_fwd",
    )(xr, xr, xr, *consts)
    nt = t // LRU_TB
    nc = TM // LRU_TB
    bspec = pl.BlockSpec((LRU_TB * PITCH_H, LRU_BLOCK), lambda i: (_time_mirror(i, nc, nt), 0))
    hr = pl.pallas_call(
        functools.partial(_lru_bwd_kernel, nb=b),
        grid=(nt,),
        in_specs=[bspec, bspec],
        out_specs=bspec,
        out_shape=coef,
        scratch_shapes=[pltpu.VMEM((LRU_BLOCKS, b, LRU_BLOCK), F32)],
        compiler_params=_params(("arbitrary",)),
        name="l1_lru_bwd",
    )(a1, u1)
    return hf, hr


def _tail1_kernel(gate_ref, hf_ref, hr_ref, x_ref, mod_ref, w_ref, l1g_ref, l1b_ref, w1_ref, w2_ref, l2g_ref, l2b_ref,
                  o_ref):
    nb, tl, _ = gate_ref.shape
    per_b = []
    for b in range(nb):
        cols = []
        for blk in range(LRU_BLOCKS):
            rows = pl.ds(blk * SUB + b, tl, stride=PITCH_H)
            cols.append(hf_ref[rows, :] + hr_ref[rows, :])
        per_b.append(jnp.concatenate(cols, axis=1))
    h = jnp.concatenate(per_b, axis=0)
    gate = gate_ref[...].reshape(nb * tl, LRU_W)
    gelu = 0.5 * gate * (1.0 + jnp.tanh(math.sqrt(2.0 / math.pi) * (gate + 0.044715 * gate * gate * gate)))
    o = _bdot(gelu * h, w_ref[...])
    mods = [mod_ref[b, 0] for b in range(nb)]
    x1 = [_layer_norm(ALPHA * x_ref[b] + mods[b][2:3] * o[b * tl:(b + 1) * tl], l1g_ref[...], l1b_ref[...])
          for b in range(nb)]
    hm = jnp.concatenate([x1[b] * (1.0 + mods[b][4:5]) + mods[b][3:4] for b in range(nb)], axis=0)
    acc = _mlp_hidden(hm.astype(BF16), w1_ref, w2_ref)
    for b in range(nb):
        z = ALPHA * x1[b] + mods[b][5:6] * acc[b * tl:(b + 1) * tl]
        o_ref[b] = _layer_norm(z, l2g_ref[...], l2b_ref[...])


def _tail1(gate, hf, hr, xc, modt, consts):
    b, t, _ = xc.shape
    off = TM // LRU_TL
    nt = t // LRU_TL - off
    lat = lambda wd: pl.BlockSpec((b, LRU_TL, wd), lambda i: (0, i + off, 0))
    hspec = pl.BlockSpec((LRU_TL * PITCH_H, LRU_BLOCK), lambda i: (i + off, 0))
    return pl.pallas_call(
        _tail1_kernel,
        grid=(nt,),
        in_specs=[lat(LRU_W), hspec, hspec, lat(D),
                  pl.BlockSpec((b, 1, SUB, D), lambda i: (0, 1, 0, 0))]
                 + [_weight_spec(c.shape) for c in consts],
        out_specs=pl.BlockSpec((b, LRU_TL, D), lambda i: (0, i, 0)),
        out_shape=jax.ShapeDtypeStruct((b, nt * LRU_TL, D), F32),
        compiler_params=_params(("parallel",)),
        name="l1_tail",
    )(gate, hf, hr, xc, modt, *consts)


def _rot_cols(w):
    ws = w.reshape(w.shape[:-1] + (2, 2, ROPE_AXIS // 2))
    return jnp.stack([-ws[..., 1, :], ws[..., 0, :]], axis=-2).reshape(w.shape)


def _rope_tables(n, n_ctx):
    rows_n = n // GRID_W
    rows = jnp.repeat(jnp.arange(rows_n, dtype=F32), GRID_W)
    cols = jnp.tile(jnp.arange(GRID_W, dtype=F32), rows_n)
    inv_freq = ROPE_THETA ** (-jnp.arange(0, ROPE_AXIS, 2, dtype=F32) / ROPE_AXIS)
    ang_r = rows[:, None] * inv_freq
    ang_c = cols[:, None] * inv_freq
    ang = jnp.concatenate([ang_r, ang_r, ang_c, ang_c], axis=-1)
    cos = jnp.concatenate([jnp.ones((n_ctx, MLA_ROPE), F32), jnp.cos(ang)], axis=0)
    sin = jnp.concatenate([jnp.zeros((n_ctx, MLA_ROPE), F32), jnp.sin(ang)], axis=0)
    t = n + n_ctx
    cs = jnp.concatenate([jnp.ones((t, MLA_NOPE), F32), cos, jnp.zeros((t, 32), F32)], axis=-1)
    sn = jnp.concatenate([jnp.zeros((t, MLA_NOPE), F32), sin, jnp.zeros((t, 32), F32)], axis=-1)
    return cs, sn


def _block_diag2(w):
    z = jnp.zeros_like(w[0])
    return jnp.concatenate([jnp.concatenate([w[0], z], axis=1), jnp.concatenate([z, w[1]], axis=1)], axis=0)


def kernel(x, c, ctx, c_ctx, l0_mod_w, l0_mod_b, l0_w_in, l0_mla_q_norm, l0_mla_w_uq, l0_mla_kv_norm, l0_mla_w_uk, l0_mla_w_uv, l0_rwkv_mu, l0_rwkv_w0, l0_rwkv_w2, l0_rwkv_a0, l0_rwkv_a2, l0_rwkv_g2, l0_rwkv_k_k, l0_rwkv_k_a, l0_rwkv_r_k, l0_rwkv_gn_w, l0_rwkv_gn_b, l0_w_out, l0_ln1_g, l0_ln1_b, l0_mlp_w1, l0_mlp_w2, l0_ln2_g, l0_ln2_b, l1_mod_w, l1_mod_b, l1_w_in, l1_conv_w, l1_conv_b, l1_lru_ga_w, l1_lru_ga_b, l1_lru_gx_w, l1_lru_gx_b, l1_lru_lambda, l1_w_out, l1_ln1_g, l1_ln1_b, l1_mlp_w1, l1_mlp_w2, l1_ln2_g, l1_ln2_b):
    b, n, _ = x.shape
    n_ctx = ctx.shape[1]
    assert n_ctx == TM and n % TM == 0 and x.shape[2] == D
    row = lambda v: v.reshape(1, -1)

    mod0 = _mod_table(c, c_ctx, l0_mod_w, l0_mod_b)
    mod1 = _mod_table(c, c_ctx, l1_mod_w, l1_mod_b)

    o_kv = MLA_Q_RANK
    o_kr = o_kv + MLA_KV_RANK
    o_rw = o_kr + MLA_ROPE
    w_kr = l0_w_in[:, o_kr:o_rw]
    zl = jnp.zeros((D, MLA_NOPE), F32)
    zr = jnp.zeros((D, HEAD_PAD - MLA_NOPE - MLA_ROPE), F32)
    win0 = jnp.concatenate([l0_w_in[:, :o_kr], zl, w_kr, zr, zl, _rot_cols(w_kr), zr,
                            l0_w_in[:, o_rw:]], axis=1).astype(BF16)
    wq = l0_mla_w_uq.reshape(MLA_Q_RANK, HEADS, MLA_NOPE + MLA_ROPE)
    q_nope, q_rope = wq[..., :MLA_NOPE], wq[..., MLA_NOPE:]
    zq = jnp.zeros((MLA_Q_RANK, HEADS, 32), F32)
    wqa = jnp.concatenate([q_nope, q_rope, zq], axis=-1).reshape(MLA_Q_RANK, HEADS * HEAD_PAD).astype(BF16)
    wqb = jnp.concatenate([jnp.zeros_like(q_nope), _rot_cols(q_rope), zq],
                          axis=-1).reshape(MLA_Q_RANK, HEADS * HEAD_PAD).astype(BF16)
    wk = l0_mla_w_uk.reshape(MLA_KV_RANK, HEADS, MLA_NOPE)
    wuk = jnp.concatenate([wk, jnp.zeros_like(wk)], axis=-1).reshape(MLA_KV_RANK, HEADS * HEAD_PAD).astype(BF16)
    wuv = l0_mla_w_uv.astype(BF16)
    cs, sn = _rope_tables(n, n_ctx)
    hid = jnp.arange(RW_W) // RW_HEAD
    ones_bd = (hid[:, None] == hid[None, :]).astype(BF16)

    mla_consts = (win0, row(l0_mla_q_norm), row(l0_mla_kv_norm), wqa, wqb, wuk, wuv)
    rw_consts = (row(l0_rwkv_mu), row(l0_rwkv_w0), _block_diag2(l0_rwkv_w2).astype(BF16), row(l0_rwkv_a0),
                 _block_diag2(l0_rwkv_a2).astype(BF16), l0_rwkv_g2.astype(BF16), row(l0_rwkv_k_k),
                 row(l0_rwkv_k_a), row(l0_rwkv_r_k), ones_bd)
    q, k, v, sf, g, bonus = _front0(ctx, x, mod0, mla_consts, cs, sn, rw_consts)
    att = _attention(q, k, v)
    ys = _rwkv_scan_both(sf)
    wo = l0_w_out.astype(BF16)
    tail0_consts = (row(l0_rwkv_gn_w), row(l0_rwkv_gn_b), ones_bd, wo[:HEADS * MLA_V], wo[HEADS * MLA_V:],
                    row(l0_ln1_g), row(l0_ln1_b), l0_mlp_w1.astype(BF16), l0_mlp_w2.astype(BF16),
                    row(l0_ln2_g), row(l0_ln2_b), l1_w_in.astype(BF16))
    xc, gate, xr = _tail0(att, ys, bonus, g, ctx, x, mod0, mod1, tail0_consts)

    wbd = jnp.concatenate([l1_lru_ga_w[0], l1_lru_gx_w[0], l1_lru_ga_w[1], l1_lru_gx_w[1]], axis=-1).astype(BF16)
    gb = jnp.stack([l1_lru_ga_b[0], l1_lru_gx_b[0], l1_lru_ga_b[1], l1_lru_gx_b[1]])
    hf, hr = _lru(xr, l1_conv_w, row(l1_conv_b), wbd, gb, l1_lru_lambda)
    tail1_consts = (l1_w_out.astype(BF16), row(l1_ln1_g), row(l1_ln1_b), l1_mlp_w1.astype(BF16),
                    l1_mlp_w2.astype(BF16), row(l1_ln2_g), row(l1_ln2_b))
    return _tail1(gate, hf, hr, xc, mod1, tail1_consts)
```

```python
import functools
import math

import jax
import jax.numpy as jnp
from jax import lax
from jax.experimental import pallas as pl
from jax.experimental.pallas import tpu as pltpu

F32 = jnp.float32
BF16 = jnp.bfloat16

D = 1024
DEPTH = 2
N_MOD = 6
ALPHA = (2.0 * DEPTH) ** 0.25
LN_EPS = 1e-5
RMS_EPS = 1e-6

HEADS = 8
MLA_NOPE = 64
MLA_ROPE = 32
MLA_V = 64
MLA_Q_RANK = 384
MLA_KV_RANK = 256
ROPE_AXIS = MLA_ROPE // 2
ROPE_THETA = 10000.0
GRID_W = 64
ATT_SCALE = (MLA_NOPE + MLA_ROPE) ** -0.5
LOG2_E = math.log2(math.e)
HEAD_PAD = 128
ATT_HEADS = 8

RW_HEAD = 64
RW_W = HEADS * RW_HEAD
LORA_W = 64
LORA_A = 64
LORA_G = 128
RW_IN = 3 * RW_W + 2 * LORA_W + 2 * LORA_A + LORA_G
GN_EPS = 64e-5
EXP_NEG_HALF = math.exp(-0.5)

LRU_W = D
LRU_BLOCKS = 8
LRU_BLOCK = LRU_W // LRU_BLOCKS
LRU_C = 8.0
D_FF = 4 * D

TM = 256
SUB = 8
SCAN_TC = 72
SCAN_TC_SMALL = 32
RELAY_T = 128
RELAY_FB = 4
MLP_CHUNK = 1024
PITCH_F = RW_HEAD + SUB
PITCH_H = LRU_BLOCKS * SUB + SUB
LRU_TL = 64
LRU_TB = 128
VMEM_LIMIT = 56 * 1024 * 1024

IN0_COLS = MLA_Q_RANK + MLA_KV_RANK + 2 * HEAD_PAD + RW_IN


def _params(sem):
    return pltpu.CompilerParams(dimension_semantics=sem, vmem_limit_bytes=VMEM_LIMIT)


def _bdot(a, w):
    return jnp.dot(a.astype(BF16), w, preferred_element_type=F32)


def _sigmoid(x):
    return 0.5 * jnp.tanh(0.5 * x) + 0.5


def _layer_norm(z, g, b):
    mu = jnp.mean(z, axis=-1, keepdims=True)
    zc = z - mu
    var = jnp.mean(zc * zc, axis=-1, keepdims=True)
    return zc * lax.rsqrt(var + LN_EPS) * g + b


def _head_sum(x, ones_bd):
    hi = x.astype(BF16)
    lo = (x - hi.astype(F32)).astype(BF16)
    return (jnp.dot(hi, ones_bd, preferred_element_type=F32)
            + jnp.dot(lo, ones_bd, preferred_element_type=F32))


def _const_spec(shape):
    nd = len(shape)
    return pl.BlockSpec(shape, lambda *_: (0,) * nd)


def _weight_spec(shape):
    nd = len(shape)
    return pl.BlockSpec(shape, lambda *_: (0,) * nd, pipeline_mode=pl.Buffered(1))


def _mod_kernel(c_ref, w_ref, b_ref, o_ref):
    c = c_ref[...]
    s = c * _sigmoid(c)
    o_ref[...] = jnp.dot(s, w_ref[...], precision=lax.Precision.HIGHEST,
                         preferred_element_type=F32) + b_ref[...]


def _mod_table(c, c_ctx, mod_w, mod_b):
    b = c.shape[0]
    rows = 16
    cc = jnp.zeros((rows, D), F32).at[:b].set(c).at[b].set(c_ctx)
    tn = 1024
    out = pl.pallas_call(
        _mod_kernel,
        grid=(N_MOD * D // tn,),
        in_specs=[pl.BlockSpec((rows, D), lambda j: (0, 0)),
                  pl.BlockSpec((D, tn), lambda j: (0, j)),
                  pl.BlockSpec((1, tn), lambda j: (0, j))],
        out_specs=pl.BlockSpec((rows, tn), lambda j: (0, j)),
        out_shape=jax.ShapeDtypeStruct((rows, N_MOD * D), F32),
        compiler_params=_params(("arbitrary",)),
        name="mod_proj",
    )(cc, mod_w, mod_b.reshape(1, -1))
    lat = out[:b].reshape(b, N_MOD, D)
    ctx = jnp.broadcast_to(out[b].reshape(1, N_MOD, D), (b, N_MOD, D))
    tab = jnp.stack([ctx, lat], axis=1)
    return jnp.pad(tab, ((0, 0), (0, 0), (0, SUB - N_MOD), (0, 0)))


def _mod_spec(combined):
    if combined:
        return pl.BlockSpec((1, 1, SUB, D), lambda b, i: (b, jnp.minimum(i, 1), 0, 0))
    return pl.BlockSpec((1, 1, SUB, D), lambda b, i: (b, 1, 0, 0))


def _front0_kernel(ctx_ref, x_ref, xp_ref, xn_ref, mod_ref, win_ref, qn_ref, kvn_ref, wqa_ref, wqb_ref, wuk_ref, wuv_ref,
                   cs_ref, sn_ref, mu_ref, w0_ref, w2_ref, a0_ref, a2_ref, g2_ref, kk_ref, ka_ref, rk_ref, ones_ref,
                   q_ref, k_ref, v_ref, sf_ref, g_ref, bonus_ref):
    i = pl.program_id(1)
    has_prev, has_next = _halo_flags(i, pl.num_programs(1))
    m = mod_ref[0, 0]
    xt = jnp.where(i == 0, ctx_ref[0], x_ref[0])
    xe = jnp.concatenate([xt, xp_ref[0], xn_ref[0]], axis=0)
    h = xe * (1.0 + m[1:2]) + m[0:1]
    fe = _bdot(h, win_ref[...])
    f = fe[:TM]
    o_kv = MLA_Q_RANK
    o_kr = o_kv + MLA_KV_RANK
    o_rw = o_kr + 2 * HEAD_PAD
    prow = jnp.where(has_prev, fe[TM + SUB - 1:TM + SUB, o_rw:], 0.0)
    nrow = jnp.where(has_next, fe[TM + SUB:TM + SUB + 1, o_rw:], 0.0)
    _rw_features(f[:, o_rw:], prow, nrow, mu_ref, w0_ref, w2_ref, a0_ref, a2_ref, g2_ref, kk_ref, ka_ref, rk_ref,
                 ones_ref, sf_ref, g_ref, bonus_ref)
    fq = f[:, :o_kv]
    fkv = f[:, o_kv:o_kr]
    fkr = f[:, o_kr:o_kr + HEAD_PAD]
    fkr_rot = f[:, o_kr + HEAD_PAD:o_rw]
    qn = fq * lax.rsqrt(jnp.mean(fq * fq, axis=-1, keepdims=True) + RMS_EPS) * qn_ref[...]
    ckv = fkv * lax.rsqrt(jnp.mean(fkv * fkv, axis=-1, keepdims=True) + RMS_EPS) * kvn_ref[...]
    qn = qn.astype(BF16)
    ckv = ckv.astype(BF16)
    qa = jnp.dot(qn, wqa_ref[...], preferred_element_type=F32)
    qb = jnp.dot(qn, wqb_ref[...], preferred_element_type=F32)
    kn = jnp.dot(ckv, wuk_ref[...], preferred_element_type=F32)
    cs = cs_ref[...]
    sn = sn_ref[...]
    kr = fkr * cs + fkr_rot * sn
    for hd in range(HEADS):
        sl = slice(hd * HEAD_PAD, (hd + 1) * HEAD_PAD)
        q_ref[0, :, sl] = ((qa[:, sl] * cs + qb[:, sl] * sn) * (ATT_SCALE * LOG2_E)).astype(BF16)
        k_ref[0, :, sl] = (kn[:, sl] + kr).astype(BF16)
    v_ref[0] = jnp.dot(ckv, wuv_ref[...], preferred_element_type=F32).astype(BF16)


def _seg_specs(n):
    ctx_spec = pl.BlockSpec((1, TM, D), lambda bb, i: (bb, 0, 0))
    lat_spec = pl.BlockSpec((1, TM, D), lambda bb, i: (bb, jnp.maximum(i - 1, 0), 0))
    return ctx_spec, lat_spec


def _front0(ctx, x, modt, mla_consts, cs, sn, rw_consts):
    b, n, _ = x.shape
    t = n + TM
    row = lambda w: pl.BlockSpec((1, TM, w), lambda bb, i: (bb, i, 0))
    ctx_spec, lat_spec = _seg_specs(n)
    hb = TM // SUB
    prev = pl.BlockSpec((1, SUB, D), lambda bb, i: (bb, jnp.maximum((i - 1) * hb - 1, 0), 0))
    nxt = pl.BlockSpec((1, SUB, D), lambda bb, i: (bb, jnp.minimum(i * hb, n // SUB - 1), 0))
    rope = pl.BlockSpec((TM, HEAD_PAD), lambda bb, i: (i, 0))
    return pl.pallas_call(
        _front0_kernel,
        grid=(b, t // TM),
        in_specs=[ctx_spec, lat_spec, prev, nxt, _mod_spec(True)] + [_weight_spec(c.shape) for c in mla_consts]
                 + [rope, rope] + [_weight_spec(c.shape) for c in rw_consts],
        out_specs=[row(HEADS * HEAD_PAD), row(HEADS * HEAD_PAD), row(HEADS * MLA_V),
                   pl.BlockSpec((1, 9, HEADS * PITCH_F, TM), lambda bb, i: (bb, 0, 0, i)), row(RW_W), row(RW_W)],
        out_shape=[jax.ShapeDtypeStruct((b, t, HEADS * HEAD_PAD), BF16),
                   jax.ShapeDtypeStruct((b, t, HEADS * HEAD_PAD), BF16),
                   jax.ShapeDtypeStruct((b, t, HEADS * MLA_V), BF16),
                   jax.ShapeDtypeStruct((b, 9, HEADS * PITCH_F, t), F32),
                   jax.ShapeDtypeStruct((b, t, RW_W), F32),
                   jax.ShapeDtypeStruct((b, t, RW_W), F32)],
        compiler_params=_params(("parallel", "parallel")),
        name="l0_front",
    )(ctx, x, x, x, modt, *mla_consts, cs, sn, *rw_consts)


def _att_kernel(q_ref, k_ref, v_ref, o_ref, *, n_ctx, n_all):
    i = pl.program_id(2)

    def attend(nk):
        lane = lax.broadcasted_iota(jnp.int32, (nk, 2 * MLA_V), 1)
        for pair in range(ATT_HEADS // 2):
            v = v_ref[0, :nk, pair * 2 * MLA_V:(pair + 1) * 2 * MLA_V]
            acc = None
            for hh in range(2):
                hd = 2 * pair + hh
                sl = slice(hd * HEAD_PAD, (hd + 1) * HEAD_PAD)
                q = q_ref[0, :, sl]
                k = k_ref[0, :nk, sl]
                s = lax.dot_general(q, k, (((1,), (1,)), ((), ())), preferred_element_type=F32)
                p = jnp.exp2(s - jnp.max(s, axis=-1, keepdims=True))
                l = jnp.sum(p, axis=-1, keepdims=True)
                keep = (lane < MLA_V) if hh == 0 else (lane >= MLA_V)
                vh = jnp.where(keep, v, jnp.zeros_like(v))
                o = jnp.dot(p.astype(BF16), vh, preferred_element_type=F32) / l
                acc = o if acc is None else acc + o
            o_ref[0, :, pair * 2 * MLA_V:(pair + 1) * 2 * MLA_V] = acc.astype(o_ref.dtype)

    @pl.when(i == 0)
    def _():
        attend(n_ctx)

    @pl.when(i > 0)
    def _():
        attend(n_all)


def _attention(q, k, v):
    b, t, _ = q.shape
    return pl.pallas_call(
        functools.partial(_att_kernel, n_ctx=TM, n_all=t),
        grid=(b, HEADS // ATT_HEADS, t // TM),
        in_specs=[pl.BlockSpec((1, TM, ATT_HEADS * HEAD_PAD), lambda bb, hp, i: (bb, i, hp)),
                  pl.BlockSpec((1, t, ATT_HEADS * HEAD_PAD), lambda bb, hp, i: (bb, 0, hp)),
                  pl.BlockSpec((1, t, ATT_HEADS * MLA_V), lambda bb, hp, i: (bb, 0, hp))],
        out_specs=pl.BlockSpec((1, TM, ATT_HEADS * MLA_V), lambda bb, hp, i: (bb, i, hp)),
        out_shape=jax.ShapeDtypeStruct((b, t, HEADS * MLA_V), BF16),
        compiler_params=_params(("parallel", "parallel", "parallel")),
        name="l0_attention",
    )(q, k, v)


def _halo_flags(i, n_tiles):
    return i >= 2, jnp.logical_and(i >= 1, i < n_tiles - 1)


def _rw_features(f, prow, nrow, mu_ref, w0_ref, w2_ref, a0_ref, a2_ref, g2_ref,
                 kk_ref, ka_ref, rk_ref, ones_ref, sf_ref, g_ref, bonus_ref):
    rid = lax.broadcasted_iota(jnp.int32, (TM, 1), 0)
    prev = jnp.where(rid == 0, prow, pltpu.roll(f, 1, axis=0))
    nxt = jnp.where(rid == TM - 1, nrow, pltpu.roll(f, TM - 1, axis=0))
    f = f + mu_ref[...] * (0.5 * (prev + nxt) - f)
    r = f[:, 0:RW_W]
    k = f[:, RW_W:2 * RW_W]
    v = f[:, 2 * RW_W:3 * RW_W]
    o = 3 * RW_W
    wl = f[:, o:o + 2 * LORA_W]
    al = f[:, o + 2 * LORA_W:o + 2 * LORA_W + 2 * LORA_A]
    gl = f[:, o + 2 * LORA_W + 2 * LORA_A:]
    w_raw = w0_ref[...] + _bdot(jnp.tanh(wl), w2_ref[...])
    decay = jnp.exp(-EXP_NEG_HALF * _sigmoid(w_raw))
    a = _sigmoid(a0_ref[...] + _bdot(al, a2_ref[...]))
    g_ref[0] = _bdot(_sigmoid(gl), g2_ref[...])
    ones_bd = ones_ref[...]
    kk = k * kk_ref[...]
    kk = kk * lax.rsqrt(_head_sum(kk * kk, ones_bd) + 1e-12)
    ka = ka_ref[...]
    kd = [k * (1.0 + (a[:, d * RW_W:(d + 1) * RW_W] - 1.0) * ka) for d in range(2)]
    bonus_ref[0] = _head_sum(r * (kd[0] + kd[1]) * rk_ref[...], ones_bd) * v
    parts = [r, kk, v, decay[:, :RW_W], kd[0], kk * a[:, :RW_W],
             decay[:, RW_W:], kd[1], kk * a[:, RW_W:]]
    pad = jnp.zeros((PITCH_F - RW_HEAD, TM), F32)
    for n, p in enumerate(parts):
        for hp in range(HEADS // 2):
            for half in range(TM // 128):
                pt = p[half * 128:(half + 1) * 128, hp * 128:(hp + 1) * 128].T
                for h2 in range(2):
                    row0 = (2 * hp + h2) * PITCH_F
                    sf_ref[0, n, row0:row0 + RW_HEAD, half * 128:(half + 1) * 128] = pt[h2 * RW_HEAD:(h2 + 1) * RW_HEAD]
        for hd in range(HEADS):
            sf_ref[0, n, hd * PITCH_F + RW_HEAD:(hd + 1) * PITCH_F, :] = pad


def _time_mirror(i, n_ctx_tiles, n_tiles):
    return jnp.where(i < n_ctx_tiles, n_ctx_tiles - 1 - i, n_tiles + n_ctx_tiles - 1 - i)


def _reverse_lanes(x, j3):
    hi = x.astype(BF16)
    r1 = x - hi.astype(F32)
    mid = r1.astype(BF16)
    lo = (r1 - mid.astype(F32)).astype(BF16)
    return jnp.dot(jnp.concatenate([hi, mid, lo], axis=1), j3, preferred_element_type=F32)


def _relayout_in_kernel(xa_ref, xb_ref, j3_ref, o_ref, *, step_major):
    nb = xa_ref.shape[0]
    tr = xa_ref.shape[3]
    half = nb * HEADS

    def gather(x_ref, f):
        return jnp.concatenate([x_ref[b, 0, pl.ds(f, HEADS, stride=PITCH_F), :] for b in range(nb)], axis=0)

    for f0 in range(0, RW_HEAD, RELAY_FB):
        fs = range(f0, f0 + RELAY_FB)
        rev = _reverse_lanes(jnp.concatenate([gather(xb_ref, f) for f in fs], axis=0), j3_ref[...])
        for n, f in enumerate(fs):
            mt = jnp.concatenate([gather(xa_ref, f), rev[n * half:(n + 1) * half]], axis=0).T
            if step_major:
                o_ref[pl.ds(f, tr, stride=PITCH_F), :] = mt
            else:
                o_ref[0, f] = mt
    if step_major:
        for f in range(RW_HEAD, PITCH_F):
            o_ref[pl.ds(f, tr, stride=PITCH_F), :] = jnp.zeros((tr, 2 * half), F32)


def _relayout_in(sf, j3):
    b, _, rows, t = sf.shape
    chains = 2 * b * HEADS
    nt = t // RELAY_T
    nc = TM // RELAY_T
    mirror = lambda i: _time_mirror(i, nc, nt)
    blk = (b, 1, rows, RELAY_T)
    dirdep = lambda p: jnp.minimum(p // 2, 1)
    x = pl.pallas_call(
        functools.partial(_relayout_in_kernel, step_major=False),
        grid=(nt, 5),
        in_specs=[pl.BlockSpec(blk, lambda i, p: (0, p + dirdep(p), 0, i)),
                  pl.BlockSpec(blk, lambda i, p: (0, p + 4 * dirdep(p), 0, mirror(i))),
                  _const_spec(j3.shape)],
        out_specs=pl.BlockSpec((1, RW_HEAD, RELAY_T, chains), lambda i, p: (p, 0, i, 0)),
        out_shape=jax.ShapeDtypeStruct((5, RW_HEAD, t, chains), F32),
        compiler_params=_params(("parallel", "parallel")),
        name="l0_rwkv_relayout_in",
    )(sf, sf, j3)
    v = pl.pallas_call(
        functools.partial(_relayout_in_kernel, step_major=True),
        grid=(nt,),
        in_specs=[pl.BlockSpec(blk, lambda i: (0, 2, 0, i)),
                  pl.BlockSpec(blk, lambda i: (0, 2, 0, mirror(i))),
                  _const_spec(j3.shape)],
        out_specs=pl.BlockSpec((RELAY_T * PITCH_F, chains), lambda i: (i, 0)),
        out_shape=jax.ShapeDtypeStruct((t * PITCH_F, chains), F32),
        compiler_params=_params(("parallel",)),
        name="l0_rwkv_relayout_v",
    )(sf, sf, j3)
    return x, v


def _rwscan_kernel(x_ref, v_ref, y_ref, s_ref):
    tc = x_ref.shape[2]
    chains = x_ref.shape[3]

    @pl.when(pl.program_id(0) == 0)
    def _():
        s_ref[...] = jnp.zeros_like(s_ref)

    groups = RW_HEAD // SUB
    zeros = tuple(jnp.zeros((SUB, chains), F32) for _ in range(groups))

    def row(a, j, s):
        return jnp.broadcast_to(x_ref[a, j, pl.ds(s, 1), :], (SUB, chains))

    def s_at(j, g):
        return s_ref.at[j, g * SUB:(g + 1) * SUB, :]

    sa0 = list(zeros)
    for j in range(RW_HEAD):
        kk = row(1, j, 0)
        for g in range(groups):
            sa0[g] = sa0[g] + s_at(j, g)[...] * kk

    def step(s, sa):
        base = pl.multiple_of(s * PITCH_F, SUB)
        v = v_ref[pl.ds(base, RW_HEAD), :]
        s_next = jnp.minimum(s + 1, tc - 1)
        y = list(zeros)
        sa_next = list(zeros)
        for j in range(RW_HEAD):
            r, kk_next, w, k, kka = row(0, j, s), row(1, j, s_next), row(2, j, s), row(3, j, s), row(4, j, s)
            for g in range(groups):
                sj = s_at(j, g)[...] * w - sa[g] * kka + v[g * SUB:(g + 1) * SUB] * k
                s_at(j, g)[...] = sj
                y[g] = y[g] + sj * r
                sa_next[g] = sa_next[g] + sj * kk_next
        y_ref[pl.ds(base, RW_HEAD), :] = jnp.concatenate(y, axis=0)
        y_ref[pl.ds(base + RW_HEAD, PITCH_F - RW_HEAD), :] = zeros[0]
        return tuple(sa_next)

    lax.fori_loop(0, tc, step, tuple(sa0))


def _rwscan(x, v):
    _, _, t, chains = x.shape
    tc = SCAN_TC if t % SCAN_TC == 0 else SCAN_TC_SMALL
    return pl.pallas_call(
        _rwscan_kernel,
        grid=(t // tc,),
        in_specs=[pl.BlockSpec((5, RW_HEAD, tc, chains), lambda i: (0, 0, i, 0)),
                  pl.BlockSpec((tc * PITCH_F, chains), lambda i: (i, 0))],
        out_specs=pl.BlockSpec((tc * PITCH_F, chains), lambda i: (i, 0)),
        out_shape=jax.ShapeDtypeStruct((t * PITCH_F, chains), F32),
        scratch_shapes=[pltpu.VMEM((RW_HEAD, RW_HEAD, chains), F32)],
        compiler_params=_params(("arbitrary",)),
        name="l0_rwkv_scan",
    )(x, v)


def _relayout_out_kernel(yf_ref, yr_ref, j3_ref, o_ref, q_ref):
    nb = o_ref.shape[0]
    tr = o_ref.shape[1]
    chains = yf_ref.shape[1]
    half = chains // 2
    pitch_c = chains + SUB
    fwd = lax.broadcasted_iota(jnp.int32, (tr, chains), 1) < half
    for i in range(RW_HEAD):
        rows = pl.ds(i, tr, stride=PITCH_F)
        mt = jnp.where(fwd, yf_ref[rows, :], yr_ref[rows, :]).T
        q_ref[i * pitch_c:i * pitch_c + half, :] = mt[:half]
        q_ref[i * pitch_c + half:i * pitch_c + chains, :] = _reverse_lanes(mt[half:], j3_ref[...])
    for b in range(nb):
        for hp in range(HEADS // 2):
            parts = []
            for h2 in range(2):
                c = b * HEADS + 2 * hp + h2
                parts.append(q_ref[pl.ds(c, RW_HEAD, stride=pitch_c), :]
                             + q_ref[pl.ds(half + c, RW_HEAD, stride=pitch_c), :])
            o_ref[b, :, hp * 128:(hp + 1) * 128] = jnp.concatenate(parts, axis=0).T


def _relayout_out(y, j3, b):
    chains = y.shape[1]
    t = y.shape[0] // PITCH_F
    nt = t // RELAY_T
    nc = TM // RELAY_T
    return pl.pallas_call(
        _relayout_out_kernel,
        grid=(nt,),
        in_specs=[pl.BlockSpec((RELAY_T * PITCH_F, chains), lambda i: (i, 0)),
                  pl.BlockSpec((RELAY_T * PITCH_F, chains), lambda i: (_time_mirror(i, nc, nt), 0)),
                  _const_spec(j3.shape)],
        out_specs=pl.BlockSpec((b, RELAY_T, RW_W), lambda i: (0, i, 0)),
        out_shape=jax.ShapeDtypeStruct((b, t, RW_W), F32),
        scratch_shapes=[pltpu.VMEM((RW_HEAD * (chains + SUB), RELAY_T), F32)],
        compiler_params=_params(("parallel",)),
        name="l0_rwkv_relayout_out",
    )(y, y, j3)


def _rwkv_scan_both(sf):
    k = jnp.arange(RELAY_T)
    anti = (k[:, None] + k[None, :] == RELAY_T - 1).astype(BF16)
    j3 = jnp.concatenate([anti, anti, anti], axis=0)
    y = _rwscan(*_relayout_in(sf, j3))
    return _relayout_out(y, j3, sf.shape[0])


def _mlp_hidden(h, w1_ref, w2_ref):
    acc = None
    for c in range(D_FF // MLP_CHUNK):
        u = jnp.dot(h, w1_ref[:, c * MLP_CHUNK:(c + 1) * MLP_CHUNK], preferred_element_type=F32)
        u = jnp.square(jnp.maximum(u, 0.0)).astype(BF16)
        part = jnp.dot(u, w2_ref[c * MLP_CHUNK:(c + 1) * MLP_CHUNK, :], preferred_element_type=F32)
        acc = part if acc is None else acc + part
    return acc


def _tail0_kernel(att_ref, ys_ref, bonus_ref, g_ref, ctx_ref, x_ref, mod_ref, mod1_ref, gnw_ref, gnb_ref, ones_ref,
                  woa_ref, wob_ref, l1g_ref, l1b_ref, w1_ref, w2_ref, l2g_ref, l2b_ref, win1_ref,
                  x2_ref, gate_ref, xr_ref):
    m = mod_ref[0, 0]
    ones_bd = ones_ref[...]
    y = ys_ref[0]
    mu = _head_sum(y, ones_bd) * (1.0 / RW_HEAD)
    yc = y - mu
    var = _head_sum(yc * yc, ones_bd) * (1.0 / RW_HEAD)
    yn = yc * lax.rsqrt(var + GN_EPS) * gnw_ref[...] + gnb_ref[...]
    rw = (yn + bonus_ref[0]) * g_ref[0]
    o = (jnp.dot(att_ref[0], woa_ref[...], preferred_element_type=F32)
         + _bdot(rw, wob_ref[...]))
    x0 = jnp.where(pl.program_id(1) == 0, ctx_ref[0], x_ref[0])
    x1 = _layer_norm(ALPHA * x0 + m[2:3] * o, l1g_ref[...], l1b_ref[...])
    acc = _mlp_hidden((x1 * (1.0 + m[4:5]) + m[3:4]).astype(BF16), w1_ref, w2_ref)
    x2 = _layer_norm(ALPHA * x1 + m[5:6] * acc, l2g_ref[...], l2b_ref[...])
    x2_ref[0] = x2
    m1 = mod1_ref[0, 0]
    f = _bdot(x2 * (1.0 + m1[1:2]) + m1[0:1], win1_ref[...])
    gate_ref[0] = f[:, :LRU_W]
    xr_ref[0] = f[:, LRU_W:]


def _tail0(att, ys, bonus, g, ctx, x, mod0, mod1, consts):
    b, n, _ = x.shape
    t = n + TM
    row = lambda w: pl.BlockSpec((1, TM, w), lambda bb, i: (bb, i, 0))
    ctx_spec, lat_spec = _seg_specs(n)
    return pl.pallas_call(
        _tail0_kernel,
        grid=(b, t // TM),
        in_specs=[row(RW_W), row(RW_W), row(RW_W), row(RW_W), ctx_spec, lat_spec, _mod_spec(True), _mod_spec(True)]
                 + [_weight_spec(c.shape) for c in consts],
        out_specs=[row(D), row(LRU_W), row(LRU_W)],
        out_shape=[jax.ShapeDtypeStruct((b, t, D), F32), jax.ShapeDtypeStruct((b, t, LRU_W), F32),
                   jax.ShapeDtypeStruct((b, t, LRU_W), F32)],
        compiler_params=_params(("parallel", "parallel")),
        name="l0_tail",
    )(att, ys, bonus, g, ctx, x, mod0, mod1, *consts)


def _lru_fwd_kernel(x_ref, xp_ref, xn_ref, cw_ref, cb_ref, wbd_ref, gb_ref, lam_ref,
                    hf_ref, a1_ref, u1_ref, a_scr, u_scr, c_ref):
    i = pl.program_id(0)
    n = pl.num_programs(0)
    nb, tl, _ = x_ref.shape
    rows = nb * tl
    nc = TM // tl
    pitch_b = tl + SUB

    @pl.when(i == 0)
    def _():
        c_ref[...] = jnp.zeros_like(c_ref)

    lam = lam_ref[...]
    nl = -lam
    softplus = jnp.maximum(nl, 0.0) + jnp.log1p(jnp.exp(-jnp.abs(nl)))
    gb = gb_ref[...]
    cw = cw_ref[...]
    tpos = jnp.bitwise_and(lax.broadcasted_iota(jnp.int32, (rows, 1), 0), tl - 1)
    seg_first = jnp.logical_or(i == 0, i == nc)
    seg_last = jnp.logical_or(i == nc - 1, i == n - 1)

    def halo(ref, r, edge):
        per_b = [jnp.broadcast_to(ref[b, r:r + 1, :], (tl, LRU_W)) for b in range(nb)]
        return jnp.where(edge, 0.0, jnp.concatenate(per_b, axis=0))

    x = x_ref[...].reshape(rows, LRU_W)
    p1 = halo(xp_ref, SUB - 1, seg_first)
    p2 = halo(xp_ref, SUB - 2, seg_first)
    n1 = halo(xn_ref, 0, seg_last)
    xm1 = jnp.where(tpos == 0, p1, pltpu.roll(x, 1, axis=0))
    xm2 = jnp.where(tpos == 0, p2, jnp.where(tpos == 1, p1, pltpu.roll(x, 2, axis=0)))
    xp1 = jnp.where(tpos == tl - 1, n1, pltpu.roll(x, rows - 1, axis=0))
    xc = cw[0:1] * xm2 + cw[1:2] * xm1 + cw[2:3] * x + cw[3:4] * xp1 + cb_ref[...]
    for blk in range(LRU_BLOCKS):
        sl = slice(blk * LRU_BLOCK, (blk + 1) * LRU_BLOCK)
        xb = xc[:, sl]
        z = _bdot(xb, wbd_ref[blk])
        for d in range(2):
            zr = z[:, 2 * d * LRU_BLOCK:(2 * d + 1) * LRU_BLOCK] + gb[2 * d:2 * d + 1, sl]
            zi = z[:, (2 * d + 1) * LRU_BLOCK:(2 * d + 2) * LRU_BLOCK] + gb[2 * d + 1:2 * d + 2, sl]
            log_a = -LRU_C * _sigmoid(zr) * softplus[d:d + 1, sl]
            a = jnp.exp(log_a)
            u = jnp.sqrt(-jnp.tanh(log_a) * (a * a + 1.0)) * (_sigmoid(zi) * xb)
            for b in range(nb):
                a_scr[d, blk, b * pitch_b:b * pitch_b + tl, :] = a[b * tl:(b + 1) * tl]
                u_scr[d, blk, b * pitch_b:b * pitch_b + tl, :] = u[b * tl:(b + 1) * tl]
    h = [c_ref[:, blk * LRU_BLOCK:(blk + 1) * LRU_BLOCK] for blk in range(LRU_BLOCKS)]
    pad = jnp.zeros((SUB, LRU_BLOCK), F32)
    for t in range(tl):
        for blk in range(LRU_BLOCKS):
            step_rows = pl.ds(t, nb, stride=pitch_b)
            dst = slice(t * PITCH_H + blk * SUB, t * PITCH_H + blk * SUB + nb)
            h[blk] = a_scr[0, blk, step_rows, :] * h[blk] + u_scr[0, blk, step_rows, :]
            hf_ref[dst, :] = h[blk]
            a1_ref[dst, :] = a_scr[1, blk, step_rows, :]
            u1_ref[dst, :] = u_scr[1, blk, step_rows, :]
        for ref in (hf_ref, a1_ref, u1_ref):
            ref[t * PITCH_H + LRU_BLOCKS * SUB:(t + 1) * PITCH_H, :] = pad
    for blk in range(LRU_BLOCKS):
        c_ref[:, blk * LRU_BLOCK:(blk + 1) * LRU_BLOCK] = h[blk]


def _lru_bwd_kernel(a_ref, u_ref, h_ref, c_ref, *, nb):
    tb = a_ref.shape[0] // PITCH_H

    @pl.when(pl.program_id(0) == 0)
    def _():
        c_ref[...] = jnp.zeros_like(c_ref)

    def step(s, h):
        base = pl.multiple_of((tb - 1 - s) * PITCH_H, SUB)
        out = []
        for blk in range(LRU_BLOCKS):
            rows = pl.ds(base + blk * SUB, nb)
            hb = a_ref[rows, :] * h[blk] + u_ref[rows, :]
            h_ref[rows, :] = hb
            out.append(hb)
        h_ref[pl.ds(base + LRU_BLOCKS * SUB, SUB), :] = jnp.zeros((SUB, LRU_BLOCK), F32)
        return tuple(out)

    h0 = tuple(c_ref[blk] for blk in range(LRU_BLOCKS))
    h = lax.fori_loop(0, tb, step, h0)
    for blk in range(LRU_BLOCKS):
        c_ref[blk] = h[blk]


def _lru(xr, cw, cb, wbd, gb, lam):
    b, t, _ = xr.shape
    assert b <= SUB
    n = t // LRU_TL
    hb = LRU_TL // SUB
    last = t // SUB - 1
    consts = (cw, cb, wbd, gb, lam)
    hspec = pl.BlockSpec((LRU_TL * PITCH_H, LRU_BLOCK), lambda i: (i, 0))
    coef = jax.ShapeDtypeStruct((t * PITCH_H, LRU_BLOCK), F32)
    scr = pltpu.VMEM((2, LRU_BLOCKS, b * (LRU_TL + SUB), LRU_BLOCK), F32)
    hf, a1, u1 = pl.pallas_call(
        _lru_fwd_kernel,
        grid=(n,),
        in_specs=[pl.BlockSpec((b, LRU_TL, LRU_W), lambda i: (0, i, 0)),
                  pl.BlockSpec((b, SUB, LRU_W), lambda i: (0, jnp.maximum(i * hb - 1, 0), 0)),
                  pl.BlockSpec((b, SUB, LRU_W), lambda i: (0, jnp.minimum((i + 1) * hb, last), 0))]
                 + [_const_spec(c.shape) for c in consts],
        out_specs=[hspec, hspec, hspec],
        out_shape=[coef, coef, coef],
        scratch_shapes=[scr, scr, pltpu.VMEM((b, LRU_W), F32)],
        compiler_params=_params(("arbitrary",)),
        name="l1_lru_fwd",
    )(xr, xr, xr, *consts)
    nt = t // LRU_TB
    nc = TM // LRU_TB
    bspec = pl.BlockSpec((LRU_TB * PITCH_H, LRU_BLOCK), lambda i: (_time_mirror(i, nc, nt), 0))
    hr = pl.pallas_call(
        functools.partial(_lru_bwd_kernel, nb=b),
        grid=(nt,),
        in_specs=[bspec, bspec],
        out_specs=bspec,
        out_shape=coef,
        scratch_shapes=[pltpu.VMEM((LRU_BLOCKS, b, LRU_BLOCK), F32)],
        compiler_params=_params(("arbitrary",)),
        name="l1_lru_bwd",
    )(a1, u1)
    return hf, hr


def _tail1_kernel(gate_ref, hf_ref, hr_ref, x_ref, mod_ref, w_ref, l1g_ref, l1b_ref, w1_ref, w2_ref, l2g_ref, l2b_ref,
                  o_ref):
    nb, tl, _ = gate_ref.shape
    per_b = []
    for b in range(nb):
        cols = []
        for blk in range(LRU_BLOCKS):
            rows = pl.ds(blk * SUB + b, tl, stride=PITCH_H)
            cols.append(hf_ref[rows, :] + hr_ref[rows, :])
        per_b.append(jnp.concatenate(cols, axis=1))
    h = jnp.concatenate(per_b, axis=0)
    gate = gate_ref[...].reshape(nb * tl, LRU_W)
    gelu = 0.5 * gate * (1.0 + jnp.tanh(math.sqrt(2.0 / math.pi) * (gate + 0.044715 * gate * gate * gate)))
    o = _bdot(gelu * h, w_ref[...])
    mods = [mod_ref[b, 0] for b in range(nb)]
    x1 = [_layer_norm(ALPHA * x_ref[b] + mods[b][2:3] * o[b * tl:(b + 1) * tl], l1g_ref[...], l1b_ref[...])
          for b in range(nb)]
    hm = jnp.concatenate([x1[b] * (1.0 + mods[b][4:5]) + mods[b][3:4] for b in range(nb)], axis=0)
    acc = _mlp_hidden(hm.astype(BF16), w1_ref, w2_ref)
    for b in range(nb):
        z = ALPHA * x1[b] + mods[b][5:6] * acc[b * tl:(b + 1) * tl]
        o_ref[b] = _layer_norm(z, l2g_ref[...], l2b_ref[...])


def _tail1(gate, hf, hr, xc, modt, consts):
    b, t, _ = xc.shape
    off = TM // LRU_TL
    nt = t // LRU_TL - off
    lat = lambda wd: pl.BlockSpec((b, LRU_TL, wd), lambda i: (0, i + off, 0))
    hspec = pl.BlockSpec((LRU_TL * PITCH_H, LRU_BLOCK), lambda i: (i + off, 0))
    return pl.pallas_call(
        _tail1_kernel,
        grid=(nt,),
        in_specs=[lat(LRU_W), hspec, hspec, lat(D),
                  pl.BlockSpec((b, 1, SUB, D), lambda i: (0, 1, 0, 0))]
                 + [_weight_spec(c.shape) for c in consts],
        out_specs=pl.BlockSpec((b, LRU_TL, D), lambda i: (0, i, 0)),
        out_shape=jax.ShapeDtypeStruct((b, nt * LRU_TL, D), F32),
        compiler_params=_params(("parallel",)),
        name="l1_tail",
    )(gate, hf, hr, xc, modt, *consts)


def _rot_cols(w):
    ws = w.reshape(w.shape[:-1] + (2, 2, ROPE_AXIS // 2))
    return jnp.stack([-ws[..., 1, :], ws[..., 0, :]], axis=-2).reshape(w.shape)


def _rope_tables(n, n_ctx):
    rows_n = n // GRID_W
    rows = jnp.repeat(jnp.arange(rows_n, dtype=F32), GRID_W)
    cols = jnp.tile(jnp.arange(GRID_W, dtype=F32), rows_n)
    inv_freq = ROPE_THETA ** (-jnp.arange(0, ROPE_AXIS, 2, dtype=F32) / ROPE_AXIS)
    ang_r = rows[:, None] * inv_freq
    ang_c = cols[:, None] * inv_freq
    ang = jnp.concatenate([ang_r, ang_r, ang_c, ang_c], axis=-1)
    cos = jnp.concatenate([jnp.ones((n_ctx, MLA_ROPE), F32), jnp.cos(ang)], axis=0)
    sin = jnp.concatenate([jnp.zeros((n_ctx, MLA_ROPE), F32), jnp.sin(ang)], axis=0)
    t = n + n_ctx
    cs = jnp.concatenate([jnp.ones((t, MLA_NOPE), F32), cos, jnp.zeros((t, 32), F32)], axis=-1)
    sn = jnp.concatenate([jnp.zeros((t, MLA_NOPE), F32), sin, jnp.zeros((t, 32), F32)], axis=-1)
    return cs, sn


def _block_diag2(w):
    z = jnp.zeros_like(w[0])
    return jnp.concatenate([jnp.concatenate([w[0], z], axis=1), jnp.concatenate([z, w[1]], axis=1)], axis=0)


def kernel(x, c, ctx, c_ctx, l0_mod_w, l0_mod_b, l0_w_in, l0_mla_q_norm, l0_mla_w_uq, l0_mla_kv_norm, l0_mla_w_uk, l0_mla_w_uv, l0_rwkv_mu, l0_rwkv_w0, l0_rwkv_w2, l0_rwkv_a0, l0_rwkv_a2, l0_rwkv_g2, l0_rwkv_k_k, l0_rwkv_k_a, l0_rwkv_r_k, l0_rwkv_gn_w, l0_rwkv_gn_b, l0_w_out, l0_ln1_g, l0_ln1_b, l0_mlp_w1, l0_mlp_w2, l0_ln2_g, l0_ln2_b, l1_mod_w, l1_mod_b, l1_w_in, l1_conv_w, l1_conv_b, l1_lru_ga_w, l1_lru_ga_b, l1_lru_gx_w, l1_lru_gx_b, l1_lru_lambda, l1_w_out, l1_ln1_g, l1_ln1_b, l1_mlp_w1, l1_mlp_w2, l1_ln2_g, l1_ln2_b):
    b, n, _ = x.shape
    n_ctx = ctx.shape[1]
    assert n_ctx == TM and n % TM == 0 and x.shape[2] == D
    row = lambda v: v.reshape(1, -1)

    mod0 = _mod_table(c, c_ctx, l0_mod_w, l0_mod_b)
    mod1 = _mod_table(c, c_ctx, l1_mod_w, l1_mod_b)

    o_kv = MLA_Q_RANK
    o_kr = o_kv + MLA_KV_RANK
    o_rw = o_kr + MLA_ROPE
    w_kr = l0_w_in[:, o_kr:o_rw]
    zl = jnp.zeros((D, MLA_NOPE), F32)
    zr = jnp.zeros((D, HEAD_PAD - MLA_NOPE - MLA_ROPE), F32)
    win0 = jnp.concatenate([l0_w_in[:, :o_kr], zl, w_kr, zr, zl, _rot_cols(w_kr), zr,
                            l0_w_in[:, o_rw:]], axis=1).astype(BF16)
    wq = l0_mla_w_uq.reshape(MLA_Q_RANK, HEADS, MLA_NOPE + MLA_ROPE)
    q_nope, q_rope = wq[..., :MLA_NOPE], wq[..., MLA_NOPE:]
    zq = jnp.zeros((MLA_Q_RANK, HEADS, 32), F32)
    wqa = jnp.concatenate([q_nope, q_rope, zq], axis=-1).reshape(MLA_Q_RANK, HEADS * HEAD_PAD).astype(BF16)
    wqb = jnp.concatenate([jnp.zeros_like(q_nope), _rot_cols(q_rope), zq],
                          axis=-1).reshape(MLA_Q_RANK, HEADS * HEAD_PAD).astype(BF16)
    wk = l0_mla_w_uk.reshape(MLA_KV_RANK, HEADS, MLA_NOPE)
    wuk = jnp.concatenate([wk, jnp.zeros_like(wk)], axis=-1).reshape(MLA_KV_RANK, HEADS * HEAD_PAD).astype(BF16)
    wuv = l0_mla_w_uv.astype(BF16)
    cs, sn = _rope_tables(n, n_ctx)
    hid = jnp.arange(RW_W) // RW_HEAD
    ones_bd = (hid[:, None] == hid[None, :]).astype(BF16)

    mla_consts = (win0, row(l0_mla_q_norm), row(l0_mla_kv_norm), wqa, wqb, wuk, wuv)
    rw_consts = (row(l0_rwkv_mu), row(l0_rwkv_w0), _block_diag2(l0_rwkv_w2).astype(BF16), row(l0_rwkv_a0),
                 _block_diag2(l0_rwkv_a2).astype(BF16), l0_rwkv_g2.astype(BF16), row(l0_rwkv_k_k),
                 row(l0_rwkv_k_a), row(l0_rwkv_r_k), ones_bd)
    q, k, v, sf, g, bonus = _front0(ctx, x, mod0, mla_consts, cs, sn, rw_consts)
    att = _attention(q, k, v)
    ys = _rwkv_scan_both(sf)
    wo = l0_w_out.astype(BF16)
    tail0_consts = (row(l0_rwkv_gn_w), row(l0_rwkv_gn_b), ones_bd, wo[:HEADS * MLA_V], wo[HEADS * MLA_V:],
                    row(l0_ln1_g), row(l0_ln1_b), l0_mlp_w1.astype(BF16), l0_mlp_w2.astype(BF16),
                    row(l0_ln2_g), row(l0_ln2_b), l1_w_in.astype(BF16))
    xc, gate, xr = _tail0(att, ys, bonus, g, ctx, x, mod0, mod1, tail0_consts)

    wbd = jnp.concatenate([l1_lru_ga_w[0], l1_lru_gx_w[0], l1_lru_ga_w[1], l1_lru_gx_w[1]], axis=-1).astype(BF16)
    gb = jnp.stack([l1_lru_ga_b[0], l1_lru_gx_b[0], l1_lru_ga_b[1], l1_lru_gx_b[1]])
    hf, hr = _lru(xr, l1_conv_w, row(l1_conv_b), wbd, gb, l1_lru_lambda)
    tail1_consts = (l1_w_out.astype(BF16), row(l1_ln1_g), row(l1_ln1_b), l1_mlp_w1.astype(BF16),
                    l1_mlp_w2.astype(BF16), row(l1_ln2_g), row(l1_ln2_b))
    return _tail1(gate, hf, hr, xc, mod1, tail1_consts)
```

```python
import functools
import math

import jax
import jax.numpy as jnp
from jax import lax
from jax.experimental import pallas as pl
from jax.experimental.pallas import tpu as pltpu

F32 = jnp.float32
BF16 = jnp.bfloat16

D = 1024
DEPTH = 2
N_MOD = 6
ALPHA = (2.0 * DEPTH) ** 0.25
LN_EPS = 1e-5
RMS_EPS = 1e-6

HEADS = 8
MLA_NOPE = 64
MLA_ROPE = 32
MLA_V = 64
MLA_Q_RANK = 384
MLA_KV_RANK = 256
ROPE_AXIS = MLA_ROPE // 2
ROPE_THETA = 10000.0
GRID_W = 64
ATT_SCALE = (MLA_NOPE + MLA_ROPE) ** -0.5
LOG2_E = math.log2(math.e)
HEAD_PAD = 128
ATT_HEADS = 8

RW_HEAD = 64
RW_W = HEADS * RW_HEAD
LORA_W = 64
LORA_A = 64
LORA_G = 128
RW_IN = 3 * RW_W + 2 * LORA_W + 2 * LORA_A + LORA_G
GN_EPS = 64e-5
EXP_NEG_HALF = math.exp(-0.5)

LRU_W = D
LRU_BLOCKS = 8
LRU_BLOCK = LRU_W // LRU_BLOCKS
LRU_C = 8.0
D_FF = 4 * D

TM = 256
SUB = 8
SCAN_TC = 72
SCAN_TC_SMALL = 32
RELAY_T = 128
RELAY_FB = 4
MLP_CHUNK = 1024
PITCH_F = RW_HEAD + SUB
PITCH_H = LRU_BLOCKS * SUB + SUB
LRU_TL = 64
LRU_TB = 128
VMEM_LIMIT = 56 * 1024 * 1024
SINGLE_BUFFER_BYTES = 2 * 1024 * 1024

IN0_COLS = MLA_Q_RANK + MLA_KV_RANK + 2 * HEAD_PAD + RW_IN


def _params(sem):
    return pltpu.CompilerParams(dimension_semantics=sem, vmem_limit_bytes=VMEM_LIMIT)


def _bdot(a, w):
    return jnp.dot(a.astype(BF16), w, preferred_element_type=F32)


def _sigmoid(x):
    return 0.5 * jnp.tanh(0.5 * x) + 0.5


def _layer_norm(z, g, b):
    mu = jnp.mean(z, axis=-1, keepdims=True)
    zc = z - mu
    var = jnp.mean(zc * zc, axis=-1, keepdims=True)
    return zc * lax.rsqrt(var + LN_EPS) * g + b


def _head_sum(x, ones_bd):
    hi = x.astype(BF16)
    lo = (x - hi.astype(F32)).astype(BF16)
    return (jnp.dot(hi, ones_bd, preferred_element_type=F32)
            + jnp.dot(lo, ones_bd, preferred_element_type=F32))


def _const_spec(shape):
    nd = len(shape)
    return pl.BlockSpec(shape, lambda *_: (0,) * nd)


def _weight_spec(w):
    nd = w.ndim
    if w.size * w.dtype.itemsize >= SINGLE_BUFFER_BYTES:
        return pl.BlockSpec(w.shape, lambda *_: (0,) * nd, pipeline_mode=pl.Buffered(1))
    return pl.BlockSpec(w.shape, lambda *_: (0,) * nd)


def _mod_kernel(c_ref, w_ref, b_ref, o_ref):
    c = c_ref[...]
    s = c * _sigmoid(c)
    o_ref[...] = jnp.dot(s, w_ref[...], precision=lax.Precision.HIGHEST,
                         preferred_element_type=F32) + b_ref[...]


def _mod_table(c, c_ctx, mod_w, mod_b):
    b = c.shape[0]
    rows = 16
    cc = jnp.zeros((rows, D), F32).at[:b].set(c).at[b].set(c_ctx)
    tn = 1024
    out = pl.pallas_call(
        _mod_kernel,
        grid=(N_MOD * D // tn,),
        in_specs=[pl.BlockSpec((rows, D), lambda j: (0, 0)),
                  pl.BlockSpec((D, tn), lambda j: (0, j)),
                  pl.BlockSpec((1, tn), lambda j: (0, j))],
        out_specs=pl.BlockSpec((rows, tn), lambda j: (0, j)),
        out_shape=jax.ShapeDtypeStruct((rows, N_MOD * D), F32),
        compiler_params=_params(("arbitrary",)),
        name="mod_proj",
    )(cc, mod_w, mod_b.reshape(1, -1))
    lat = out[:b].reshape(b, N_MOD, D)
    ctx = jnp.broadcast_to(out[b].reshape(1, N_MOD, D), (b, N_MOD, D))
    tab = jnp.stack([ctx, lat], axis=1)
    return jnp.pad(tab, ((0, 0), (0, 0), (0, SUB - N_MOD), (0, 0)))


def _mod_spec(combined):
    if combined:
        return pl.BlockSpec((1, 1, SUB, D), lambda b, i: (b, jnp.minimum(i, 1), 0, 0))
    return pl.BlockSpec((1, 1, SUB, D), lambda b, i: (b, 1, 0, 0))


def _front0_kernel(ctx_ref, x_ref, xp_ref, xn_ref, mod_ref, win_ref, qn_ref, kvn_ref, wqa_ref, wqb_ref, wuk_ref, wuv_ref,
                   cs_ref, sn_ref, mu_ref, w0_ref, w2_ref, a0_ref, a2_ref, g2_ref, kk_ref, ka_ref, rk_ref, ones_ref,
                   q_ref, k_ref, v_ref, sf_ref, g_ref, bonus_ref):
    i = pl.program_id(1)
    has_prev, has_next = _halo_flags(i, pl.num_programs(1))
    m = mod_ref[0, 0]
    xt = jnp.where(i == 0, ctx_ref[0], x_ref[0])
    xe = jnp.concatenate([xt, xp_ref[0], xn_ref[0]], axis=0)
    h = xe * (1.0 + m[1:2]) + m[0:1]
    fe = _bdot(h, win_ref[...])
    f = fe[:TM]
    o_kv = MLA_Q_RANK
    o_kr = o_kv + MLA_KV_RANK
    o_rw = o_kr + 2 * HEAD_PAD
    prow = jnp.where(has_prev, fe[TM + SUB - 1:TM + SUB, o_rw:], 0.0)
    nrow = jnp.where(has_next, fe[TM + SUB:TM + SUB + 1, o_rw:], 0.0)
    _rw_features(f[:, o_rw:], prow, nrow, mu_ref, w0_ref, w2_ref, a0_ref, a2_ref, g2_ref, kk_ref, ka_ref, rk_ref,
                 ones_ref, sf_ref, g_ref, bonus_ref)
    fq = f[:, :o_kv]
    fkv = f[:, o_kv:o_kr]
    fkr = f[:, o_kr:o_kr + HEAD_PAD]
    fkr_rot = f[:, o_kr + HEAD_PAD:o_rw]
    qn = fq * lax.rsqrt(jnp.mean(fq * fq, axis=-1, keepdims=True) + RMS_EPS) * qn_ref[...]
    ckv = fkv * lax.rsqrt(jnp.mean(fkv * fkv, axis=-1, keepdims=True) + RMS_EPS) * kvn_ref[...]
    qn = qn.astype(BF16)
    ckv = ckv.astype(BF16)
    qa = jnp.dot(qn, wqa_ref[...], preferred_element_type=F32)
    qb = jnp.dot(qn, wqb_ref[...], preferred_element_type=F32)
    kn = jnp.dot(ckv, wuk_ref[...], preferred_element_type=F32)
    cs = cs_ref[...]
    sn = sn_ref[...]
    kr = fkr * cs + fkr_rot * sn
    for hd in range(HEADS):
        sl = slice(hd * HEAD_PAD, (hd + 1) * HEAD_PAD)
        q_ref[0, :, sl] = ((qa[:, sl] * cs + qb[:, sl] * sn) * (ATT_SCALE * LOG2_E)).astype(BF16)
        k_ref[0, :, sl] = (kn[:, sl] + kr).astype(BF16)
    v_ref[0] = jnp.dot(ckv, wuv_ref[...], preferred_element_type=F32).astype(BF16)


def _seg_specs(n):
    ctx_spec = pl.BlockSpec((1, TM, D), lambda bb, i: (bb, 0, 0))
    lat_spec = pl.BlockSpec((1, TM, D), lambda bb, i: (bb, jnp.maximum(i - 1, 0), 0))
    return ctx_spec, lat_spec


def _front0(ctx, x, modt, mla_consts, cs, sn, rw_consts):
    b, n, _ = x.shape
    t = n + TM
    row = lambda w: pl.BlockSpec((1, TM, w), lambda bb, i: (bb, i, 0))
    ctx_spec, lat_spec = _seg_specs(n)
    hb = TM // SUB
    prev = pl.BlockSpec((1, SUB, D), lambda bb, i: (bb, jnp.maximum((i - 1) * hb - 1, 0), 0))
    nxt = pl.BlockSpec((1, SUB, D), lambda bb, i: (bb, jnp.minimum(i * hb, n // SUB - 1), 0))
    rope = pl.BlockSpec((TM, HEAD_PAD), lambda bb, i: (i, 0))
    return pl.pallas_call(
        _front0_kernel,
        grid=(b, t // TM),
        in_specs=[ctx_spec, lat_spec, prev, nxt, _mod_spec(True)] + [_weight_spec(c) for c in mla_consts]
                 + [rope, rope] + [_weight_spec(c) for c in rw_consts],
        out_specs=[row(HEADS * HEAD_PAD), row(HEADS * HEAD_PAD), row(HEADS * MLA_V),
                   pl.BlockSpec((1, 9, HEADS * PITCH_F, TM), lambda bb, i: (bb, 0, 0, i)), row(RW_W), row(RW_W)],
        out_shape=[jax.ShapeDtypeStruct((b, t, HEADS * HEAD_PAD), BF16),
                   jax.ShapeDtypeStruct((b, t, HEADS * HEAD_PAD), BF16),
                   jax.ShapeDtypeStruct((b, t, HEADS * MLA_V), BF16),
                   jax.ShapeDtypeStruct((b, 9, HEADS * PITCH_F, t), F32),
                   jax.ShapeDtypeStruct((b, t, RW_W), F32),
                   jax.ShapeDtypeStruct((b, t, RW_W), F32)],
        compiler_params=_params(("parallel", "parallel")),
        name="l0_front",
    )(ctx, x, x, x, modt, *mla_consts, cs, sn, *rw_consts)


def _att_kernel(q_ref, k_ref, v_ref, o_ref, *, n_ctx, n_all):
    i = pl.program_id(2)

    def attend(nk):
        lane = lax.broadcasted_iota(jnp.int32, (nk, 2 * MLA_V), 1)
        for pair in range(ATT_HEADS // 2):
            v = v_ref[0, :nk, pair * 2 * MLA_V:(pair + 1) * 2 * MLA_V]
            acc = None
            for hh in range(2):
                hd = 2 * pair + hh
                sl = slice(hd * HEAD_PAD, (hd + 1) * HEAD_PAD)
                q = q_ref[0, :, sl]
                k = k_ref[0, :nk, sl]
                s = lax.dot_general(q, k, (((1,), (1,)), ((), ())), preferred_element_type=F32)
                p = jnp.exp2(s - jnp.max(s, axis=-1, keepdims=True))
                l = jnp.sum(p, axis=-1, keepdims=True)
                keep = (lane < MLA_V) if hh == 0 else (lane >= MLA_V)
                vh = jnp.where(keep, v, jnp.zeros_like(v))
                o = jnp.dot(p.astype(BF16), vh, preferred_element_type=F32) / l
                acc = o if acc is None else acc + o
            o_ref[0, :, pair * 2 * MLA_V:(pair + 1) * 2 * MLA_V] = acc.astype(o_ref.dtype)

    @pl.when(i == 0)
    def _():
        attend(n_ctx)

    @pl.when(i > 0)
    def _():
        attend(n_all)


def _attention(q, k, v):
    b, t, _ = q.shape
    return pl.pallas_call(
        functools.partial(_att_kernel, n_ctx=TM, n_all=t),
        grid=(b, HEADS // ATT_HEADS, t // TM),
        in_specs=[pl.BlockSpec((1, TM, ATT_HEADS * HEAD_PAD), lambda bb, hp, i: (bb, i, hp)),
                  pl.BlockSpec((1, t, ATT_HEADS * HEAD_PAD), lambda bb, hp, i: (bb, 0, hp)),
                  pl.BlockSpec((1, t, ATT_HEADS * MLA_V), lambda bb, hp, i: (bb, 0, hp))],
        out_specs=pl.BlockSpec((1, TM, ATT_HEADS * MLA_V), lambda bb, hp, i: (bb, i, hp)),
        out_shape=jax.ShapeDtypeStruct((b, t, HEADS * MLA_V), BF16),
        compiler_params=_params(("parallel", "parallel", "parallel")),
        name="l0_attention",
    )(q, k, v)


def _halo_flags(i, n_tiles):
    return i >= 2, jnp.logical_and(i >= 1, i < n_tiles - 1)


def _rw_features(f, prow, nrow, mu_ref, w0_ref, w2_ref, a0_ref, a2_ref, g2_ref,
                 kk_ref, ka_ref, rk_ref, ones_ref, sf_ref, g_ref, bonus_ref):
    rid = lax.broadcasted_iota(jnp.int32, (TM, 1), 0)
    prev = jnp.where(rid == 0, prow, pltpu.roll(f, 1, axis=0))
    nxt = jnp.where(rid == TM - 1, nrow, pltpu.roll(f, TM - 1, axis=0))
    f = f + mu_ref[...] * (0.5 * (prev + nxt) - f)
    r = f[:, 0:RW_W]
    k = f[:, RW_W:2 * RW_W]
    v = f[:, 2 * RW_W:3 * RW_W]
    o = 3 * RW_W
    wl = f[:, o:o + 2 * LORA_W]
    al = f[:, o + 2 * LORA_W:o + 2 * LORA_W + 2 * LORA_A]
    gl = f[:, o + 2 * LORA_W + 2 * LORA_A:]
    w_raw = w0_ref[...] + _bdot(jnp.tanh(wl), w2_ref[...])
    decay = jnp.exp(-EXP_NEG_HALF * _sigmoid(w_raw))
    a = _sigmoid(a0_ref[...] + _bdot(al, a2_ref[...]))
    g_ref[0] = _bdot(_sigmoid(gl), g2_ref[...])
    ones_bd = ones_ref[...]
    kk = k * kk_ref[...]
    kk = kk * lax.rsqrt(_head_sum(kk * kk, ones_bd) + 1e-12)
    ka = ka_ref[...]
    kd = [k * (1.0 + (a[:, d * RW_W:(d + 1) * RW_W] - 1.0) * ka) for d in range(2)]
    bonus_ref[0] = _head_sum(r * (kd[0] + kd[1]) * rk_ref[...], ones_bd) * v
    parts = [r, kk, v, decay[:, :RW_W], kd[0], kk * a[:, :RW_W],
             decay[:, RW_W:], kd[1], kk * a[:, RW_W:]]
    pad = jnp.zeros((PITCH_F - RW_HEAD, TM), F32)
    for n, p in enumerate(parts):
        for hp in range(HEADS // 2):
            for half in range(TM // 128):
                pt = p[half * 128:(half + 1) * 128, hp * 128:(hp + 1) * 128].T
                for h2 in range(2):
                    row0 = (2 * hp + h2) * PITCH_F
                    sf_ref[0, n, row0:row0 + RW_HEAD, half * 128:(half + 1) * 128] = pt[h2 * RW_HEAD:(h2 + 1) * RW_HEAD]
        for hd in range(HEADS):
            sf_ref[0, n, hd * PITCH_F + RW_HEAD:(hd + 1) * PITCH_F, :] = pad


def _time_mirror(i, n_ctx_tiles, n_tiles):
    return jnp.where(i < n_ctx_tiles, n_ctx_tiles - 1 - i, n_tiles + n_ctx_tiles - 1 - i)


def _reverse_lanes(x, j3):
    hi = x.astype(BF16)
    r1 = x - hi.astype(F32)
    mid = r1.astype(BF16)
    lo = (r1 - mid.astype(F32)).astype(BF16)
    return jnp.dot(jnp.concatenate([hi, mid, lo], axis=1), j3, preferred_element_type=F32)


def _relayout_in_kernel(xa_ref, xb_ref, j3_ref, o_ref, *, step_major):
    nb = xa_ref.shape[0]
    tr = xa_ref.shape[3]
    half = nb * HEADS

    def gather(x_ref, f):
        return jnp.concatenate([x_ref[b, 0, pl.ds(f, HEADS, stride=PITCH_F), :] for b in range(nb)], axis=0)

    for f0 in range(0, RW_HEAD, RELAY_FB):
        fs = range(f0, f0 + RELAY_FB)
        rev = _reverse_lanes(jnp.concatenate([gather(xb_ref, f) for f in fs], axis=0), j3_ref[...])
        for n, f in enumerate(fs):
            mt = jnp.concatenate([gather(xa_ref, f), rev[n * half:(n + 1) * half]], axis=0).T
            if step_major:
                o_ref[pl.ds(f, tr, stride=PITCH_F), :] = mt
            else:
                o_ref[0, f] = mt
    if step_major:
        for f in range(RW_HEAD, PITCH_F):
            o_ref[pl.ds(f, tr, stride=PITCH_F), :] = jnp.zeros((tr, 2 * half), F32)


def _relayout_in(sf, j3):
    b, _, rows, t = sf.shape
    chains = 2 * b * HEADS
    nt = t // RELAY_T
    nc = TM // RELAY_T
    mirror = lambda i: _time_mirror(i, nc, nt)
    blk = (b, 1, rows, RELAY_T)
    dirdep = lambda p: jnp.minimum(p // 2, 1)
    x = pl.pallas_call(
        functools.partial(_relayout_in_kernel, step_major=False),
        grid=(nt, 5),
        in_specs=[pl.BlockSpec(blk, lambda i, p: (0, p + dirdep(p), 0, i)),
                  pl.BlockSpec(blk, lambda i, p: (0, p + 4 * dirdep(p), 0, mirror(i))),
                  _const_spec(j3.shape)],
        out_specs=pl.BlockSpec((1, RW_HEAD, RELAY_T, chains), lambda i, p: (p, 0, i, 0)),
        out_shape=jax.ShapeDtypeStruct((5, RW_HEAD, t, chains), F32),
        compiler_params=_params(("parallel", "parallel")),
        name="l0_rwkv_relayout_in",
    )(sf, sf, j3)
    v = pl.pallas_call(
        functools.partial(_relayout_in_kernel, step_major=True),
        grid=(nt,),
        in_specs=[pl.BlockSpec(blk, lambda i: (0, 2, 0, i)),
                  pl.BlockSpec(blk, lambda i: (0, 2, 0, mirror(i))),
                  _const_spec(j3.shape)],
        out_specs=pl.BlockSpec((RELAY_T * PITCH_F, chains), lambda i: (i, 0)),
        out_shape=jax.ShapeDtypeStruct((t * PITCH_F, chains), F32),
        compiler_params=_params(("parallel",)),
        name="l0_rwkv_relayout_v",
    )(sf, sf, j3)
    return x, v


def _rwscan_kernel(x_ref, v_ref, y_ref, s_ref):
    tc = x_ref.shape[2]
    chains = x_ref.shape[3]

    @pl.when(pl.program_id(0) == 0)
    def _():
        s_ref[...] = jnp.zeros_like(s_ref)

    groups = RW_HEAD // SUB
    zeros = tuple(jnp.zeros((SUB, chains), F32) for _ in range(groups))

    def row(a, j, s):
        return jnp.broadcast_to(x_ref[a, j, pl.ds(s, 1), :], (SUB, chains))

    def s_at(j, g):
        return s_ref.at[j, g * SUB:(g + 1) * SUB, :]

    sa0 = list(zeros)
    for j in range(RW_HEAD):
        kk = row(1, j, 0)
        for g in range(groups):
            sa0[g] = sa0[g] + s_at(j, g)[...] * kk

    def step(s, sa):
        base = pl.multiple_of(s * PITCH_F, SUB)
        v = v_ref[pl.ds(base, RW_HEAD), :]
        s_next = jnp.minimum(s + 1, tc - 1)
        y = list(zeros)
        sa_next = list(zeros)
        for j in range(RW_HEAD):
            r, kk_next, w, k, kka = row(0, j, s), row(1, j, s_next), row(2, j, s), row(3, j, s), row(4, j, s)
            for g in range(groups):
                sj = s_at(j, g)[...] * w - sa[g] * kka + v[g * SUB:(g + 1) * SUB] * k
                s_at(j, g)[...] = sj
                y[g] = y[g] + sj * r
                sa_next[g] = sa_next[g] + sj * kk_next
        y_ref[pl.ds(base, RW_HEAD), :] = jnp.concatenate(y, axis=0)
        y_ref[pl.ds(base + RW_HEAD, PITCH_F - RW_HEAD), :] = zeros[0]
        return tuple(sa_next)

    lax.fori_loop(0, tc, step, tuple(sa0))


def _rwscan(x, v):
    _, _, t, chains = x.shape
    tc = SCAN_TC if t % SCAN_TC == 0 else SCAN_TC_SMALL
    return pl.pallas_call(
        _rwscan_kernel,
        grid=(t // tc,),
        in_specs=[pl.BlockSpec((5, RW_HEAD, tc, chains), lambda i: (0, 0, i, 0)),
                  pl.BlockSpec((tc * PITCH_F, chains), lambda i: (i, 0))],
        out_specs=pl.BlockSpec((tc * PITCH_F, chains), lambda i: (i, 0)),
        out_shape=jax.ShapeDtypeStruct((t * PITCH_F, chains), F32),
        scratch_shapes=[pltpu.VMEM((RW_HEAD, RW_HEAD, chains), F32)],
        compiler_params=_params(("arbitrary",)),
        name="l0_rwkv_scan",
    )(x, v)


def _relayout_out_kernel(yf_ref, yr_ref, j3_ref, o_ref, q_ref):
    nb = o_ref.shape[0]
    tr = o_ref.shape[1]
    chains = yf_ref.shape[1]
    half = chains // 2
    pitch_c = chains + SUB
    fwd = lax.broadcasted_iota(jnp.int32, (tr, chains), 1) < half
    for i in range(RW_HEAD):
        rows = pl.ds(i, tr, stride=PITCH_F)
        mt = jnp.where(fwd, yf_ref[rows, :], yr_ref[rows, :]).T
        q_ref[i * pitch_c:i * pitch_c + half, :] = mt[:half]
        q_ref[i * pitch_c + half:i * pitch_c + chains, :] = _reverse_lanes(mt[half:], j3_ref[...])
    for b in range(nb):
        for hp in range(HEADS // 2):
            parts = []
            for h2 in range(2):
                c = b * HEADS + 2 * hp + h2
                parts.append(q_ref[pl.ds(c, RW_HEAD, stride=pitch_c), :]
                             + q_ref[pl.ds(half + c, RW_HEAD, stride=pitch_c), :])
            o_ref[b, :, hp * 128:(hp + 1) * 128] = jnp.concatenate(parts, axis=0).T


def _relayout_out(y, j3, b):
    chains = y.shape[1]
    t = y.shape[0] // PITCH_F
    nt = t // RELAY_T
    nc = TM // RELAY_T
    return pl.pallas_call(
        _relayout_out_kernel,
        grid=(nt,),
        in_specs=[pl.BlockSpec((RELAY_T * PITCH_F, chains), lambda i: (i, 0)),
                  pl.BlockSpec((RELAY_T * PITCH_F, chains), lambda i: (_time_mirror(i, nc, nt), 0)),
                  _const_spec(j3.shape)],
        out_specs=pl.BlockSpec((b, RELAY_T, RW_W), lambda i: (0, i, 0)),
        out_shape=jax.ShapeDtypeStruct((b, t, RW_W), F32),
        scratch_shapes=[pltpu.VMEM((RW_HEAD * (chains + SUB), RELAY_T), F32)],
        compiler_params=_params(("parallel",)),
        name="l0_rwkv_relayout_out",
    )(y, y, j3)


def _rwkv_scan_both(sf):
    k = jnp.arange(RELAY_T)
    anti = (k[:, None] + k[None, :] == RELAY_T - 1).astype(BF16)
    j3 = jnp.concatenate([anti, anti, anti], axis=0)
    y = _rwscan(*_relayout_in(sf, j3))
    return _relayout_out(y, j3, sf.shape[0])


def _mlp_hidden(h, w1_ref, w2_ref):
    acc = None
    for c in range(D_FF // MLP_CHUNK):
        u = jnp.dot(h, w1_ref[:, c * MLP_CHUNK:(c + 1) * MLP_CHUNK], preferred_element_type=F32)
        u = jnp.square(jnp.maximum(u, 0.0)).astype(BF16)
        part = jnp.dot(u, w2_ref[c * MLP_CHUNK:(c + 1) * MLP_CHUNK, :], preferred_element_type=F32)
        acc = part if acc is None else acc + part
    return acc


def _tail0_kernel(att_ref, ys_ref, bonus_ref, g_ref, ctx_ref, x_ref, mod_ref, mod1_ref, gnw_ref, gnb_ref, ones_ref,
                  woa_ref, wob_ref, l1g_ref, l1b_ref, w1_ref, w2_ref, l2g_ref, l2b_ref, win1_ref,
                  x2_ref, gate_ref, xr_ref):
    m = mod_ref[0, 0]
    ones_bd = ones_ref[...]
    y = ys_ref[0]
    mu = _head_sum(y, ones_bd) * (1.0 / RW_HEAD)
    yc = y - mu
    var = _head_sum(yc * yc, ones_bd) * (1.0 / RW_HEAD)
    yn = yc * lax.rsqrt(var + GN_EPS) * gnw_ref[...] + gnb_ref[...]
    rw = (yn + bonus_ref[0]) * g_ref[0]
    o = (jnp.dot(att_ref[0], woa_ref[...], preferred_element_type=F32)
         + _bdot(rw, wob_ref[...]))
    x0 = jnp.where(pl.program_id(1) == 0, ctx_ref[0], x_ref[0])
    x1 = _layer_norm(ALPHA * x0 + m[2:3] * o, l1g_ref[...], l1b_ref[...])
    acc = _mlp_hidden((x1 * (1.0 + m[4:5]) + m[3:4]).astype(BF16), w1_ref, w2_ref)
    x2 = _layer_norm(ALPHA * x1 + m[5:6] * acc, l2g_ref[...], l2b_ref[...])
    x2_ref[0] = x2
    m1 = mod1_ref[0, 0]
    f = _bdot(x2 * (1.0 + m1[1:2]) + m1[0:1], win1_ref[...])
    gate_ref[0] = f[:, :LRU_W]
    xr_ref[0] = f[:, LRU_W:]


def _tail0(att, ys, bonus, g, ctx, x, mod0, mod1, consts):
    b, n, _ = x.shape
    t = n + TM
    row = lambda w: pl.BlockSpec((1, TM, w), lambda bb, i: (bb, i, 0))
    ctx_spec, lat_spec = _seg_specs(n)
    return pl.pallas_call(
        _tail0_kernel,
        grid=(b, t // TM),
        in_specs=[row(RW_W), row(RW_W), row(RW_W), row(RW_W), ctx_spec, lat_spec, _mod_spec(True), _mod_spec(True)]
                 + [_weight_spec(c) for c in consts],
        out_specs=[row(D), row(LRU_W), row(LRU_W)],
        out_shape=[jax.ShapeDtypeStruct((b, t, D), F32), jax.ShapeDtypeStruct((b, t, LRU_W), F32),
                   jax.ShapeDtypeStruct((b, t, LRU_W), F32)],
        compiler_params=_params(("parallel", "parallel")),
        name="l0_tail",
    )(att, ys, bonus, g, ctx, x, mod0, mod1, *consts)


def _lru_fwd_kernel(x_ref, xp_ref, xn_ref, cw_ref, cb_ref, wbd_ref, gb_ref, lam_ref,
                    hf_ref, a1_ref, u1_ref, a_scr, u_scr, c_ref):
    i = pl.program_id(0)
    n = pl.num_programs(0)
    nb, tl, _ = x_ref.shape
    rows = nb * tl
    nc = TM // tl
    pitch_b = tl + SUB

    @pl.when(i == 0)
    def _():
        c_ref[...] = jnp.zeros_like(c_ref)

    lam = lam_ref[...]
    nl = -lam
    softplus = jnp.maximum(nl, 0.0) + jnp.log1p(jnp.exp(-jnp.abs(nl)))
    gb = gb_ref[...]
    cw = cw_ref[...]
    tpos = jnp.bitwise_and(lax.broadcasted_iota(jnp.int32, (rows, 1), 0), tl - 1)
    seg_first = jnp.logical_or(i == 0, i == nc)
    seg_last = jnp.logical_or(i == nc - 1, i == n - 1)

    def halo(ref, r, edge):
        per_b = [jnp.broadcast_to(ref[b, r:r + 1, :], (tl, LRU_W)) for b in range(nb)]
        return jnp.where(edge, 0.0, jnp.concatenate(per_b, axis=0))

    x = x_ref[...].reshape(rows, LRU_W)
    p1 = halo(xp_ref, SUB - 1, seg_first)
    p2 = halo(xp_ref, SUB - 2, seg_first)
    n1 = halo(xn_ref, 0, seg_last)
    xm1 = jnp.where(tpos == 0, p1, pltpu.roll(x, 1, axis=0))
    xm2 = jnp.where(tpos == 0, p2, jnp.where(tpos == 1, p1, pltpu.roll(x, 2, axis=0)))
    xp1 = jnp.where(tpos == tl - 1, n1, pltpu.roll(x, rows - 1, axis=0))
    xc = cw[0:1] * xm2 + cw[1:2] * xm1 + cw[2:3] * x + cw[3:4] * xp1 + cb_ref[...]
    for blk in range(LRU_BLOCKS):
        sl = slice(blk * LRU_BLOCK, (blk + 1) * LRU_BLOCK)
        xb = xc[:, sl]
        z = _bdot(xb, wbd_ref[blk])
        for d in range(2):
            zr = z[:, 2 * d * LRU_BLOCK:(2 * d + 1) * LRU_BLOCK] + gb[2 * d:2 * d + 1, sl]
            zi = z[:, (2 * d + 1) * LRU_BLOCK:(2 * d + 2) * LRU_BLOCK] + gb[2 * d + 1:2 * d + 2, sl]
            log_a = -LRU_C * _sigmoid(zr) * softplus[d:d + 1, sl]
            a = jnp.exp(log_a)
            u = jnp.sqrt(-jnp.tanh(log_a) * (a * a + 1.0)) * (_sigmoid(zi) * xb)
            for b in range(nb):
                a_scr[d, blk, b * pitch_b:b * pitch_b + tl, :] = a[b * tl:(b + 1) * tl]
                u_scr[d, blk, b * pitch_b:b * pitch_b + tl, :] = u[b * tl:(b + 1) * tl]
    h = [c_ref[:, blk * LRU_BLOCK:(blk + 1) * LRU_BLOCK] for blk in range(LRU_BLOCKS)]
    pad = jnp.zeros((SUB, LRU_BLOCK), F32)
    for t in range(tl):
        for blk in range(LRU_BLOCKS):
            step_rows = pl.ds(t, nb, stride=pitch_b)
            dst = slice(t * PITCH_H + blk * SUB, t * PITCH_H + blk * SUB + nb)
            h[blk] = a_scr[0, blk, step_rows, :] * h[blk] + u_scr[0, blk, step_rows, :]
            hf_ref[dst, :] = h[blk]
            a1_ref[dst, :] = a_scr[1, blk, step_rows, :]
            u1_ref[dst, :] = u_scr[1, blk, step_rows, :]
        for ref in (hf_ref, a1_ref, u1_ref):
            ref[t * PITCH_H + LRU_BLOCKS * SUB:(t + 1) * PITCH_H, :] = pad
    for blk in range(LRU_BLOCKS):
        c_ref[:, blk * LRU_BLOCK:(blk + 1) * LRU_BLOCK] = h[blk]


def _lru_bwd_kernel(a_ref, u_ref, h_ref, c_ref, *, nb):
    tb = a_ref.shape[0] // PITCH_H

    @pl.when(pl.program_id(0) == 0)
    def _():
        c_ref[...] = jnp.zeros_like(c_ref)

    def step(s, h):
        base = pl.multiple_of((tb - 1 - s) * PITCH_H, SUB)
        out = []
        for blk in range(LRU_BLOCKS):
            rows = pl.ds(base + blk * SUB, nb)
            hb = a_ref[rows, :] * h[blk] + u_ref[rows, :]
            h_ref[rows, :] = hb
            out.append(hb)
        h_ref[pl.ds(base + LRU_BLOCKS * SUB, SUB), :] = jnp.zeros((SUB, LRU_BLOCK), F32)
        return tuple(out)

    h0 = tuple(c_ref[blk] for blk in range(LRU_BLOCKS))
    h = lax.fori_loop(0, tb, step, h0)
    for blk in range(LRU_BLOCKS):
        c_ref[blk] = h[blk]


def _lru(xr, cw, cb, wbd, gb, lam):
    b, t, _ = xr.shape
    assert b <= SUB
    n = t // LRU_TL
    hb = LRU_TL // SUB
    last = t // SUB - 1
    consts = (cw, cb, wbd, gb, lam)
    hspec = pl.BlockSpec((LRU_TL * PITCH_H, LRU_BLOCK), lambda i: (i, 0))
    coef = jax.ShapeDtypeStruct((t * PITCH_H, LRU_BLOCK), F32)
    scr = pltpu.VMEM((2, LRU_BLOCKS, b * (LRU_TL + SUB), LRU_BLOCK), F32)
    hf, a1, u1 = pl.pallas_call(
        _lru_fwd_kernel,
        grid=(n,),
        in_specs=[pl.BlockSpec((b, LRU_TL, LRU_W), lambda i: (0, i, 0)),
                  pl.BlockSpec((b, SUB, LRU_W), lambda i: (0, jnp.maximum(i * hb - 1, 0), 0)),
                  pl.BlockSpec((b, SUB, LRU_W), lambda i: (0, jnp.minimum((i + 1) * hb, last), 0))]
                 + [_const_spec(c.shape) for c in consts],
        out_specs=[hspec, hspec, hspec],
        out_shape=[coef, coef, coef],
        scratch_shapes=[scr, scr, pltpu.VMEM((b, LRU_W), F32)],
        compiler_params=_params(("arbitrary",)),
        name="l1_lru_fwd",
    )(xr, xr, xr, *consts)
    nt = t // LRU_TB
    nc = TM // LRU_TB
    bspec = pl.BlockSpec((LRU_TB * PITCH_H, LRU_BLOCK), lambda i: (_time_mirror(i, nc, nt), 0))
    hr = pl.pallas_call(
        functools.partial(_lru_bwd_kernel, nb=b),
        grid=(nt,),
        in_specs=[bspec, bspec],
        out_specs=bspec,
        out_shape=coef,
        scratch_shapes=[pltpu.VMEM((LRU_BLOCKS, b, LRU_BLOCK), F32)],
        compiler_params=_params(("arbitrary",)),
        name="l1_lru_bwd",
    )(a1, u1)
    return hf, hr


def _tail1_kernel(gate_ref, hf_ref, hr_ref, x_ref, mod_ref, w_ref, l1g_ref, l1b_ref, w1_ref, w2_ref, l2g_ref, l2b_ref,
                  o_ref):
    nb, tl, _ = gate_ref.shape
    per_b = []
    for b in range(nb):
        cols = []
        for blk in range(LRU_BLOCKS):
            rows = pl.ds(blk * SUB + b, tl, stride=PITCH_H)
            cols.append(hf_ref[rows, :] + hr_ref[rows, :])
        per_b.append(jnp.concatenate(cols, axis=1))
    h = jnp.concatenate(per_b, axis=0)
    gate = gate_ref[...].reshape(nb * tl, LRU_W)
    gelu = 0.5 * gate * (1.0 + jnp.tanh(math.sqrt(2.0 / math.pi) * (gate + 0.044715 * gate * gate * gate)))
    o = _bdot(gelu * h, w_ref[...])
    mods = [mod_ref[b, 0] for b in range(nb)]
    x1 = [_layer_norm(ALPHA * x_ref[b] + mods[b][2:3] * o[b * tl:(b + 1) * tl], l1g_ref[...], l1b_ref[...])
          for b in range(nb)]
    hm = jnp.concatenate([x1[b] * (1.0 + mods[b][4:5]) + mods[b][3:4] for b in range(nb)], axis=0)
    acc = _mlp_hidden(hm.astype(BF16), w1_ref, w2_ref)
    for b in range(nb):
        z = ALPHA * x1[b] + mods[b][5:6] * acc[b * tl:(b + 1) * tl]
        o_ref[b] = _layer_norm(z, l2g_ref[...], l2b_ref[...])


def _tail1(gate, hf, hr, xc, modt, consts):
    b, t, _ = xc.shape
    off = TM // LRU_TL
    nt = t // LRU_TL - off
    lat = lambda wd: pl.BlockSpec((b, LRU_TL, wd), lambda i: (0, i + off, 0))
    hspec = pl.BlockSpec((LRU_TL * PITCH_H, LRU_BLOCK), lambda i: (i + off, 0))
    return pl.pallas_call(
        _tail1_kernel,
        grid=(nt,),
        in_specs=[lat(LRU_W), hspec, hspec, lat(D),
                  pl.BlockSpec((b, 1, SUB, D), lambda i: (0, 1, 0, 0))]
                 + [_weight_spec(c) for c in consts],
        out_specs=pl.BlockSpec((b, LRU_TL, D), lambda i: (0, i, 0)),
        out_shape=jax.ShapeDtypeStruct((b, nt * LRU_TL, D), F32),
        compiler_params=_params(("parallel",)),
        name="l1_tail",
    )(gate, hf, hr, xc, modt, *consts)


def _rot_cols(w):
    ws = w.reshape(w.shape[:-1] + (2, 2, ROPE_AXIS // 2))
    return jnp.stack([-ws[..., 1, :], ws[..., 0, :]], axis=-2).reshape(w.shape)


def _rope_tables(n, n_ctx):
    rows_n = n // GRID_W
    rows = jnp.repeat(jnp.arange(rows_n, dtype=F32), GRID_W)
    cols = jnp.tile(jnp.arange(GRID_W, dtype=F32), rows_n)
    inv_freq = ROPE_THETA ** (-jnp.arange(0, ROPE_AXIS, 2, dtype=F32) / ROPE_AXIS)
    ang_r = rows[:, None] * inv_freq
    ang_c = cols[:, None] * inv_freq
    ang = jnp.concatenate([ang_r, ang_r, ang_c, ang_c], axis=-1)
    cos = jnp.concatenate([jnp.ones((n_ctx, MLA_ROPE), F32), jnp.cos(ang)], axis=0)
    sin = jnp.concatenate([jnp.zeros((n_ctx, MLA_ROPE), F32), jnp.sin(ang)], axis=0)
    t = n + n_ctx
    cs = jnp.concatenate([jnp.ones((t, MLA_NOPE), F32), cos, jnp.zeros((t, 32), F32)], axis=-1)
    sn = jnp.concatenate([jnp.zeros((t, MLA_NOPE), F32), sin, jnp.zeros((t, 32), F32)], axis=-1)
    return cs, sn


def _block_diag2(w):
    z = jnp.zeros_like(w[0])
    return jnp.concatenate([jnp.concatenate([w[0], z], axis=1), jnp.concatenate([z, w[1]], axis=1)], axis=0)


def kernel(x, c, ctx, c_ctx, l0_mod_w, l0_mod_b, l0_w_in, l0_mla_q_norm, l0_mla_w_uq, l0_mla_kv_norm, l0_mla_w_uk, l0_mla_w_uv, l0_rwkv_mu, l0_rwkv_w0, l0_rwkv_w2, l0_rwkv_a0, l0_rwkv_a2, l0_rwkv_g2, l0_rwkv_k_k, l0_rwkv_k_a, l0_rwkv_r_k, l0_rwkv_gn_w, l0_rwkv_gn_b, l0_w_out, l0_ln1_g, l0_ln1_b, l0_mlp_w1, l0_mlp_w2, l0_ln2_g, l0_ln2_b, l1_mod_w, l1_mod_b, l1_w_in, l1_conv_w, l1_conv_b, l1_lru_ga_w, l1_lru_ga_b, l1_lru_gx_w, l1_lru_gx_b, l1_lru_lambda, l1_w_out, l1_ln1_g, l1_ln1_b, l1_mlp_w1, l1_mlp_w2, l1_ln2_g, l1_ln2_b):
    b, n, _ = x.shape
    n_ctx = ctx.shape[1]
    assert n_ctx == TM and n % TM == 0 and x.shape[2] == D
    row = lambda v: v.reshape(1, -1)

    mod0 = _mod_table(c, c_ctx, l0_mod_w, l0_mod_b)
    mod1 = _mod_table(c, c_ctx, l1_mod_w, l1_mod_b)

    o_kv = MLA_Q_RANK
    o_kr = o_kv + MLA_KV_RANK
    o_rw = o_kr + MLA_ROPE
    w_kr = l0_w_in[:, o_kr:o_rw]
    zl = jnp.zeros((D, MLA_NOPE), F32)
    zr = jnp.zeros((D, HEAD_PAD - MLA_NOPE - MLA_ROPE), F32)
    win0 = jnp.concatenate([l0_w_in[:, :o_kr], zl, w_kr, zr, zl, _rot_cols(w_kr), zr,
                            l0_w_in[:, o_rw:]], axis=1).astype(BF16)
    wq = l0_mla_w_uq.reshape(MLA_Q_RANK, HEADS, MLA_NOPE + MLA_ROPE)
    q_nope, q_rope = wq[..., :MLA_NOPE], wq[..., MLA_NOPE:]
    zq = jnp.zeros((MLA_Q_RANK, HEADS, 32), F32)
    wqa = jnp.concatenate([q_nope, q_rope, zq], axis=-1).reshape(MLA_Q_RANK, HEADS * HEAD_PAD).astype(BF16)
    wqb = jnp.concatenate([jnp.zeros_like(q_nope), _rot_cols(q_rope), zq],
                          axis=-1).reshape(MLA_Q_RANK, HEADS * HEAD_PAD).astype(BF16)
    wk = l0_mla_w_uk.reshape(MLA_KV_RANK, HEADS, MLA_NOPE)
    wuk = jnp.concatenate([wk, jnp.zeros_like(wk)], axis=-1).reshape(MLA_KV_RANK, HEADS * HEAD_PAD).astype(BF16)
    wuv = l0_mla_w_uv.astype(BF16)
    cs, sn = _rope_tables(n, n_ctx)
    hid = jnp.arange(RW_W) // RW_HEAD
    ones_bd = (hid[:, None] == hid[None, :]).astype(BF16)

    mla_consts = (win0, row(l0_mla_q_norm), row(l0_mla_kv_norm), wqa, wqb, wuk, wuv)
    rw_consts = (row(l0_rwkv_mu), row(l0_rwkv_w0), _block_diag2(l0_rwkv_w2).astype(BF16), row(l0_rwkv_a0),
                 _block_diag2(l0_rwkv_a2).astype(BF16), l0_rwkv_g2.astype(BF16), row(l0_rwkv_k_k),
                 row(l0_rwkv_k_a), row(l0_rwkv_r_k), ones_bd)
    q, k, v, sf, g, bonus = _front0(ctx, x, mod0, mla_consts, cs, sn, rw_consts)
    att = _attention(q, k, v)
    ys = _rwkv_scan_both(sf)
    wo = l0_w_out.astype(BF16)
    tail0_consts = (row(l0_rwkv_gn_w), row(l0_rwkv_gn_b), ones_bd, wo[:HEADS * MLA_V], wo[HEADS * MLA_V:],
                    row(l0_ln1_g), row(l0_ln1_b), l0_mlp_w1.astype(BF16), l0_mlp_w2.astype(BF16),
                    row(l0_ln2_g), row(l0_ln2_b), l1_w_in.astype(BF16))
    xc, gate, xr = _tail0(att, ys, bonus, g, ctx, x, mod0, mod1, tail0_consts)

    wbd = jnp.concatenate([l1_lru_ga_w[0], l1_lru_gx_w[0], l1_lru_ga_w[1], l1_lru_gx_w[1]], axis=-1).astype(BF16)
    gb = jnp.stack([l1_lru_ga_b[0], l1_lru_gx_b[0], l1_lru_ga_b[1], l1_lru_gx_b[1]])
    hf, hr = _lru(xr, l1_conv_w, row(l1_conv_b), wbd, gb, l1_lru_lambda)
    tail1_consts = (l1_w_out.astype(BF16), row(l1_ln1_g), row(l1_ln1_b), l1_mlp_w1.astype(BF16),
                    l1_mlp_w2.astype(BF16), row(l1_ln2_g), row(l1_ln2_b))
    return _tail1(gate, hf, hr, xc, mod1, tail1_consts)
```
